```python
import jax, jax.numpy as jnp
from jax import lax
import numpy as np

D_MODEL = 4096
BATCH = 4
SEQ = 4096
DEPTH = 4
DEC_BATCH = 1
DEC_SEQ = 8192
PAST_LEN = 128

HEAD_DIM = 128
GRID_W = 64
A_HEADS = 16
A_KV_HEADS = 4
A_GROUP = A_HEADS // A_KV_HEADS
A_HALF_WINDOW = 128
A_BLOCK = 128
B_HEADS = 16
NA_KH = 8
NA_KW = 16
C_HEADS = 32
C_BRANCHES = ((128, 1), (512, 4), (2048, 16))
C_BLOCK = 64
A_Q = A_HEADS * HEAD_DIM
A_KV = A_KV_HEADS * HEAD_DIM
B_W = B_HEADS * HEAD_DIM
AB_IN = A_Q + 2 * A_KV + 3 * B_W
AB_OUT = A_Q + B_W
AB_SPLITS = (A_Q, A_Q + A_KV, A_Q + 2 * A_KV, A_Q + 2 * A_KV + B_W, A_Q + 2 * A_KV + 2 * B_W)
C_W = C_HEADS * HEAD_DIM
N_EXPERTS = 16
N_GROUPS = 4
EXPERTS_PER_GROUP = N_EXPERTS // N_GROUPS
TOP_K = 2
D_EXPERT = 1024
N_EVEN = (DEPTH + 1) // 2
N_ODD = DEPTH // 2
DEEPNORM_ALPHA = (2 * DEPTH) ** 0.25
DEEPNORM_BETA = (8 * DEPTH) ** -0.25
LN_EPS = 1e-5
NEG_INF = -1e30

kernel_name = 'hybrid_bidir_encoder_swa_natten_dilated_grouped_moe'


def alibi_slopes(n):
    return jnp.exp2(-8.0 * jnp.arange(1, n + 1, dtype=jnp.float32) / n)


def layer_norm(x, g, b):
    xf = x.astype(jnp.float32)
    mu = xf.mean(-1, keepdims=True)
    var = jnp.square(xf - mu).mean(-1, keepdims=True)
    y = (xf - mu) * lax.rsqrt(var + LN_EPS) * g.astype(jnp.float32) + b.astype(jnp.float32)
    return y.astype(x.dtype)


def banded_attention(q, k, v, half_w, block, step, slopes, sink):
    n, L, hk, g, hd = q.shape
    nblk = -(-L // block)
    lp = nblk * block
    kb_len = block + 2 * half_w
    qb = jnp.pad(q, ((0, 0), (0, lp - L), (0, 0), (0, 0), (0, 0))).reshape(n, nblk, block, hk, g, hd)
    pad_kv = ((0, 0), (half_w, lp - L + half_w), (0, 0), (0, 0))
    kidx = (jnp.arange(nblk) * block)[:, None] + jnp.arange(kb_len)[None, :]
    kb = jnp.pad(k, pad_kv)[:, kidx]
    vb = jnp.pad(v, pad_kv)[:, kidx]
    s = jnp.einsum('nbqhgd,nbkhd->nhgbqk', qb, kb).astype(jnp.float32) * (hd ** -0.5)
    qpos = jnp.arange(lp).reshape(nblk, block)
    kpos = kidx - half_w
    dist = jnp.abs(kpos[:, None, :] - qpos[:, :, None])
    valid = (dist <= half_w) & ((kpos >= 0) & (kpos < L))[:, None, :]
    s = s - slopes.astype(jnp.float32)[:, :, None, None, None] * (step * dist).astype(jnp.float32)
    s = jnp.where(valid, s, NEG_INF)
    m = s.max(-1)
    if sink is not None:
        sk = sink.astype(jnp.float32)[:, :, None, None]
        m = jnp.maximum(m, sk)
    p = jnp.exp(s - m[..., None])
    l = p.sum(-1)
    if sink is not None:
        l = l + jnp.exp(sk - m)
    o = jnp.einsum('nhgbqk,nbkhd->nbqhgd', p.astype(v.dtype), vb).astype(jnp.float32)
    o = (o / l.transpose(0, 3, 4, 1, 2)[..., None]).astype(v.dtype)
    o = o.reshape(n, lp, hk, g, hd)[:, :L]
    lse = (m + jnp.log(l)).transpose(0, 3, 4, 1, 2).reshape(n, lp, hk, g)[:, :L]
    return o, lse


def to_residues(a, dil):
    b, t, h, hd = a.shape
    return a.reshape(b, t // dil, dil, h, hd).transpose(0, 2, 1, 3, 4).reshape(b * dil, t // dil, h, hd)


def from_residues(a, b, dil):
    rest = a.shape[2:]
    L = a.shape[1]
    a = a.reshape((b, dil, L) + rest)
    a = jnp.swapaxes(a, 1, 2)
    return a.reshape((b, L * dil) + rest)


def dilated_attention(q, k, v, slopes):
    b, t, h, hd = q.shape
    outs, lses = [], []
    for window, dil in C_BRANCHES:
        half = (window // 2) // dil
        o, lse = banded_attention(to_residues(q, dil)[:, :, :, None], to_residues(k, dil), to_residues(v, dil),
                                  half, C_BLOCK, dil, slopes[:, None], None)
        outs.append(from_residues(o[:, :, :, 0], b, dil))
        lses.append(from_residues(lse[..., 0], b, dil))
    w = jax.nn.softmax(jnp.stack(lses), axis=0)
    return jnp.einsum('rbth,rbthd->bthd', w.astype(q.dtype), jnp.stack(outs))


def neighborhood_attention(q, k, v, rpb):
    b, t, h, hd = q.shape
    rows = t // GRID_W
    kh = min(NA_KH, rows)
    r = jnp.arange(rows)
    row_idx = jnp.clip(r - kh // 2, 0, rows - kh)[:, None] + jnp.arange(kh)[None, :]
    grid = lambda a: a.reshape(b, rows, GRID_W, h, hd)
    kg = grid(k)[:, row_idx]
    vg = grid(v)[:, row_idx]
    s = jnp.einsum('brchd,brikhd->bhrcik', grid(q), kg).astype(jnp.float32) * (hd ** -0.5)
    c = jnp.arange(GRID_W)
    cs = jnp.clip(c - NA_KW // 2, 0, GRID_W - NA_KW)
    col_ok = (c[None, :] >= cs[:, None]) & (c[None, :] < cs[:, None] + NA_KW)
    dr = row_idx - r[:, None] + (NA_KH - 1)
    dc = jnp.clip(c[None, :] - c[:, None] + (NA_KW - 1), 0, 2 * NA_KW - 2)
    bias = rpb.astype(jnp.float32)[:, dr[:, None, :, None], dc[None, :, None, :]]
    s = jnp.where(col_ok[:, None, :], s + bias[None], NEG_INF)
    p = jax.nn.softmax(s.reshape(b, h, rows, GRID_W, kh * GRID_W), axis=-1).reshape(s.shape)
    o = jnp.einsum('bhrcik,brikhd->brchd', p.astype(v.dtype), vg)
    return o.reshape(b, t, h, hd)


def mixer_ab(x, w_in, w_out, sink, rpb):
    b, t, _ = x.shape
    qa, ka, va, qb, kb, vb = jnp.split(x @ w_in, AB_SPLITS, axis=-1)
    qa = qa.reshape(b, t, A_KV_HEADS, A_GROUP, HEAD_DIM)
    ka = ka.reshape(b, t, A_KV_HEADS, HEAD_DIM)
    va = va.reshape(b, t, A_KV_HEADS, HEAD_DIM)
    slopes = alibi_slopes(A_HEADS).reshape(A_KV_HEADS, A_GROUP)
    oa, _ = banded_attention(qa, ka, va, A_HALF_WINDOW, A_BLOCK, 1, slopes,
                             sink.reshape(A_KV_HEADS, A_GROUP))
    heads = lambda a: a.reshape(b, t, B_HEADS, HEAD_DIM)
    ob = neighborhood_attention(heads(qb), heads(kb), heads(vb), rpb)
    o = jnp.concatenate([oa.reshape(b, t, A_Q), ob.reshape(b, t, B_W)], axis=-1)
    return o @ w_out


def mixer_c(x, w_in, w_out):
    b, t, _ = x.shape
    q, k, v = jnp.split(x @ w_in, 3, axis=-1)
    heads = lambda a: a.reshape(b, t, C_HEADS, HEAD_DIM)
    o = dilated_attention(heads(q), heads(k), heads(v), alibi_slopes(C_HEADS))
    return o.reshape(b, t, C_W) @ w_out


def grouped_moe(x, router_w, router_b, w1, w3, w2):
    b, t, d = x.shape
    xt = x.reshape(b * t, d)
    logits = (xt @ router_w).astype(jnp.float32) + router_b.astype(jnp.float32)
    probs = jax.nn.softmax(logits, axis=-1)
    grouped = probs.reshape(-1, N_GROUPS, EXPERTS_PER_GROUP)
    group_score = lax.top_k(grouped, TOP_K)[0].sum(-1)
    gsel = jnp.argmax(group_score, axis=-1)
    in_group = jnp.take_along_axis(grouped, gsel[:, None, None], axis=1)[:, 0]
    top_v, top_i = lax.top_k(in_group, TOP_K)
    gate = top_v / top_v.sum(-1, keepdims=True)
    expert_idx = gsel[:, None] * EXPERTS_PER_GROUP + top_i
    combine = (jax.nn.one_hot(expert_idx, N_EXPERTS, dtype=jnp.float32) * gate[..., None]).sum(1)
    combine = combine.astype(x.dtype)
    y = jnp.zeros_like(xt)
    for e in range(N_EXPERTS):
        hdn = jax.nn.silu(xt @ w1[e]) * (xt @ w3[e])
        y = y + combine[:, e:e + 1] * (hdn @ w2[e])
    return y.reshape(b, t, d)


def trunk(x, w_in_ab, w_out_ab, a_sink, b_rpb, w_in_c, w_out_c, router_w, router_b,
          moe_w1, moe_w3, moe_w2, ln_g, ln_b):
    for layer in range(DEPTH):
        i = layer // 2
        if layer % 2 == 0:
            mix = mixer_ab(x, w_in_ab[i], w_out_ab[i], a_sink[i], b_rpb[i])
        else:
            mix = mixer_c(x, w_in_c[i], w_out_c[i])
        x = layer_norm(DEEPNORM_ALPHA * x + mix, ln_g[layer, 0], ln_b[layer, 0])
        ffn = grouped_moe(x, router_w, router_b, moe_w1[layer], moe_w3[layer], moe_w2[layer])
        x = layer_norm(DEEPNORM_ALPHA * x + ffn, ln_g[layer, 1], ln_b[layer, 1])
    return x


def setup_inputs(seed: int = 0) -> dict:
    key = jax.random.key(seed)
    ks = jax.random.split(key, 16)
    f32 = jnp.float32
    beta = DEEPNORM_BETA
    in_scale = jnp.concatenate([
        jnp.ones((A_Q + A_KV,), f32), jnp.full((A_KV,), beta, f32),
        jnp.ones((2 * B_W,), f32), jnp.full((B_W,), beta, f32)])
    c_scale = jnp.concatenate([jnp.ones((2 * C_W,), f32), jnp.full((C_W,), beta, f32)])
    return {
        'x_prompt': jax.random.normal(ks[0], (BATCH, SEQ, D_MODEL), f32),
        'x_sample': jax.random.normal(ks[1], (DEC_BATCH, DEC_SEQ, D_MODEL), f32),
        'w_in_ab': jax.random.normal(ks[2], (N_EVEN, D_MODEL, AB_IN), f32) * (D_MODEL ** -0.5) * in_scale,
        'w_out_ab': jax.random.normal(ks[3], (N_EVEN, AB_OUT, D_MODEL), f32) * (AB_OUT ** -0.5) * beta,
        'a_sink': jax.random.normal(ks[4], (N_EVEN, A_HEADS), f32) * 0.5,
        'b_rpb': jax.random.normal(ks[5], (N_EVEN, B_HEADS, 2 * NA_KH - 1, 2 * NA_KW - 1), f32) * 0.1,
        'w_in_c': jax.random.normal(ks[6], (N_ODD, D_MODEL, 3 * C_W), f32) * (D_MODEL ** -0.5) * c_scale,
        'w_out_c': jax.random.normal(ks[7], (N_ODD, C_W, D_MODEL), f32) * (C_W ** -0.5) * beta,
        'router_w': jax.random.normal(ks[8], (D_MODEL, N_EXPERTS), f32) * (D_MODEL ** -0.5),
        'router_b': jax.random.normal(ks[9], (N_EXPERTS,), f32) * 0.01,
        'moe_w1': jax.random.normal(ks[10], (DEPTH, N_EXPERTS, D_MODEL, D_EXPERT), f32) * (D_MODEL ** -0.5) * beta,
        'moe_w3': jax.random.normal(ks[11], (DEPTH, N_EXPERTS, D_MODEL, D_EXPERT), f32) * (D_MODEL ** -0.5) * beta,
        'moe_w2': jax.random.normal(ks[12], (DEPTH, N_EXPERTS, D_EXPERT, D_MODEL), f32) * (D_EXPERT ** -0.5) * beta,
        'ln_g': 1.0 + 0.01 * jax.random.normal(ks[13], (DEPTH, 2, D_MODEL), f32),
        'ln_b': 0.01 * jax.random.normal(ks[14], (DEPTH, 2, D_MODEL), f32),
    }


def reference(x_prompt, x_sample, w_in_ab, w_out_ab, a_sink, b_rpb, w_in_c, w_out_c, router_w, router_b,
              moe_w1, moe_w3, moe_w2, ln_g, ln_b):
    y_prompt = trunk(x_prompt, w_in_ab, w_out_ab, a_sink, b_rpb, w_in_c, w_out_c, router_w, router_b,
                     moe_w1, moe_w3, moe_w2, ln_g, ln_b)
    y_sample = trunk(x_sample, w_in_ab, w_out_ab, a_sink, b_rpb, w_in_c, w_out_c, router_w, router_b,
                     moe_w1, moe_w3, moe_w2, ln_g, ln_b)
    return (y_prompt, y_sample)
```

```python
import functools

import jax
import jax.numpy as jnp
from jax import lax
from jax.experimental import pallas as pl
from jax.experimental.pallas import tpu as pltpu

HEAD_DIM = 128
GRID_W = 64
A_HEADS = 16
A_KV_HEADS = 4
A_GROUP = A_HEADS // A_KV_HEADS
A_HALF_WINDOW = 128
B_HEADS = 16
NA_KH = 8
NA_KW = 16
C_HEADS = 32
C_BRANCHES = ((128, 1), (512, 4), (2048, 16))
N_EXPERTS = 16
N_GROUPS = 4
EXPERTS_PER_GROUP = N_EXPERTS // N_GROUPS
DEPTH = 4
DEEPNORM_ALPHA = (2 * DEPTH) ** 0.25
LN_EPS = 1e-5
NEG_INF = -1e30

A_Q = A_HEADS * HEAD_DIM
A_KV = A_KV_HEADS * HEAD_DIM
B_W = B_HEADS * HEAD_DIM
C_W = C_HEADS * HEAD_DIM

Q_BLOCK = 128
C_HEAD_GROUP = 8
B_HEAD_GROUP = 8
EXPERT_TILE = 512
VMEM_LIMIT = 56 * 1024 * 1024

_NT = (((1,), (1,)), ((), ()))


def _cparams(sem):
    return pltpu.CompilerParams(dimension_semantics=sem, vmem_limit_bytes=VMEM_LIMIT)


def _seg_bounds(pos, bounds):
    s = jnp.int32(bounds[0][0])
    e = jnp.int32(bounds[0][1])
    for a, b in bounds[1:]:
        inside = pos >= a
        s = jnp.where(inside, jnp.int32(a), s)
        e = jnp.where(inside, jnp.int32(b), e)
    return s, e


def _bounds(segs, unit):
    out, pos = [], 0
    for n, length in segs:
        for _ in range(n):
            out.append((pos // unit, (pos + length) // unit))
            pos += length
    return tuple(out)


def _matmul_kernel(*refs, widths):
    xs, w_ref, o_ref = refs[:len(widths)], refs[len(widths)], refs[len(widths) + 1]
    acc, off = None, 0
    for x_ref, k in zip(xs, widths):
        part = jnp.dot(x_ref[...], w_ref[off:off + k, :], preferred_element_type=jnp.float32)
        acc = part if acc is None else acc + part
        off += k
    o_ref[...] = acc.astype(o_ref.dtype)


def matmul(xs, w, out_dtype, tm=1024, tn=512):
    t = xs[0].shape[0]
    k, n = w.shape
    widths = tuple(x.shape[1] for x in xs)
    assert sum(widths) == k
    tm, tn = min(tm, t), min(tn, n)
    assert t % tm == 0 and n % tn == 0
    in_specs = [pl.BlockSpec((tm, kw), lambda i, j: (i, 0)) for kw in widths]
    in_specs.append(pl.BlockSpec((k, tn), lambda i, j: (0, j)))
    return pl.pallas_call(
        functools.partial(_matmul_kernel, widths=widths),
        grid=(t // tm, n // tn),
        in_specs=in_specs,
        out_specs=pl.BlockSpec((tm, tn), lambda i, j: (i, j)),
        out_shape=jax.ShapeDtypeStruct((t, n), out_dtype),
        compiler_params=_cparams(("parallel", "arbitrary")),
        name="matmul",
    )(*xs, w)


def _resid_ln_kernel(*refs, n_add):
    x_ref = refs[0]
    adds = refs[1:1 + n_add]
    g_ref, b_ref, o_ref, obf_ref = refs[1 + n_add:]
    y = adds[0][...]
    for a in adds[1:]:
        y = y + a[...]
    z = DEEPNORM_ALPHA * x_ref[...] + y
    mu = jnp.mean(z, axis=-1, keepdims=True)
    zc = z - mu
    var = jnp.mean(zc * zc, axis=-1, keepdims=True)
    out = zc * lax.rsqrt(var + LN_EPS) * g_ref[...] + b_ref[...]
    o_ref[...] = out
    obf_ref[...] = out.astype(jnp.bfloat16)


def resid_ln(x, adds, g, b, tm=128):
    t, d = x.shape
    tm = min(tm, t)
    in_specs = [pl.BlockSpec((tm, d), lambda i: (i, 0))]
    args = [x]
    for arr, lead in adds:
        if lead is None:
            in_specs.append(pl.BlockSpec((tm, d), lambda i: (i, 0)))
        else:
            in_specs.append(pl.BlockSpec((None, tm, d), functools.partial(lambda i, lead: (lead, i, 0), lead=lead)))
        args.append(arr)
    in_specs += [pl.BlockSpec((1, d), lambda i: (0, 0))] * 2
    args += [g.reshape(1, d), b.reshape(1, d)]
    return pl.pallas_call(
        functools.partial(_resid_ln_kernel, n_add=len(adds)),
        grid=(t // tm,),
        in_specs=in_specs,
        out_specs=[pl.BlockSpec((tm, d), lambda i: (i, 0))] * 2,
        out_shape=[jax.ShapeDtypeStruct((t, d), jnp.float32), jax.ShapeDtypeStruct((t, d), jnp.bfloat16)],
        compiler_params=_cparams(("parallel",)),
        name="resid_ln",
    )(*args)


def _banded_kernel(*refs, half, nk, group, step, bounds, use_sink, merge_prev, emit_lse, row_axis, head_axis):
    refs = list(refs)
    slopes_ref = refs.pop(0)
    sink_ref = refs.pop(0) if use_sink else None
    q_ref, kp_ref, kc_ref, kn_ref, vp_ref, vc_ref, vn_ref = refs[:7]
    refs = refs[7:]
    if merge_prev:
        oprev_ref, lprev_ref = refs[:2]
        refs = refs[2:]
    o_ref = refs.pop(0)
    lse_ref = refs.pop(0) if emit_lse else None

    ub = pl.program_id(row_axis)
    hgi = pl.program_id(head_axis)
    s0, e0 = _seg_bounds(ub * Q_BLOCK, bounds)
    has_prev = ub * Q_BLOCK > s0
    has_next = (ub + 1) * Q_BLOCK < e0
    win = Q_BLOCK + 2 * half
    row = lax.broadcasted_iota(jnp.int32, (Q_BLOCK, win), 0)
    col = lax.broadcasted_iota(jnp.int32, (Q_BLOCK, win), 1)
    dist = jnp.abs(col - half - row)
    valid = (dist <= half) & ((col >= half) | has_prev) & ((col < half + Q_BLOCK) | has_next)
    distf = (step * dist).astype(jnp.float32)
    scale = HEAD_DIM ** -0.5
    lane = lax.broadcasted_iota(jnp.int32, (Q_BLOCK, HEAD_DIM), 1)
    lse_tile = jnp.zeros((Q_BLOCK, HEAD_DIM), jnp.float32)

    for k in range(nk):
        ksl = slice(k * HEAD_DIM, (k + 1) * HEAD_DIM)
        kwin = jnp.concatenate([kp_ref[:, ksl], kc_ref[:, ksl], kn_ref[:, ksl]], axis=0)
        vwin = jnp.concatenate([vp_ref[:, ksl], vc_ref[:, ksl], vn_ref[:, ksl]], axis=0)
        for g in range(group):
            hidx = k * group + g
            hsl = slice(hidx * HEAD_DIM, (hidx + 1) * HEAD_DIM)
            head = hgi * (nk * group) + hidx
            s = lax.dot_general(q_ref[:, hsl], kwin, _NT, preferred_element_type=jnp.float32) * scale
            s = jnp.where(valid, s - slopes_ref[head] * distf, NEG_INF)
            m = jnp.max(s, axis=-1, keepdims=True)
            if use_sink:
                sk = sink_ref[head]
                m = jnp.maximum(m, sk)
            p = jnp.exp(s - m)
            l = jnp.sum(p, axis=-1, keepdims=True)
            if use_sink:
                l = l + jnp.exp(sk - m)
            o = jnp.dot(p.astype(jnp.bfloat16), vwin, preferred_element_type=jnp.float32) / l
            if merge_prev or emit_lse:
                lse = m + jnp.log(l)
            if merge_prev:
                lp = lprev_ref[:, hidx:hidx + 1]
                mm = jnp.maximum(lp, lse)
                wa = jnp.exp(lp - mm)
                wb = jnp.exp(lse - mm)
                o = (wa * oprev_ref[:, hsl].astype(jnp.float32) + wb * o) / (wa + wb)
                lse = mm + jnp.log(wa + wb)
            if emit_lse:
                lse_tile = jnp.where(lane == hidx, lse, lse_tile)
            o_ref[:, hsl] = o.astype(o_ref.dtype)
    if emit_lse:
        lse_ref[...] = lse_tile


def attention_a(qkv, slopes, sink, segs):
    t = qkv.shape[0]
    nblk = t // Q_BLOCK
    bounds = _bounds(segs, 1)
    qw, kw = A_Q, A_KV
    k_col, v_col = A_Q // kw, (A_Q + A_KV) // kw
    prev = lambda b: jnp.maximum(b - 1, 0)
    nxt = lambda b: jnp.minimum(b + 1, nblk - 1)
    smem = pl.BlockSpec(memory_space=pltpu.SMEM)
    in_specs = [smem, smem,
                pl.BlockSpec((Q_BLOCK, qw), lambda b: (b, 0)),
                pl.BlockSpec((Q_BLOCK, kw), lambda b: (prev(b), k_col)),
                pl.BlockSpec((Q_BLOCK, kw), lambda b: (b, k_col)),
                pl.BlockSpec((Q_BLOCK, kw), lambda b: (nxt(b), k_col)),
                pl.BlockSpec((Q_BLOCK, kw), lambda b: (prev(b), v_col)),
                pl.BlockSpec((Q_BLOCK, kw), lambda b: (b, v_col)),
                pl.BlockSpec((Q_BLOCK, kw), lambda b: (nxt(b), v_col))]
    kern = functools.partial(_banded_kernel, half=A_HALF_WINDOW, nk=A_KV_HEADS, group=A_GROUP, step=1,
                             bounds=bounds, use_sink=True, merge_prev=False, emit_lse=False,
                             row_axis=0, head_axis=1)
    return pl.pallas_call(
        kern,
        grid=(nblk, 1),
        in_specs=[_with_extra_axis(s) for s in in_specs],
        out_specs=pl.BlockSpec((Q_BLOCK, qw), lambda b, h: (b, 0)),
        out_shape=jax.ShapeDtypeStruct((t, qw), jnp.bfloat16),
        compiler_params=_cparams(("parallel", "arbitrary")),
        name="attention_a",
    )(slopes, sink, *([qkv] * 7))


def _with_extra_axis(spec):
    if spec.index_map is None:
        return spec
    f = spec.index_map
    return pl.BlockSpec(spec.block_shape, lambda b, h: f(b))


def attention_c_branch(qkv, slopes, dil, segs, prev_out, last):
    t = qkv.shape[0]
    rows = t // dil
    nub = rows // Q_BLOCK
    half = 64
    hw = C_HEAD_GROUP * HEAD_DIM
    ngroups = C_HEADS // C_HEAD_GROUP
    bounds = _bounds(segs, dil)
    qkv_v = qkv.reshape(rows, dil * 3 * C_W)
    per_tok = 3 * C_W // hw
    q_col = lambda r, h: r * per_tok + h
    k_col = lambda r, h: r * per_tok + C_W // hw + h
    v_col = lambda r, h: r * per_tok + 2 * C_W // hw + h
    n64 = rows // half
    prev = lambda u: jnp.maximum(2 * u - 1, 0)
    nxt = lambda u: jnp.minimum(2 * u + 2, n64 - 1)
    smem = pl.BlockSpec(memory_space=pltpu.SMEM)
    in_specs = [smem,
                pl.BlockSpec((Q_BLOCK, hw), lambda r, u, h: (u, q_col(r, h))),
                pl.BlockSpec((half, hw), lambda r, u, h: (prev(u), k_col(r, h))),
                pl.BlockSpec((Q_BLOCK, hw), lambda r, u, h: (u, k_col(r, h))),
                pl.BlockSpec((half, hw), lambda r, u, h: (nxt(u), k_col(r, h))),
                pl.BlockSpec((half, hw), lambda r, u, h: (prev(u), v_col(r, h))),
                pl.BlockSpec((Q_BLOCK, hw), lambda r, u, h: (u, v_col(r, h))),
                pl.BlockSpec((half, hw), lambda r, u, h: (nxt(u), v_col(r, h)))]
    args = [slopes] + [qkv_v] * 7
    o_spec = pl.BlockSpec((Q_BLOCK, hw), lambda r, u, h: (u, r * ngroups + h))
    l_spec = pl.BlockSpec((Q_BLOCK, HEAD_DIM), lambda r, u, h: (u, r * ngroups + h))
    merge_prev = prev_out is not None
    if merge_prev:
        in_specs += [o_spec, l_spec]
        args += [prev_out[0].reshape(rows, dil * C_W), prev_out[1].reshape(rows, dil * ngroups * HEAD_DIM)]
    out_specs = [o_spec]
    out_shape = [jax.ShapeDtypeStruct((rows, dil * C_W), jnp.bfloat16)]
    if not last:
        out_specs.append(l_spec)
        out_shape.append(jax.ShapeDtypeStruct((rows, dil * ngroups * HEAD_DIM), jnp.float32))
    kern = functools.partial(_banded_kernel, half=half, nk=C_HEAD_GROUP, group=1, step=dil, bounds=bounds,
                             use_sink=False, merge_prev=merge_prev, emit_lse=not last, row_axis=1, head_axis=2)
    res = pl.pallas_call(
        kern,
        grid=(dil, nub, ngroups),
        in_specs=in_specs,
        out_specs=out_specs,
        out_shape=out_shape,
        compiler_params=_cparams(("parallel", "parallel", "arbitrary")),
        name="attention_c_d%d" % dil,
    )(*args)
    o = res[0].reshape(t, C_W)
    lse = None if last else res[1].reshape(t, ngroups * HEAD_DIM)
    return o, lse


def attention_c(qkv, slopes, segs):
    state = None
    for i, (_, dil) in enumerate(C_BRANCHES):
        state = attention_c_branch(qkv, slopes, dil, segs, state, last=(i == len(C_BRANCHES) - 1))
    return state[0]


def _natten_kernel(*refs):
    q_ref = refs[0]
    k_refs = refs[1:1 + NA_KH]
    v_refs = refs[1 + NA_KH:1 + 2 * NA_KH]
    bias_ref, o_ref = refs[1 + 2 * NA_KH:]
    scale = HEAD_DIM ** -0.5
    for h in range(B_HEAD_GROUP):
        hsl = slice(h * HEAD_DIM, (h + 1) * HEAD_DIM)
        kwin = jnp.concatenate([r[:, hsl] for r in k_refs], axis=0)
        vwin = jnp.concatenate([r[:, hsl] for r in v_refs], axis=0)
        s = lax.dot_general(q_ref[:, hsl], kwin, _NT, preferred_element_type=jnp.float32) * scale + bias_ref[h]
        m = jnp.max(s, axis=-1, keepdims=True)
        p = jnp.exp(s - m)
        l = jnp.sum(p, axis=-1, keepdims=True)
        o = jnp.dot(p.astype(jnp.bfloat16), vwin, preferred_element_type=jnp.float32) / l
        o_ref[:, hsl] = o.astype(o_ref.dtype)


def natten_bias_table(rpb):
    c = jnp.arange(GRID_W)
    cs = jnp.clip(c - NA_KW // 2, 0, GRID_W - NA_KW)
    col_ok = (c[None, :] >= cs[:, None]) & (c[None, :] < cs[:, None] + NA_KW)
    dc = jnp.clip(c[None, :] - c[:, None] + (NA_KW - 1), 0, 2 * NA_KW - 2)
    dr = jnp.arange(NA_KH)[:, None] + jnp.arange(NA_KH)[None, :]
    tab = rpb.astype(jnp.float32)[:, dr[:, None, :, None], dc[None, :, None, :]]
    tab = jnp.where(col_ok[None, None, :, None, :], tab, NEG_INF)
    return tab.transpose(1, 0, 2, 3, 4).reshape(NA_KH, B_HEADS, GRID_W, NA_KH * GRID_W)


def attention_b(qkv, bias_tab, segs):
    t = qkv.shape[0]
    nrows = t // GRID_W
    bounds = _bounds(segs, GRID_W)
    hw = B_HEAD_GROUP * HEAD_DIM
    ngroups = B_HEADS // B_HEAD_GROUP
    q_col0 = (A_Q + 2 * A_KV) // hw
    k_col0 = q_col0 + B_W // hw
    v_col0 = k_col0 + B_W // hw

    def win_start(r):
        rs, re = _seg_bounds(r, bounds)
        return jnp.clip(r - NA_KH // 2, rs, re - NA_KH)

    def kv_spec(col0, i):
        return pl.BlockSpec((GRID_W, hw), lambda h, r: (win_start(r) + i, col0 + h))

    in_specs = [pl.BlockSpec((GRID_W, hw), lambda h, r: (r, q_col0 + h))]
    in_specs += [kv_spec(k_col0, i) for i in range(NA_KH)]
    in_specs += [kv_spec(v_col0, i) for i in range(NA_KH)]
    in_specs.append(pl.BlockSpec((None, B_HEAD_GROUP, GRID_W, NA_KH * GRID_W),
                                 lambda h, r: (NA_KH - 1 - (r - win_start(r)), h, 0, 0)))
    return pl.pallas_call(
        _natten_kernel,
        grid=(ngroups, nrows),
        in_specs=in_specs,
        out_specs=pl.BlockSpec((GRID_W, hw), lambda h, r: (r, h)),
        out_shape=jax.ShapeDtypeStruct((t, B_W), jnp.bfloat16),
        compiler_params=_cparams(("parallel", "arbitrary")),
        name="attention_b",
    )(*([qkv] * (1 + 2 * NA_KH)), bias_tab)


def _top2(vals):
    v1, i1 = vals[0], jnp.zeros(vals[0].shape, jnp.int32)
    for j in range(1, len(vals)):
        better = vals[j] > v1
        v1 = jnp.where(better, vals[j], v1)
        i1 = jnp.where(better, j, i1)
    v2, i2 = None, None
    for j in range(len(vals)):
        cand = jnp.where(i1 == j, -1.0, vals[j])
        if v2 is None:
            v2, i2 = cand, jnp.zeros(vals[0].shape, jnp.int32)
        else:
            better = cand > v2
            v2 = jnp.where(better, cand, v2)
            i2 = jnp.where(better, j, i2)
    return v1, i1, v2, i2


def _router_kernel(x_ref, w2_ref, whi_ref, b_ref, idx_ref, gate_ref):
    x = x_ref[...]
    x_hi = x.astype(jnp.bfloat16)
    x_lo = (x - x_hi.astype(jnp.float32)).astype(jnp.bfloat16)
    r1 = lax.dot_general(w2_ref[...], x_hi, _NT, preferred_element_type=jnp.float32)
    r2 = lax.dot_general(whi_ref[...], x_lo, _NT, preferred_element_type=jnp.float32)
    logits = r1[:N_EXPERTS] + r1[N_EXPERTS:] + r2 + b_ref[...]
    m = jnp.max(logits, axis=0, keepdims=True)
    ex = jnp.exp(logits - m)
    probs = ex / jnp.sum(ex, axis=0, keepdims=True)
    rows = [probs[e:e + 1, :] for e in range(N_EXPERTS)]
    tops = [_top2(rows[g * EXPERTS_PER_GROUP:(g + 1) * EXPERTS_PER_GROUP]) for g in range(N_GROUPS)]
    score = [tp[0] + tp[2] for tp in tops]
    best, gsel = score[0], jnp.zeros(score[0].shape, jnp.int32)
    for g in range(1, N_GROUPS):
        better = score[g] > best
        best = jnp.where(better, score[g], best)
        gsel = jnp.where(better, g, gsel)
    v1, i1, v2, i2 = tops[0]
    for g in range(1, N_GROUPS):
        pick = gsel == g
        v1 = jnp.where(pick, tops[g][0], v1)
        i1 = jnp.where(pick, tops[g][1], i1)
        v2 = jnp.where(pick, tops[g][2], v2)
        i2 = jnp.where(pick, tops[g][3], i2)
    tot = v1 + v2
    idx_ref[0:1, :] = gsel * EXPERTS_PER_GROUP + i1
    idx_ref[1:2, :] = gsel * EXPERTS_PER_GROUP + i2
    gate_ref[0:1, :] = v1 / tot
    gate_ref[1:2, :] = v2 / tot


def router(x, router_w, router_b, tm=512):
    t, d = x.shape
    tm = min(tm, t)
    wt = router_w.astype(jnp.float32).T
    w_hi = wt.astype(jnp.bfloat16)
    w_lo = (wt - w_hi.astype(jnp.float32)).astype(jnp.bfloat16)
    w2 = jnp.concatenate([w_hi, w_lo], axis=0)
    return pl.pallas_call(
        _router_kernel,
        grid=(t // tm,),
        in_specs=[pl.BlockSpec((tm, d), lambda i: (i, 0)),
                  pl.BlockSpec((2 * N_EXPERTS, d), lambda i: (0, 0)),
                  pl.BlockSpec((N_EXPERTS, d), lambda i: (0, 0)),
                  pl.BlockSpec((N_EXPERTS, 1), lambda i: (0, 0))],
        out_specs=[pl.BlockSpec((2, tm), lambda i: (0, i))] * 2,
        out_shape=[jax.ShapeDtypeStruct((2, t), jnp.int32), jax.ShapeDtypeStruct((2, t), jnp.float32)],
        compiler_params=_cparams(("parallel",)),
        name="router",
    )(x, w2, w_hi, router_b.astype(jnp.float32).reshape(N_EXPERTS, 1))


GATHER_CHUNK = 64


def _gather_kernel(idx_ref, src_ref, dst_ref, sems, *, n_rows):
    n_chunks = n_rows // GATHER_CHUNK

    def row_copy(row, slot):
        return pltpu.make_async_copy(src_ref.at[pl.ds(idx_ref[row], 1)], dst_ref.at[pl.ds(row, 1)], sems.at[slot])

    def issue(c, slot):
        def body(r, _):
            row_copy(c * GATHER_CHUNK + r, slot).start()
            return 0
        lax.fori_loop(0, GATHER_CHUNK, body, 0)

    def drain(c, slot):
        def body(r, _):
            row_copy(c * GATHER_CHUNK + r, slot).wait()
            return 0
        lax.fori_loop(0, GATHER_CHUNK, body, 0)

    issue(0, 0)

    def step(c, _):
        slot = c % 2

        @pl.when(c + 1 < n_chunks)
        def _():
            issue(c + 1, 1 - slot)

        drain(c, slot)
        return 0

    lax.fori_loop(0, n_chunks, step, 0)


def gather_rows(src, idx):
    n = idx.shape[0]
    assert n % GATHER_CHUNK == 0
    return pl.pallas_call(
        functools.partial(_gather_kernel, n_rows=n),
        grid_spec=pltpu.PrefetchScalarGridSpec(
            num_scalar_prefetch=1,
            grid=(1,),
            in_specs=[pl.BlockSpec(memory_space=pl.ANY)],
            out_specs=pl.BlockSpec(memory_space=pl.ANY),
            scratch_shapes=[pltpu.SemaphoreType.DMA((2,))]),
        out_shape=jax.ShapeDtypeStruct((n, src.shape[1]), src.dtype),
        compiler_params=_cparams(("arbitrary",)),
        name="gather_rows",
    )(idx, src)


def _expert_kernel(te_ref, na_ref, x_ref, w1_ref, w3_ref, w2_ref, g_ref, o_ref, *, n_out_chunks):
    i = pl.program_id(0)
    j = pl.program_id(1)
    nj = pl.num_programs(1)

    @pl.when(i < na_ref[0])
    def _():
        x = x_ref[...].astype(jnp.bfloat16)
        h1 = jnp.dot(x, w1_ref[...], preferred_element_type=jnp.float32)
        h3 = jnp.dot(x, w3_ref[...], preferred_element_type=jnp.float32)
        h = (jax.nn.silu(h1) * h3).astype(jnp.bfloat16)
        cw = o_ref.shape[1] // n_out_chunks
        for n in range(n_out_chunks):
            sl = slice(n * cw, (n + 1) * cw)
            part = jnp.dot(h, w2_ref[:, sl], preferred_element_type=jnp.float32)

            @pl.when(j == 0)
            def _():
                o_ref[:, sl] = part

            @pl.when(j > 0)
            def _():
                o_ref[:, sl] += part

        @pl.when(j == nj - 1)
        def _():
            o_ref[...] = o_ref[...] * g_ref[:, 0:1]


def expert_ffn(xs, w1, w3, w2, gate_b, tile_expert, n_active, tf=256):
    s, d = xs.shape
    f = w1.shape[2]
    tm = EXPERT_TILE
    tf = min(tf, f)
    n_tiles = s // tm
    nj = f // tf

    def act(i, na):
        return jnp.minimum(i, na[0] - 1)

    def jact(i, j, na):
        return jnp.where(i < na[0], j, nj - 1)

    grid_spec = pltpu.PrefetchScalarGridSpec(
        num_scalar_prefetch=2,
        grid=(n_tiles, nj),
        in_specs=[pl.BlockSpec((tm, d), lambda i, j, te, na: (act(i, na), 0)),
                  pl.BlockSpec((None, d, tf), lambda i, j, te, na: (te[act(i, na)], 0, jact(i, j, na))),
                  pl.BlockSpec((None, d, tf), lambda i, j, te, na: (te[act(i, na)], 0, jact(i, j, na))),
                  pl.BlockSpec((None, tf, d), lambda i, j, te, na: (te[act(i, na)], jact(i, j, na), 0)),
                  pl.BlockSpec((tm, HEAD_DIM), lambda i, j, te, na: (act(i, na), 0))],
        out_specs=pl.BlockSpec((tm, d), lambda i, j, te, na: (act(i, na), 0)))
    return pl.pallas_call(
        functools.partial(_expert_kernel, n_out_chunks=max(1, d // 512)),
        grid_spec=grid_spec,
        out_shape=jax.ShapeDtypeStruct((s, d), jnp.float32),
        compiler_params=_cparams(("arbitrary", "arbitrary")),
        name="expert_ffn",
    )(tile_expert, n_active, xs, w1, w3, w2, gate_b)


def dispatch_plan(idx, gate):
    k, t = idx.shape
    tm = EXPERT_TILE
    n_assign = k * t
    n_tiles = n_assign // tm + N_EXPERTS
    n_slots = n_tiles * tm
    e_flat = idx.reshape(n_assign)
    onehot = (e_flat[:, None] == jnp.arange(N_EXPERTS, dtype=jnp.int32)[None, :]).astype(jnp.int32)
    csum = jnp.cumsum(onehot, axis=0)
    rank = jnp.sum(csum * onehot, axis=1) - 1
    counts = csum[-1]
    padded = (counts + tm - 1) // tm * tm
    ends = jnp.cumsum(padded)
    offsets = ends - padded
    pos = offsets[e_flat] + rank
    tok = jnp.arange(n_assign, dtype=jnp.int32) % t
    src_tok = jnp.zeros((n_slots,), jnp.int32).at[pos].set(tok, unique_indices=True)
    gate_slots = jnp.zeros((n_slots,), jnp.float32).at[pos].set(gate.reshape(n_assign), unique_indices=True)
    tile_start = jnp.arange(n_tiles, dtype=jnp.int32) * tm
    tile_expert = jnp.minimum(jnp.searchsorted(ends, tile_start, side="right"), N_EXPERTS - 1).astype(jnp.int32)
    n_active = (ends[-1] // tm).astype(jnp.int32).reshape(1)
    return src_tok, gate_slots, pos.astype(jnp.int32), tile_expert, n_active


def moe_layer(x, x_bf_unused, router_w, router_b, w1, w3, w2, g, b):
    idx, gate = router(x, router_w, router_b)
    src_tok, gate_slots, pos, tile_expert, n_active = dispatch_plan(idx, gate)
    xs = gather_rows(x, src_tok)
    gate_b = jnp.broadcast_to(gate_slots[:, None], (gate_slots.shape[0], HEAD_DIM))
    ys = expert_ffn(xs, w1, w3, w2, gate_b, tile_expert, n_active)
    ysel = gather_rows(ys, pos).reshape(2, x.shape[0], x.shape[1])
    return resid_ln(x, [(ysel, 0), (ysel, 1)], g, b)


def alibi_slopes(n):
    return jnp.exp2(-8.0 * jnp.arange(1, n + 1, dtype=jnp.float32) / n)


def trunk(x, segs, w_in_ab, w_out_ab, a_sink, b_rpb, w_in_c, w_out_c, router_w, router_b,
          moe_w1, moe_w3, moe_w2, ln_g, ln_b):
    bf = jnp.bfloat16
    x_bf = x.astype(bf)
    slopes_a = alibi_slopes(A_HEADS)
    slopes_c = alibi_slopes(C_HEADS)
    for layer in range(DEPTH):
        i = layer // 2
        if layer % 2 == 0:
            qkv = matmul([x_bf], w_in_ab[i].astype(bf), bf)
            oa = attention_a(qkv, slopes_a, a_sink[i].astype(jnp.float32), segs)
            ob = attention_b(qkv, natten_bias_table(b_rpb[i]), segs)
            mix = matmul([oa, ob], w_out_ab[i].astype(bf), jnp.float32)
        else:
            qkv = matmul([x_bf], w_in_c[i].astype(bf), bf)
            oc = attention_c(qkv, slopes_c, segs)
            mix = matmul([oc], w_out_c[i].astype(bf), jnp.float32)
        x, x_bf = resid_ln(x, [(mix, None)], ln_g[layer, 0], ln_b[layer, 0])
        x, x_bf = moe_layer(x, x_bf, router_w, router_b, moe_w1[layer].astype(bf), moe_w3[layer].astype(bf),
                            moe_w2[layer].astype(bf), ln_g[layer, 1], ln_b[layer, 1])
    return x


def kernel(x_prompt, x_sample, w_in_ab, w_out_ab, a_sink, b_rpb, w_in_c, w_out_c, router_w, router_b,
           moe_w1, moe_w3, moe_w2, ln_g, ln_b):
    bp, sp, d = x_prompt.shape
    bs, ss, _ = x_sample.shape
    segs = ((bp, sp), (bs, ss))
    x = jnp.concatenate([x_prompt.reshape(bp * sp, d), x_sample.reshape(bs * ss, d)], axis=0)
    y = trunk(x, segs, w_in_ab, w_out_ab, a_sink, b_rpb, w_in_c, w_out_c, router_w, router_b,
              moe_w1, moe_w3, moe_w2, ln_g, ln_b)
    return (y[:bp * sp].reshape(bp, sp, d), y[bp * sp:].reshape(bs, ss, d))
```

```python
import functools

import numpy as np
import jax
import jax.numpy as jnp
from jax import lax
from jax.experimental import pallas as pl
from jax.experimental.pallas import tpu as pltpu

HEAD_DIM = 128
GRID_W = 64
A_HEADS = 16
A_KV_HEADS = 4
A_GROUP = A_HEADS // A_KV_HEADS
A_HALF_WINDOW = 128
B_HEADS = 16
NA_KH = 8
NA_KW = 16
C_HEADS = 32
C_BRANCHES = ((128, 1), (512, 4), (2048, 16))
C_HALF = 64
N_EXPERTS = 16
N_GROUPS = 4
EXPERTS_PER_GROUP = N_EXPERTS // N_GROUPS
DEPTH = 4
DEEPNORM_ALPHA = (2 * DEPTH) ** 0.25
LN_EPS = 1e-5
NEG_INF = -1e30

A_Q = A_HEADS * HEAD_DIM
A_KV = A_KV_HEADS * HEAD_DIM
B_W = B_HEADS * HEAD_DIM
C_W = C_HEADS * HEAD_DIM

LANES = 128
BF16_SUBLANES = 16
Q_BLOCK = 128
C_HEAD_GROUP = 8
C_GROUPS = C_HEADS // C_HEAD_GROUP
B_HEAD_GROUP = 8
MERGE_TILE = 256
MOE_TILE = 512
SLOT_CHUNK = BF16_SUBLANES
LOCAL_SLOTS = 2 * MOE_TILE + N_EXPERTS * SLOT_CHUNK
CHUNKS_PER_TILE = LOCAL_SLOTS // SLOT_CHUNK
EXPERT_TILE = 512
CHUNKS_PER_EXPERT_TILE = EXPERT_TILE // SLOT_CHUNK
GATHER_WAVE = 16
VMEM_LIMIT = 56 * 1024 * 1024

_NT = (((1,), (1,)), ((), ()))


def _cparams(sem):
    return pltpu.CompilerParams(dimension_semantics=sem, vmem_limit_bytes=VMEM_LIMIT)


def _round_up(x, m):
    return (x + m - 1) // m * m


def _seg_bounds(pos, bounds):
    s = jnp.int32(bounds[0][0])
    e = jnp.int32(bounds[0][1])
    for a, b in bounds[1:]:
        inside = pos >= a
        s = jnp.where(inside, jnp.int32(a), s)
        e = jnp.where(inside, jnp.int32(b), e)
    return s, e


def _bounds(segs, unit):
    out, pos = [], 0
    for n, length in segs:
        for _ in range(n):
            out.append((pos // unit, (pos + length) // unit))
            pos += length
    return tuple(out)


def _matmul_kernel(*refs, widths):
    xs, w_ref, o_ref = refs[:len(widths)], refs[len(widths)], refs[len(widths) + 1]
    acc, off = None, 0
    for x_ref, k in zip(xs, widths):
        part = jnp.dot(x_ref[...], w_ref[off:off + k, :], preferred_element_type=jnp.float32)
        acc = part if acc is None else acc + part
        off += k
    o_ref[...] = acc.astype(o_ref.dtype)


def matmul(xs, w, out_dtype, tm=1024, tn=512):
    t = xs[0].shape[0]
    k, n = w.shape
    widths = tuple(x.shape[1] for x in xs)
    assert sum(widths) == k
    tm, tn = min(tm, t), min(tn, n)
    assert t % tm == 0 and n % tn == 0
    in_specs = [pl.BlockSpec((tm, kw), lambda i, j: (i, 0)) for kw in widths]
    in_specs.append(pl.BlockSpec((k, tn), lambda i, j: (0, j)))
    return pl.pallas_call(
        functools.partial(_matmul_kernel, widths=widths),
        grid=(t // tm, n // tn),
        in_specs=in_specs,
        out_specs=pl.BlockSpec((tm, tn), lambda i, j: (i, j)),
        out_shape=jax.ShapeDtypeStruct((t, n), out_dtype),
        compiler_params=_cparams(("parallel", "arbitrary")),
        name="matmul",
    )(*xs, w)


def _matmul_dilated_kernel(x_ref, w_ref, o_ref, *rest, dils):
    z_refs, acc_ref = rest[:-1], rest[-1]
    tm, tn = o_ref.shape
    acc = jnp.dot(x_ref[...], w_ref[...], preferred_element_type=jnp.float32)
    o_ref[...] = acc.astype(o_ref.dtype)
    for s in range(tn // LANES):
        acc_ref[s] = acc[:, s * LANES:(s + 1) * LANES]
    for z_ref, d in zip(z_refs, dils):
        for rho in range(d):
            for s in range(tn // LANES):
                z_ref[rho, :, s * LANES:(s + 1) * LANES] = (
                    acc_ref[s, pl.ds(rho, tm // d, stride=d), :].astype(z_ref.dtype))


def matmul_dilated(x, w, dils, tm=1024, tn=512):
    t, k = x.shape
    n = w.shape[1]
    tm, tn = min(tm, t), min(tn, n)
    assert t % tm == 0 and n % tn == 0 and all(tm % (d * BF16_SUBLANES) == 0 for d in dils)
    out_specs = [pl.BlockSpec((tm, tn), lambda i, j: (i, j))]
    out_shape = [jax.ShapeDtypeStruct((t, n), jnp.bfloat16)]
    for d in dils:
        out_specs.append(pl.BlockSpec((d, tm // d, tn), lambda i, j: (0, i, j)))
        out_shape.append(jax.ShapeDtypeStruct((d, t // d, n), jnp.bfloat16))
    return pl.pallas_call(
        functools.partial(_matmul_dilated_kernel, dils=tuple(dils)),
        grid=(t // tm, n // tn),
        in_specs=[pl.BlockSpec((tm, k), lambda i, j: (i, 0)), pl.BlockSpec((k, tn), lambda i, j: (0, j))],
        out_specs=out_specs,
        out_shape=out_shape,
        scratch_shapes=[pltpu.VMEM((tn // LANES, tm, LANES), jnp.float32)],
        compiler_params=_cparams(("parallel", "arbitrary")),
        name="matmul_dilated",
    )(x, w)


def _resid_ln_kernel(x_ref, y_ref, g_ref, b_ref, o_ref, obf_ref):
    z = DEEPNORM_ALPHA * x_ref[...] + y_ref[...]
    mu = jnp.mean(z, axis=-1, keepdims=True)
    zc = z - mu
    var = jnp.mean(zc * zc, axis=-1, keepdims=True)
    out = zc * lax.rsqrt(var + LN_EPS) * g_ref[...] + b_ref[...]
    o_ref[...] = out
    obf_ref[...] = out.astype(jnp.bfloat16)


def resid_ln(x, y, g, b, rows=None, tm=128):
    t, d = x.shape
    start, count = (0, t) if rows is None else rows
    tm = min(tm, count)
    assert count % tm == 0 and start % tm == 0
    first = start // tm
    row_in = pl.BlockSpec((tm, d), lambda i: (first + i, 0))
    row_out = pl.BlockSpec((tm, d), lambda i: (i, 0))
    vec = pl.BlockSpec((1, d), lambda i: (0, 0))
    return pl.pallas_call(
        _resid_ln_kernel,
        grid=(count // tm,),
        in_specs=[row_in, row_in, vec, vec],
        out_specs=[row_out, row_out],
        out_shape=[jax.ShapeDtypeStruct((count, d), jnp.float32), jax.ShapeDtypeStruct((count, d), jnp.bfloat16)],
        compiler_params=_cparams(("parallel",)),
        name="resid_ln",
    )(x, y, g.reshape(1, d), b.reshape(1, d))


def _banded_kernel(*refs, half, nk, group, step, bounds, use_sink, emit_lse, row_axis, head_axis):
    refs = list(refs)
    slopes_ref = refs.pop(0)
    sink_ref = refs.pop(0) if use_sink else None
    q_ref, kp_ref, kc_ref, kn_ref, vp_ref, vc_ref, vn_ref = refs[:7]
    o_ref = refs[7]
    lse_ref = refs[8] if emit_lse else None

    ub = pl.program_id(row_axis)
    hgi = pl.program_id(head_axis)
    s0, e0 = _seg_bounds(ub * Q_BLOCK, bounds)
    has_prev = ub * Q_BLOCK > s0
    has_next = (ub + 1) * Q_BLOCK < e0
    win = Q_BLOCK + 2 * half
    row = lax.broadcasted_iota(jnp.int32, (Q_BLOCK, win), 0)
    col = lax.broadcasted_iota(jnp.int32, (Q_BLOCK, win), 1)
    dist = jnp.abs(col - half - row)
    valid = (dist <= half) & ((col >= half) | has_prev) & ((col < half + Q_BLOCK) | has_next)
    distf = (step * dist).astype(jnp.float32)
    scale = HEAD_DIM ** -0.5
    lane = lax.broadcasted_iota(jnp.int32, (Q_BLOCK, LANES), 1)
    lse_tile = jnp.zeros((Q_BLOCK, LANES), jnp.float32)

    for k in range(nk):
        ksl = slice(k * HEAD_DIM, (k + 1) * HEAD_DIM)
        kwin = jnp.concatenate([kp_ref[:, ksl], kc_ref[:, ksl], kn_ref[:, ksl]], axis=0)
        vwin = jnp.concatenate([vp_ref[:, ksl], vc_ref[:, ksl], vn_ref[:, ksl]], axis=0)
        for g in range(group):
            hidx = k * group + g
            hsl = slice(hidx * HEAD_DIM, (hidx + 1) * HEAD_DIM)
            head = hgi * (nk * group) + hidx
            s = lax.dot_general(q_ref[:, hsl], kwin, _NT, preferred_element_type=jnp.float32) * scale
            s = jnp.where(valid, s - slopes_ref[head] * distf, NEG_INF)
            m = jnp.max(s, axis=-1, keepdims=True)
            if use_sink:
                sk = sink_ref[head]
                m = jnp.maximum(m, sk)
            p = jnp.exp(s - m)
            l = jnp.sum(p, axis=-1, keepdims=True)
            if use_sink:
                l = l + jnp.exp(sk - m)
            o = jnp.dot(p.astype(jnp.bfloat16), vwin, preferred_element_type=jnp.float32) / l
            if emit_lse:
                lse_tile = jnp.where(lane == hidx, m + jnp.log(l), lse_tile)
            o_ref[:, hsl] = o.astype(o_ref.dtype)
    if emit_lse:
        lse_ref[...] = lse_tile


def attention_a(qkv, slopes, sink, segs):
    t = qkv.shape[0]
    nblk = t // Q_BLOCK
    bounds = _bounds(segs, 1)
    qw, kw = A_Q, A_KV
    k_col, v_col = A_Q // kw, (A_Q + A_KV) // kw
    prev = lambda b: jnp.maximum(b - 1, 0)
    nxt = lambda b: jnp.minimum(b + 1, nblk - 1)
    smem = pl.BlockSpec(memory_space=pltpu.SMEM)
    in_specs = [smem, smem,
                pl.BlockSpec((Q_BLOCK, qw), lambda b, h: (b, 0)),
                pl.BlockSpec((Q_BLOCK, kw), lambda b, h: (prev(b), k_col)),
                pl.BlockSpec((Q_BLOCK, kw), lambda b, h: (b, k_col)),
                pl.BlockSpec((Q_BLOCK, kw), lambda b, h: (nxt(b), k_col)),
                pl.BlockSpec((Q_BLOCK, kw), lambda b, h: (prev(b), v_col)),
                pl.BlockSpec((Q_BLOCK, kw), lambda b, h: (b, v_col)),
                pl.BlockSpec((Q_BLOCK, kw), lambda b, h: (nxt(b), v_col))]
    kern = functools.partial(_banded_kernel, half=A_HALF_WINDOW, nk=A_KV_HEADS, group=A_GROUP, step=1,
                             bounds=bounds, use_sink=True, emit_lse=False, row_axis=0, head_axis=1)
    return pl.pallas_call(
        kern,
        grid=(nblk, 1),
        in_specs=in_specs,
        out_specs=pl.BlockSpec((Q_BLOCK, qw), lambda b, h: (b, 0)),
        out_shape=jax.ShapeDtypeStruct((t, qw), jnp.bfloat16),
        compiler_params=_cparams(("parallel", "arbitrary")),
        name="attention_a",
    )(slopes, sink, *([qkv] * 7))


def attention_c_branch(z, slopes, dil, segs):
    rows = z.shape[1]
    nub = rows // Q_BLOCK
    half = C_HALF
    hw = C_HEAD_GROUP * HEAD_DIM
    bounds = _bounds(segs, dil)
    k_col0, v_col0 = C_W // hw, 2 * C_W // hw
    nhalf = rows // half
    prev = lambda u: jnp.maximum(2 * u - 1, 0)
    nxt = lambda u: jnp.minimum(2 * u + 2, nhalf - 1)
    smem = pl.BlockSpec(memory_space=pltpu.SMEM)
    in_specs = [smem,
                pl.BlockSpec((None, Q_BLOCK, hw), lambda r, u, h: (r, u, h)),
                pl.BlockSpec((None, half, hw), lambda r, u, h: (r, prev(u), k_col0 + h)),
                pl.BlockSpec((None, Q_BLOCK, hw), lambda r, u, h: (r, u, k_col0 + h)),
                pl.BlockSpec((None, half, hw), lambda r, u, h: (r, nxt(u), k_col0 + h)),
                pl.BlockSpec((None, half, hw), lambda r, u, h: (r, prev(u), v_col0 + h)),
                pl.BlockSpec((None, Q_BLOCK, hw), lambda r, u, h: (r, u, v_col0 + h)),
                pl.BlockSpec((None, half, hw), lambda r, u, h: (r, nxt(u), v_col0 + h))]
    kern = functools.partial(_banded_kernel, half=half, nk=C_HEAD_GROUP, group=1, step=dil, bounds=bounds,
                             use_sink=False, emit_lse=True, row_axis=1, head_axis=2)
    return pl.pallas_call(
        kern,
        grid=(dil, nub, C_GROUPS),
        in_specs=in_specs,
        out_specs=[pl.BlockSpec((None, Q_BLOCK, hw), lambda r, u, h: (r, u, h)),
                   pl.BlockSpec((None, Q_BLOCK, LANES), lambda r, u, h: (r, u, h))],
        out_shape=[jax.ShapeDtypeStruct((dil, rows, C_W), jnp.bfloat16),
                   jax.ShapeDtypeStruct((dil, rows, C_GROUPS * LANES), jnp.float32)],
        compiler_params=_cparams(("parallel", "parallel", "arbitrary")),
        name="attention_c_d%d" % dil,
    )(slopes, *([z] * 7))


def _plane_perm(dil, tm):
    p = np.zeros((tm, tm), np.float32)
    tok = np.arange(tm)
    p[tok, (tok % dil) * (tm // dil) + tok // dil] = 1.0
    return jnp.asarray(p, jnp.bfloat16)


def _split3(x):
    hi = x.astype(jnp.bfloat16)
    r = x - hi.astype(jnp.float32)
    mid = r.astype(jnp.bfloat16)
    lo = (r - mid.astype(jnp.float32)).astype(jnp.bfloat16)
    return hi, mid, lo


def _merge_kernel(*refs, dils):
    nb = len(dils)
    o_refs, l_refs = refs[:nb], refs[nb:2 * nb]
    p_refs = refs[2 * nb:2 * nb + nb - 1]
    out_ref = refs[-1]
    outs, lses = [o_refs[0][0].astype(jnp.float32)], [l_refs[0][0]]
    for b in range(1, nb):
        d = dils[b]
        perm = p_refs[b - 1][...]
        stacked = jnp.concatenate([o_refs[b][r] for r in range(d)], axis=0)
        outs.append(jnp.dot(perm, stacked, preferred_element_type=jnp.float32))
        lstack = jnp.concatenate([l_refs[b][r] for r in range(d)], axis=0)
        lses.append(sum(jnp.dot(perm, part, preferred_element_type=jnp.float32) for part in _split3(lstack)))
    for h in range(C_HEADS):
        hsl = slice(h * HEAD_DIM, (h + 1) * HEAD_DIM)
        col = (h // C_HEAD_GROUP) * LANES + h % C_HEAD_GROUP
        ls = [l[:, col:col + 1] for l in lses]
        m = functools.reduce(jnp.maximum, ls)
        es = [jnp.exp(l - m) for l in ls]
        tot = sum(es)
        acc = sum((e / tot) * o[:, hsl] for e, o in zip(es, outs))
        out_ref[:, hsl] = acc.astype(out_ref.dtype)


def merge_branches(os_, ls_, dils):
    t = os_[0].shape[0] * os_[0].shape[1]
    tm = min(MERGE_TILE, t)
    in_specs, args = [], []
    for arr, width in ((os_, C_W), (ls_, C_GROUPS * LANES)):
        for a, d in zip(arr, dils):
            in_specs.append(pl.BlockSpec((d, tm // d, width), lambda i: (0, i, 0)))
            args.append(a)
    for d in dils[1:]:
        in_specs.append(pl.BlockSpec((tm, tm), lambda i: (0, 0)))
        args.append(_plane_perm(d, tm))
    return pl.pallas_call(
        functools.partial(_merge_kernel, dils=tuple(dils)),
        grid=(t // tm,),
        in_specs=in_specs,
        out_specs=pl.BlockSpec((tm, C_W), lambda i: (i, 0)),
        out_shape=jax.ShapeDtypeStruct((t, C_W), jnp.bfloat16),
        compiler_params=_cparams(("parallel",)),
        name="merge_branches",
    )(*args)


def attention_c(zs, slopes, segs):
    dils = tuple(d for _, d in C_BRANCHES)
    res = [attention_c_branch(z, slopes, d, segs) for z, d in zip(zs, dils)]
    return merge_branches([r[0] for r in res], [r[1] for r in res], dils)


def _natten_kernel(*refs):
    q_ref = refs[0]
    k_refs = refs[1:1 + NA_KH]
    v_refs = refs[1 + NA_KH:1 + 2 * NA_KH]
    bias_ref, o_ref = refs[1 + 2 * NA_KH:]
    scale = HEAD_DIM ** -0.5
    for h in range(B_HEAD_GROUP):
        hsl = slice(h * HEAD_DIM, (h + 1) * HEAD_DIM)
        kwin = jnp.concatenate([r[:, hsl] for r in k_refs], axis=0)
        vwin = jnp.concatenate([r[:, hsl] for r in v_refs], axis=0)
        s = lax.dot_general(q_ref[:, hsl], kwin, _NT, preferred_element_type=jnp.float32) * scale + bias_ref[h]
        m = jnp.max(s, axis=-1, keepdims=True)
        p = jnp.exp(s - m)
        l = jnp.sum(p, axis=-1, keepdims=True)
        o = jnp.dot(p.astype(jnp.bfloat16), vwin, preferred_element_type=jnp.float32) / l
        o_ref[:, hsl] = o.astype(o_ref.dtype)


def natten_bias_table(rpb):
    c = jnp.arange(GRID_W)
    cs = jnp.clip(c - NA_KW // 2, 0, GRID_W - NA_KW)
    col_ok = (c[None, :] >= cs[:, None]) & (c[None, :] < cs[:, None] + NA_KW)
    dc = jnp.clip(c[None, :] - c[:, None] + (NA_KW - 1), 0, 2 * NA_KW - 2)
    pick = (dc[:, :, None] == jnp.arange(2 * NA_KW - 1)[None, None, :]).astype(jnp.float32)
    rows = jnp.stack([rpb.astype(jnp.float32)[:, o:o + NA_KH, :] for o in range(NA_KH)], axis=0)
    tab = jnp.einsum("ohid,qkd->ohqik", rows, pick, precision=lax.Precision.HIGHEST)
    tab = jnp.where(col_ok[None, None, :, None, :], tab, NEG_INF)
    return tab.reshape(NA_KH, B_HEADS, GRID_W, NA_KH * GRID_W)


def attention_b(qkv, bias_tab, segs):
    t = qkv.shape[0]
    nrows = t // GRID_W
    bounds = _bounds(segs, GRID_W)
    hw = B_HEAD_GROUP * HEAD_DIM
    ngroups = B_HEADS // B_HEAD_GROUP
    q_col0 = (A_Q + 2 * A_KV) // hw
    k_col0 = q_col0 + B_W // hw
    v_col0 = k_col0 + B_W // hw

    def win_start(r):
        rs, re = _seg_bounds(r, bounds)
        return jnp.clip(r - NA_KH // 2, rs, re - NA_KH)

    def kv_spec(col0, i):
        return pl.BlockSpec((GRID_W, hw), lambda h, r: (win_start(r) + i, col0 + h))

    in_specs = [pl.BlockSpec((GRID_W, hw), lambda h, r: (r, q_col0 + h))]
    in_specs += [kv_spec(k_col0, i) for i in range(NA_KH)]
    in_specs += [kv_spec(v_col0, i) for i in range(NA_KH)]
    in_specs.append(pl.BlockSpec((None, B_HEAD_GROUP, GRID_W, NA_KH * GRID_W),
                                 lambda h, r: (NA_KH - 1 - (r - win_start(r)), h, 0, 0)))
    return pl.pallas_call(
        _natten_kernel,
        grid=(ngroups, nrows),
        in_specs=in_specs,
        out_specs=pl.BlockSpec((GRID_W, hw), lambda h, r: (r, h)),
        out_shape=jax.ShapeDtypeStruct((t, B_W), jnp.bfloat16),
        compiler_params=_cparams(("parallel", "arbitrary")),
        name="attention_b",
    )(*([qkv] * (1 + 2 * NA_KH)), bias_tab)


def _top2(vals):
    v1, i1 = vals[0], jnp.zeros(vals[0].shape, jnp.int32)
    for j in range(1, len(vals)):
        better = vals[j] > v1
        v1 = jnp.where(better, vals[j], v1)
        i1 = jnp.where(better, j, i1)
    v2, i2 = None, None
    for j in range(len(vals)):
        cand = jnp.where(i1 == j, -1.0, vals[j])
        if v2 is None:
            v2, i2 = cand, jnp.zeros(vals[0].shape, jnp.int32)
        else:
            better = cand > v2
            v2 = jnp.where(better, cand, v2)
            i2 = jnp.where(better, j, i2)
    return v1, i1, v2, i2


def _router_kernel(x_ref, w2_ref, whi_ref, b_ref, idx_ref, gate_ref):
    x = x_ref[...]
    x_hi = x.astype(jnp.bfloat16)
    x_lo = (x - x_hi.astype(jnp.float32)).astype(jnp.bfloat16)
    r1 = lax.dot_general(w2_ref[...], x_hi, _NT, preferred_element_type=jnp.float32)
    r2 = lax.dot_general(whi_ref[...], x_lo, _NT, preferred_element_type=jnp.float32)
    logits = r1[:N_EXPERTS] + r1[N_EXPERTS:] + r2 + b_ref[...]
    m = jnp.max(logits, axis=0, keepdims=True)
    ex = jnp.exp(logits - m)
    probs = ex / jnp.sum(ex, axis=0, keepdims=True)
    rows = [probs[e:e + 1, :] for e in range(N_EXPERTS)]
    tops = [_top2(rows[g * EXPERTS_PER_GROUP:(g + 1) * EXPERTS_PER_GROUP]) for g in range(N_GROUPS)]
    score = [tp[0] + tp[2] for tp in tops]
    best, gsel = score[0], jnp.zeros(score[0].shape, jnp.int32)
    for g in range(1, N_GROUPS):
        better = score[g] > best
        best = jnp.where(better, score[g], best)
        gsel = jnp.where(better, g, gsel)
    v1, i1, v2, i2 = tops[0]
    for g in range(1, N_GROUPS):
        pick = gsel == g
        v1 = jnp.where(pick, tops[g][0], v1)
        i1 = jnp.where(pick, tops[g][1], i1)
        v2 = jnp.where(pick, tops[g][2], v2)
        i2 = jnp.where(pick, tops[g][3], i2)
    tot = v1 + v2
    idx_ref[0:1, :] = gsel * EXPERTS_PER_GROUP + i1
    idx_ref[1:2, :] = gsel * EXPERTS_PER_GROUP + i2
    gate_ref[0:1, :] = v1 / tot
    gate_ref[1:2, :] = v2 / tot


def router(x, router_w, router_b, tm=512):
    t, d = x.shape
    tm = min(tm, t)
    wt = router_w.astype(jnp.float32).T
    w_hi = wt.astype(jnp.bfloat16)
    w_lo = (wt - w_hi.astype(jnp.float32)).astype(jnp.bfloat16)
    w2 = jnp.concatenate([w_hi, w_lo], axis=0)
    return pl.pallas_call(
        _router_kernel,
        grid=(t // tm,),
        in_specs=[pl.BlockSpec((tm, d), lambda i: (i, 0)),
                  pl.BlockSpec((2 * N_EXPERTS, d), lambda i: (0, 0)),
                  pl.BlockSpec((N_EXPERTS, d), lambda i: (0, 0)),
                  pl.BlockSpec((N_EXPERTS, 1), lambda i: (0, 0))],
        out_specs=[pl.BlockSpec((2, tm), lambda i: (0, i))] * 2,
        out_shape=[jax.ShapeDtypeStruct((2, t), jnp.int32), jax.ShapeDtypeStruct((2, t), jnp.float32)],
        compiler_params=_cparams(("parallel",)),
        name="router",
    )(x, w2, w_hi, router_b.astype(jnp.float32).reshape(N_EXPERTS, 1))


def dispatch_plan(idx):
    _, t = idx.shape
    nt = t // MOE_TILE
    n_chunks = _round_up(nt * CHUNKS_PER_TILE + N_EXPERTS * (CHUNKS_PER_EXPERT_TILE - 1), CHUNKS_PER_EXPERT_TILE)
    i32 = jnp.int32
    e_tile = idx.reshape(2, nt, MOE_TILE).transpose(1, 0, 2).reshape(nt, 2 * MOE_TILE)
    onehot = (e_tile[:, :, None] == jnp.arange(N_EXPERTS, dtype=i32)[None, None, :]).astype(i32)
    csum = jnp.cumsum(onehot, axis=1)
    rank = jnp.sum(csum * onehot, axis=-1) - 1
    cnt = csum[:, -1, :]
    nc = (cnt + SLOT_CHUNK - 1) // SLOT_CHUNK
    lo = jnp.cumsum(nc, axis=1) - nc
    local_pos = jnp.take_along_axis(lo * SLOT_CHUNK, e_tile, axis=1) + rank
    pos = local_pos.reshape(nt, 2, MOE_TILE).transpose(1, 0, 2).reshape(2, t).astype(i32)

    nc_t = nc.T
    e_chunks = nc_t.sum(axis=1)
    e_pad = _round_up(e_chunks, CHUNKS_PER_EXPERT_TILE)
    e_end = jnp.cumsum(e_pad)
    run_start = (e_end - e_pad)[:, None] + jnp.cumsum(nc_t, axis=1) - nc_t
    src0 = jnp.arange(nt, dtype=i32)[None, :] * CHUNKS_PER_TILE + lo.T
    rs_f, nc_f, src0_f = run_start.reshape(-1), nc_t.reshape(-1), src0.reshape(-1)
    d = jnp.arange(n_chunks, dtype=i32)
    run = jnp.maximum(jnp.searchsorted(rs_f, d, side="right") - 1, 0)
    j = d - rs_f[run]
    fwd = jnp.where(j < nc_f[run], src0_f[run] + j, 0).astype(i32)

    lc = jnp.arange(CHUNKS_PER_TILE, dtype=i32)
    e_of = jnp.sum((lc[None, :, None] >= lo[:, None, :]).astype(i32), axis=-1) - 1
    jj = lc[None, :] - jnp.take_along_axis(lo, e_of, axis=1)
    ok = jj < jnp.take_along_axis(nc, e_of, axis=1)
    dst = jnp.take_along_axis(run_start.T, e_of, axis=1) + jj
    back = jnp.where(ok, dst, 0).reshape(nt * CHUNKS_PER_TILE).astype(i32)

    n_tiles = n_chunks // CHUNKS_PER_EXPERT_TILE
    tile_first = jnp.arange(n_tiles, dtype=i32) * CHUNKS_PER_EXPERT_TILE
    tile_expert = jnp.minimum(jnp.searchsorted(e_end, tile_first, side="right"), N_EXPERTS - 1).astype(i32)
    n_active = (e_end[-1] // CHUNKS_PER_EXPERT_TILE).astype(i32).reshape(1)
    return pos, fwd, back, tile_expert, n_active


def _permute_kernel(x_ref, pos_ref, gate_ref, xs_ref, gl_ref):
    tt, d = x_ref.shape
    slot = lax.broadcasted_iota(jnp.int32, (LOCAL_SLOTS, tt), 0)
    hit1 = slot == pos_ref[0:1, :]
    hit2 = slot == pos_ref[1:2, :]
    perm = jnp.where(hit1 | hit2, 1.0, 0.0).astype(jnp.bfloat16)
    cw = min(d, 512)
    for n in range(d // cw):
        sl = slice(n * cw, (n + 1) * cw)
        xs_ref[:, sl] = jnp.dot(perm, x_ref[:, sl], preferred_element_type=jnp.float32).astype(xs_ref.dtype)
    gates = jnp.where(hit1, gate_ref[0:1, :], 0.0) + jnp.where(hit2, gate_ref[1:2, :], 0.0)
    gl_ref[...] = jnp.broadcast_to(jnp.sum(gates, axis=1, keepdims=True), gl_ref.shape)


def permute_tokens(x_bf, pos, gate):
    t, d = x_bf.shape
    nt = t // MOE_TILE
    return pl.pallas_call(
        _permute_kernel,
        grid=(nt,),
        in_specs=[pl.BlockSpec((MOE_TILE, d), lambda i: (i, 0)),
                  pl.BlockSpec((2, MOE_TILE), lambda i: (0, i)),
                  pl.BlockSpec((2, MOE_TILE), lambda i: (0, i))],
        out_specs=[pl.BlockSpec((LOCAL_SLOTS, d), lambda i: (i, 0)),
                   pl.BlockSpec((LOCAL_SLOTS, LANES), lambda i: (i, 0))],
        out_shape=[jax.ShapeDtypeStruct((nt * LOCAL_SLOTS, d), jnp.bfloat16),
                   jax.ShapeDtypeStruct((nt * LOCAL_SLOTS, LANES), jnp.float32)],
        compiler_params=_cparams(("parallel",)),
        name="permute_tokens",
    )(x_bf, pos, gate)


def _unpermute_kernel(ys_ref, gl_ref, p1_ref, p2_ref, y_ref):
    tt, d = y_ref.shape
    slot = lax.broadcasted_iota(jnp.int32, (tt, LOCAL_SLOTS), 1)
    hit = (slot == p1_ref[:, 0:1]) | (slot == p2_ref[:, 0:1])
    perm = jnp.where(hit, 1.0, 0.0).astype(jnp.bfloat16)
    g = gl_ref[:, 0:1]
    cw = min(d, 512)
    for n in range(d // cw):
        sl = slice(n * cw, (n + 1) * cw)
        scaled = (ys_ref[:, sl].astype(jnp.float32) * g).astype(jnp.bfloat16)
        y_ref[:, sl] = jnp.dot(perm, scaled, preferred_element_type=jnp.float32)


def unpermute_tokens(ys1, gate_local, pos):
    _, t = pos.shape
    d = ys1.shape[1]
    nt = t // MOE_TILE
    p1 = jnp.broadcast_to(pos[0][:, None], (t, LANES))
    p2 = jnp.broadcast_to(pos[1][:, None], (t, LANES))
    return pl.pallas_call(
        _unpermute_kernel,
        grid=(nt,),
        in_specs=[pl.BlockSpec((LOCAL_SLOTS, d), lambda i: (i, 0)),
                  pl.BlockSpec((LOCAL_SLOTS, LANES), lambda i: (i, 0)),
                  pl.BlockSpec((MOE_TILE, LANES), lambda i: (i, 0)),
                  pl.BlockSpec((MOE_TILE, LANES), lambda i: (i, 0))],
        out_specs=pl.BlockSpec((MOE_TILE, d), lambda i: (i, 0)),
        out_shape=jax.ShapeDtypeStruct((t, d), jnp.float32),
        compiler_params=_cparams(("parallel",)),
        name="unpermute_tokens",
    )(ys1, gate_local, p1, p2)


def _gather_kernel(idx_ref, src_ref, dst_ref, sems, *, n_items):
    n_waves = n_items // GATHER_WAVE

    def item_copy(item, slot):
        return pltpu.make_async_copy(src_ref.at[pl.ds(idx_ref[item], 1)], dst_ref.at[pl.ds(item, 1)], sems.at[slot])

    def issue(w, slot):
        def body(r, _):
            item_copy(w * GATHER_WAVE + r, slot).start()
            return 0
        lax.fori_loop(0, GATHER_WAVE, body, 0)

    def drain(w, slot):
        def body(r, _):
            item_copy(w * GATHER_WAVE + r, slot).wait()
            return 0
        lax.fori_loop(0, GATHER_WAVE, body, 0)

    issue(0, 0)

    def step(w, _):
        slot = w % 2

        @pl.when(w + 1 < n_waves)
        def _():
            issue(w + 1, 1 - slot)

        drain(w, slot)
        return 0

    lax.fori_loop(0, n_waves, step, 0)


def gather_chunks(src, idx):
    rows, d = src.shape
    n = idx.shape[0]
    assert n % GATHER_WAVE == 0 and rows % SLOT_CHUNK == 0
    out = pl.pallas_call(
        functools.partial(_gather_kernel, n_items=n),
        grid_spec=pltpu.PrefetchScalarGridSpec(
            num_scalar_prefetch=1,
            grid=(1,),
            in_specs=[pl.BlockSpec(memory_space=pl.ANY)],
            out_specs=pl.BlockSpec(memory_space=pl.ANY),
            scratch_shapes=[pltpu.SemaphoreType.DMA((2,))]),
        out_shape=jax.ShapeDtypeStruct((n, SLOT_CHUNK, d), src.dtype),
        compiler_params=_cparams(("arbitrary",)),
        name="gather_chunks",
    )(idx, src.reshape(rows // SLOT_CHUNK, SLOT_CHUNK, d))
    return out.reshape(n * SLOT_CHUNK, d)


def _expert_up_kernel(te_ref, na_ref, x_ref, w1_ref, w3_ref, h_ref):
    @pl.when(pl.program_id(1) < na_ref[0])
    def _():
        x = x_ref[...]
        h1 = jnp.dot(x, w1_ref[...].astype(jnp.bfloat16), preferred_element_type=jnp.float32)
        h3 = jnp.dot(x, w3_ref[...].astype(jnp.bfloat16), preferred_element_type=jnp.float32)
        h_ref[...] = (jax.nn.silu(h1) * h3).astype(h_ref.dtype)


def _expert_down_kernel(te_ref, na_ref, h_ref, w2_ref, y_ref):
    @pl.when(pl.program_id(1) < na_ref[0])
    def _():
        y_ref[...] = jnp.dot(h_ref[...], w2_ref[...].astype(jnp.bfloat16),
                             preferred_element_type=jnp.float32).astype(y_ref.dtype)


def expert_ffn(xs, w1, w3, w2, tile_expert, n_active, tf=512, tn=1024):
    s, d = xs.shape
    f = w1.shape[2]
    tm = EXPERT_TILE
    tf, tn = min(tf, f), min(tn, d)
    n_tiles = s // tm

    def act(i, na):
        return jnp.minimum(i, na[0] - 1)

    up = pl.pallas_call(
        _expert_up_kernel,
        grid_spec=pltpu.PrefetchScalarGridSpec(
            num_scalar_prefetch=2,
            grid=(f // tf, n_tiles),
            in_specs=[pl.BlockSpec((tm, d), lambda j, i, te, na: (act(i, na), 0)),
                      pl.BlockSpec((None, d, tf), lambda j, i, te, na: (te[act(i, na)], 0, j)),
                      pl.BlockSpec((None, d, tf), lambda j, i, te, na: (te[act(i, na)], 0, j))],
            out_specs=pl.BlockSpec((tm, tf), lambda j, i, te, na: (act(i, na), j))),
        out_shape=jax.ShapeDtypeStruct((s, f), jnp.bfloat16),
        compiler_params=_cparams(("arbitrary", "arbitrary")),
        name="expert_up",
    )
    h = up(tile_expert, n_active, xs, w1, w3)
    down = pl.pallas_call(
        _expert_down_kernel,
        grid_spec=pltpu.PrefetchScalarGridSpec(
            num_scalar_prefetch=2,
            grid=(d // tn, n_tiles),
            in_specs=[pl.BlockSpec((tm, f), lambda j, i, te, na: (act(i, na), 0)),
                      pl.BlockSpec((None, f, tn), lambda j, i, te, na: (te[act(i, na)], 0, j))],
            out_specs=pl.BlockSpec((tm, tn), lambda j, i, te, na: (act(i, na), j))),
        out_shape=jax.ShapeDtypeStruct((s, d), jnp.bfloat16),
        compiler_params=_cparams(("arbitrary", "arbitrary")),
        name="expert_down",
    )
    return down(tile_expert, n_active, h, w2)


def moe_layer(x, x_bf, router_w, router_b, w1, w3, w2, g, b, out_rows=None):
    idx, gate = router(x, router_w, router_b)
    pos, fwd, back, tile_expert, n_active = dispatch_plan(idx)
    xs1, gate_local = permute_tokens(x_bf, pos, gate)
    xs = gather_chunks(xs1, fwd)
    ys = expert_ffn(xs, w1, w3, w2, tile_expert, n_active)
    ys1 = gather_chunks(ys, back)
    y = unpermute_tokens(ys1, gate_local, pos)
    if out_rows is None:
        return resid_ln(x, y, g, b)
    return [resid_ln(x, y, g, b, rows=r)[0] for r in out_rows]


def alibi_slopes(n):
    return jnp.exp2(-8.0 * jnp.arange(1, n + 1, dtype=jnp.float32) / n)


def trunk(x, segs, out_rows, w_in_ab, w_out_ab, a_sink, b_rpb, w_in_c, w_out_c, router_w, router_b,
          moe_w1, moe_w3, moe_w2, ln_g, ln_b):
    bf = jnp.bfloat16
    x_bf = x.astype(bf)
    slopes_a = alibi_slopes(A_HEADS)
    slopes_c = alibi_slopes(C_HEADS)
    dils = tuple(d for _, d in C_BRANCHES)
    for layer in range(DEPTH):
        i = layer // 2
        if layer % 2 == 0:
            qkv = matmul([x_bf], w_in_ab[i].astype(bf), bf)
            oa = attention_a(qkv, slopes_a, a_sink[i].astype(jnp.float32), segs)
            ob = attention_b(qkv, natten_bias_table(b_rpb[i]), segs)
            mix = matmul([oa, ob], w_out_ab[i].astype(bf), jnp.float32)
        else:
            zs = matmul_dilated(x_bf, w_in_c[i].astype(bf), dils[1:])
            zs = [zs[0].reshape((1,) + zs[0].shape)] + list(zs[1:])
            oc = attention_c(zs, slopes_c, segs)
            mix = matmul([oc], w_out_c[i].astype(bf), jnp.float32)
        x, x_bf = resid_ln(x, mix, ln_g[layer, 0], ln_b[layer, 0])
        last = layer == DEPTH - 1
        res = moe_layer(x, x_bf, router_w, router_b, moe_w1[layer], moe_w3[layer], moe_w2[layer],
                        ln_g[layer, 1], ln_b[layer, 1], out_rows=out_rows if last else None)
        if last:
            return res
        x, x_bf = res


def kernel(x_prompt, x_sample, w_in_ab, w_out_ab, a_sink, b_rpb, w_in_c, w_out_c, router_w, router_b,
           moe_w1, moe_w3, moe_w2, ln_g, ln_b):
    bp, sp, d = x_prompt.shape
    bs, ss, _ = x_sample.shape
    segs = ((bp, sp), (bs, ss))
    x = jnp.concatenate([x_prompt.reshape(bp * sp, d), x_sample.reshape(bs * ss, d)], axis=0)
    out_rows = ((0, bp * sp), (bp * sp, bs * ss))
    yp, ys = trunk(x, segs, out_rows, w_in_ab, w_out_ab, a_sink, b_rpb, w_in_c, w_out_c, router_w, router_b,
                   moe_w1, moe_w3, moe_w2, ln_g, ln_b)
    return (yp.reshape(bp, sp, d), ys.reshape(bs, ss, d))
```

```python
import functools

import numpy as np
import jax
import jax.numpy as jnp
from jax import lax
from jax.experimental import pallas as pl
from jax.experimental.pallas import tpu as pltpu

HEAD_DIM = 128
GRID_W = 64
A_HEADS = 16
A_KV_HEADS = 4
A_GROUP = A_HEADS // A_KV_HEADS
A_HALF_WINDOW = 128
B_HEADS = 16
NA_KH = 8
NA_KW = 16
C_HEADS = 32
C_BRANCHES = ((128, 1), (512, 4), (2048, 16))
C_HALF = 64
N_EXPERTS = 16
N_GROUPS = 4
EXPERTS_PER_GROUP = N_EXPERTS // N_GROUPS
DEPTH = 4
DEEPNORM_ALPHA = (2 * DEPTH) ** 0.25
LN_EPS = 1e-5
NEG_INF = -1e30

A_Q = A_HEADS * HEAD_DIM
A_KV = A_KV_HEADS * HEAD_DIM
B_W = B_HEADS * HEAD_DIM
C_W = C_HEADS * HEAD_DIM

LANES = 128
BF16_SUBLANES = 16
Q_BLOCK = 128
C_HEAD_GROUP = 8
C_GROUPS = C_HEADS // C_HEAD_GROUP
B_HEAD_GROUP = 8
MERGE_TILE = 256
MOE_TILE = 512
SLOT_CHUNK = BF16_SUBLANES
LOCAL_SLOTS = 2 * MOE_TILE + N_EXPERTS * SLOT_CHUNK
CHUNKS_PER_TILE = LOCAL_SLOTS // SLOT_CHUNK
EXPERT_TILE = 512
CHUNKS_PER_EXPERT_TILE = EXPERT_TILE // SLOT_CHUNK
GATHER_WAVE = 32
VMEM_LIMIT = 56 * 1024 * 1024

_NT = (((1,), (1,)), ((), ()))


def _cparams(sem):
    return pltpu.CompilerParams(dimension_semantics=sem, vmem_limit_bytes=VMEM_LIMIT)


def _round_up(x, m):
    return (x + m - 1) // m * m


def _seg_bounds(pos, bounds):
    s = jnp.int32(bounds[0][0])
    e = jnp.int32(bounds[0][1])
    for a, b in bounds[1:]:
        inside = pos >= a
        s = jnp.where(inside, jnp.int32(a), s)
        e = jnp.where(inside, jnp.int32(b), e)
    return s, e


def _bounds(segs, unit):
    out, pos = [], 0
    for n, length in segs:
        for _ in range(n):
            out.append((pos // unit, (pos + length) // unit))
            pos += length
    return tuple(out)


def _matmul_kernel(*refs, widths):
    xs, w_ref, o_ref = refs[:len(widths)], refs[len(widths)], refs[len(widths) + 1]
    acc, off = None, 0
    for x_ref, k in zip(xs, widths):
        part = jnp.dot(x_ref[...], w_ref[off:off + k, :], preferred_element_type=jnp.float32)
        acc = part if acc is None else acc + part
        off += k
    o_ref[...] = acc.astype(o_ref.dtype)


def matmul(xs, w, out_dtype, tm=1024, tn=512):
    t = xs[0].shape[0]
    k, n = w.shape
    widths = tuple(x.shape[1] for x in xs)
    assert sum(widths) == k
    tm, tn = min(tm, t), min(tn, n)
    assert t % tm == 0 and n % tn == 0
    in_specs = [pl.BlockSpec((tm, kw), lambda i, j: (i, 0)) for kw in widths]
    in_specs.append(pl.BlockSpec((k, tn), lambda i, j: (0, j)))
    return pl.pallas_call(
        functools.partial(_matmul_kernel, widths=widths),
        grid=(t // tm, n // tn),
        in_specs=in_specs,
        out_specs=pl.BlockSpec((tm, tn), lambda i, j: (i, j)),
        out_shape=jax.ShapeDtypeStruct((t, n), out_dtype),
        compiler_params=_cparams(("parallel", "arbitrary")),
        name="matmul",
    )(*xs, w)


def _matmul_dilated_kernel(x_ref, w_ref, o_ref, *rest, dils):
    z_refs, acc_ref = rest[:-1], rest[-1]
    tm, tn = o_ref.shape
    acc = jnp.dot(x_ref[...], w_ref[...], preferred_element_type=jnp.float32)
    o_ref[...] = acc.astype(o_ref.dtype)
    for s in range(tn // LANES):
        acc_ref[s] = acc[:, s * LANES:(s + 1) * LANES]
    for z_ref, d in zip(z_refs, dils):
        for rho in range(d):
            for s in range(tn // LANES):
                z_ref[rho, :, s * LANES:(s + 1) * LANES] = (
                    acc_ref[s, pl.ds(rho, tm // d, stride=d), :].astype(z_ref.dtype))


def matmul_dilated(x, w, dils, tm=1024, tn=512):
    t, k = x.shape
    n = w.shape[1]
    tm, tn = min(tm, t), min(tn, n)
    assert t % tm == 0 and n % tn == 0 and all(tm % (d * BF16_SUBLANES) == 0 for d in dils)
    out_specs = [pl.BlockSpec((tm, tn), lambda i, j: (i, j))]
    out_shape = [jax.ShapeDtypeStruct((t, n), jnp.bfloat16)]
    for d in dils:
        out_specs.append(pl.BlockSpec((d, tm // d, tn), lambda i, j: (0, i, j)))
        out_shape.append(jax.ShapeDtypeStruct((d, t // d, n), jnp.bfloat16))
    return pl.pallas_call(
        functools.partial(_matmul_dilated_kernel, dils=tuple(dils)),
        grid=(t // tm, n // tn),
        in_specs=[pl.BlockSpec((tm, k), lambda i, j: (i, 0)), pl.BlockSpec((k, tn), lambda i, j: (0, j))],
        out_specs=out_specs,
        out_shape=out_shape,
        scratch_shapes=[pltpu.VMEM((tn // LANES, tm, LANES), jnp.float32)],
        compiler_params=_cparams(("parallel", "arbitrary")),
        name="matmul_dilated",
    )(x, w)


def _resid_ln_kernel(x_ref, y_ref, g_ref, b_ref, o_ref, obf_ref):
    z = DEEPNORM_ALPHA * x_ref[...] + y_ref[...]
    mu = jnp.mean(z, axis=-1, keepdims=True)
    zc = z - mu
    var = jnp.mean(zc * zc, axis=-1, keepdims=True)
    out = zc * lax.rsqrt(var + LN_EPS) * g_ref[...] + b_ref[...]
    o_ref[...] = out
    obf_ref[...] = out.astype(jnp.bfloat16)


def resid_ln(x, y, g, b, rows=None, tm=128):
    t, d = x.shape
    start, count = (0, t) if rows is None else rows
    tm = min(tm, count)
    assert count % tm == 0 and start % tm == 0
    first = start // tm
    row_in = pl.BlockSpec((tm, d), lambda i: (first + i, 0))
    row_out = pl.BlockSpec((tm, d), lambda i: (i, 0))
    vec = pl.BlockSpec((1, d), lambda i: (0, 0))
    return pl.pallas_call(
        _resid_ln_kernel,
        grid=(count // tm,),
        in_specs=[row_in, row_in, vec, vec],
        out_specs=[row_out, row_out],
        out_shape=[jax.ShapeDtypeStruct((count, d), jnp.float32), jax.ShapeDtypeStruct((count, d), jnp.bfloat16)],
        compiler_params=_cparams(("parallel",)),
        name="resid_ln",
    )(x, y, g.reshape(1, d), b.reshape(1, d))


def _concat_cast_kernel(a_ref, b_ref, o_ref, obf_ref, *, a_blocks):
    i = pl.program_id(0)

    @pl.when(i < a_blocks)
    def _():
        o_ref[...] = a_ref[...]
        obf_ref[...] = a_ref[...].astype(jnp.bfloat16)

    @pl.when(i >= a_blocks)
    def _():
        o_ref[...] = b_ref[...]
        obf_ref[...] = b_ref[...].astype(jnp.bfloat16)


def concat_cast(a, b, tm=256):
    (ta, d), tb = a.shape, b.shape[0]
    tm = min(tm, ta, tb)
    assert ta % tm == 0 and tb % tm == 0
    a_blocks = ta // tm
    row = pl.BlockSpec((tm, d), lambda i: (i, 0))
    return pl.pallas_call(
        functools.partial(_concat_cast_kernel, a_blocks=a_blocks),
        grid=((ta + tb) // tm,),
        in_specs=[pl.BlockSpec((tm, d), lambda i: (jnp.minimum(i, a_blocks - 1), 0)),
                  pl.BlockSpec((tm, d), lambda i: (jnp.maximum(i - a_blocks, 0), 0))],
        out_specs=[row, row],
        out_shape=[jax.ShapeDtypeStruct((ta + tb, d), jnp.float32), jax.ShapeDtypeStruct((ta + tb, d), jnp.bfloat16)],
        compiler_params=_cparams(("arbitrary",)),
        name="concat_cast",
    )(a, b)


def _banded_kernel(*refs, half, nk, group, step, bounds, use_sink, emit_lse, row_axis, head_axis):
    refs = list(refs)
    slopes_ref = refs.pop(0)
    sink_ref = refs.pop(0) if use_sink else None
    q_ref, kp_ref, kc_ref, kn_ref, vp_ref, vc_ref, vn_ref = refs[:7]
    o_ref = refs[7]
    lse_ref = refs[8] if emit_lse else None

    ub = pl.program_id(row_axis)
    hgi = pl.program_id(head_axis)
    s0, e0 = _seg_bounds(ub * Q_BLOCK, bounds)
    has_prev = ub * Q_BLOCK > s0
    has_next = (ub + 1) * Q_BLOCK < e0
    win = Q_BLOCK + 2 * half
    row = lax.broadcasted_iota(jnp.int32, (Q_BLOCK, win), 0)
    col = lax.broadcasted_iota(jnp.int32, (Q_BLOCK, win), 1)
    dist = jnp.abs(col - half - row)
    valid = (dist <= half) & ((col >= half) | has_prev) & ((col < half + Q_BLOCK) | has_next)
    distf = (step * dist).astype(jnp.float32)
    scale = HEAD_DIM ** -0.5
    lane = lax.broadcasted_iota(jnp.int32, (Q_BLOCK, LANES), 1)
    lse_tile = jnp.zeros((Q_BLOCK, LANES), jnp.float32)

    for k in range(nk):
        ksl = slice(k * HEAD_DIM, (k + 1) * HEAD_DIM)
        kwin = jnp.concatenate([kp_ref[:, ksl], kc_ref[:, ksl], kn_ref[:, ksl]], axis=0)
        vwin = jnp.concatenate([vp_ref[:, ksl], vc_ref[:, ksl], vn_ref[:, ksl]], axis=0)
        for g in range(group):
            hidx = k * group + g
            hsl = slice(hidx * HEAD_DIM, (hidx + 1) * HEAD_DIM)
            head = hgi * (nk * group) + hidx
            s = lax.dot_general(q_ref[:, hsl], kwin, _NT, preferred_element_type=jnp.float32) * scale
            s = jnp.where(valid, s - slopes_ref[head] * distf, NEG_INF)
            m = jnp.max(s, axis=-1, keepdims=True)
            if use_sink:
                sk = sink_ref[head]
                m = jnp.maximum(m, sk)
            p = jnp.exp(s - m)
            l = jnp.sum(p, axis=-1, keepdims=True)
            if use_sink:
                l = l + jnp.exp(sk - m)
            o = jnp.dot(p.astype(jnp.bfloat16), vwin, preferred_element_type=jnp.float32) / l
            if emit_lse:
                lse_tile = jnp.where(lane == hidx, m + jnp.log(l), lse_tile)
            o_ref[:, hsl] = o.astype(o_ref.dtype)
    if emit_lse:
        lse_ref[...] = lse_tile


def attention_a(qkv, slopes, sink, segs):
    t = qkv.shape[0]
    nblk = t // Q_BLOCK
    bounds = _bounds(segs, 1)
    qw, kw = A_Q, A_KV
    k_col, v_col = A_Q // kw, (A_Q + A_KV) // kw
    prev = lambda b: jnp.maximum(b - 1, 0)
    nxt = lambda b: jnp.minimum(b + 1, nblk - 1)
    smem = pl.BlockSpec(memory_space=pltpu.SMEM)
    in_specs = [smem, smem,
                pl.BlockSpec((Q_BLOCK, qw), lambda b, h: (b, 0)),
                pl.BlockSpec((Q_BLOCK, kw), lambda b, h: (prev(b), k_col)),
                pl.BlockSpec((Q_BLOCK, kw), lambda b, h: (b, k_col)),
                pl.BlockSpec((Q_BLOCK, kw), lambda b, h: (nxt(b), k_col)),
                pl.BlockSpec((Q_BLOCK, kw), lambda b, h: (prev(b), v_col)),
                pl.BlockSpec((Q_BLOCK, kw), lambda b, h: (b, v_col)),
                pl.BlockSpec((Q_BLOCK, kw), lambda b, h: (nxt(b), v_col))]
    kern = functools.partial(_banded_kernel, half=A_HALF_WINDOW, nk=A_KV_HEADS, group=A_GROUP, step=1,
                             bounds=bounds, use_sink=True, emit_lse=False, row_axis=0, head_axis=1)
    return pl.pallas_call(
        kern,
        grid=(nblk, 1),
        in_specs=in_specs,
        out_specs=pl.BlockSpec((Q_BLOCK, qw), lambda b, h: (b, 0)),
        out_shape=jax.ShapeDtypeStruct((t, qw), jnp.bfloat16),
        compiler_params=_cparams(("parallel", "arbitrary")),
        name="attention_a",
    )(slopes, sink, *([qkv] * 7))


def attention_c_branch(z, slopes, dil, segs):
    rows = z.shape[1]
    nub = rows // Q_BLOCK
    half = C_HALF
    hw = C_HEAD_GROUP * HEAD_DIM
    bounds = _bounds(segs, dil)
    k_col0, v_col0 = C_W // hw, 2 * C_W // hw
    nhalf = rows // half
    prev = lambda u: jnp.maximum(2 * u - 1, 0)
    nxt = lambda u: jnp.minimum(2 * u + 2, nhalf - 1)
    smem = pl.BlockSpec(memory_space=pltpu.SMEM)
    in_specs = [smem,
                pl.BlockSpec((None, Q_BLOCK, hw), lambda r, u, h: (r, u, h)),
                pl.BlockSpec((None, half, hw), lambda r, u, h: (r, prev(u), k_col0 + h)),
                pl.BlockSpec((None, Q_BLOCK, hw), lambda r, u, h: (r, u, k_col0 + h)),
                pl.BlockSpec((None, half, hw), lambda r, u, h: (r, nxt(u), k_col0 + h)),
                pl.BlockSpec((None, half, hw), lambda r, u, h: (r, prev(u), v_col0 + h)),
                pl.BlockSpec((None, Q_BLOCK, hw), lambda r, u, h: (r, u, v_col0 + h)),
                pl.BlockSpec((None, half, hw), lambda r, u, h: (r, nxt(u), v_col0 + h))]
    kern = functools.partial(_banded_kernel, half=half, nk=C_HEAD_GROUP, group=1, step=dil, bounds=bounds,
                             use_sink=False, emit_lse=True, row_axis=1, head_axis=2)
    return pl.pallas_call(
        kern,
        grid=(dil, nub, C_GROUPS),
        in_specs=in_specs,
        out_specs=[pl.BlockSpec((None, Q_BLOCK, hw), lambda r, u, h: (r, u, h)),
                   pl.BlockSpec((None, Q_BLOCK, LANES), lambda r, u, h: (r, u, h))],
        out_shape=[jax.ShapeDtypeStruct((dil, rows, C_W), jnp.bfloat16),
                   jax.ShapeDtypeStruct((dil, rows, C_GROUPS * LANES), jnp.float32)],
        compiler_params=_cparams(("parallel", "parallel", "arbitrary")),
        name="attention_c_d%d" % dil,
    )(slopes, *([z] * 7))


def _plane_perm(dil, tm):
    p = np.zeros((tm, tm), np.float32)
    tok = np.arange(tm)
    p[tok, (tok % dil) * (tm // dil) + tok // dil] = 1.0
    return jnp.asarray(p, jnp.bfloat16)


def _split3(x):
    hi = x.astype(jnp.bfloat16)
    r = x - hi.astype(jnp.float32)
    mid = r.astype(jnp.bfloat16)
    lo = (r - mid.astype(jnp.float32)).astype(jnp.bfloat16)
    return hi, mid, lo


def _merge_kernel(*refs, dils):
    nb = len(dils)
    o_refs, l_refs = refs[:nb], refs[nb:2 * nb]
    p_refs = refs[2 * nb:2 * nb + nb - 1]
    out_ref = refs[-1]
    outs, lses = [o_refs[0][0].astype(jnp.float32)], [l_refs[0][0]]
    for b in range(1, nb):
        d = dils[b]
        perm = p_refs[b - 1][...]
        stacked = jnp.concatenate([o_refs[b][r] for r in range(d)], axis=0)
        outs.append(jnp.dot(perm, stacked, preferred_element_type=jnp.float32))
        lstack = jnp.concatenate([l_refs[b][r] for r in range(d)], axis=0)
        lses.append(sum(jnp.dot(perm, part, preferred_element_type=jnp.float32) for part in _split3(lstack)))
    for h in range(C_HEADS):
        hsl = slice(h * HEAD_DIM, (h + 1) * HEAD_DIM)
        col = (h // C_HEAD_GROUP) * LANES + h % C_HEAD_GROUP
        ls = [l[:, col:col + 1] for l in lses]
        m = functools.reduce(jnp.maximum, ls)
        es = [jnp.exp(l - m) for l in ls]
        tot = sum(es)
        acc = sum((e / tot) * o[:, hsl] for e, o in zip(es, outs))
        out_ref[:, hsl] = acc.astype(out_ref.dtype)


def merge_branches(os_, ls_, dils):
    t = os_[0].shape[0] * os_[0].shape[1]
    tm = min(MERGE_TILE, t)
    in_specs, args = [], []
    for arr, width in ((os_, C_W), (ls_, C_GROUPS * LANES)):
        for a, d in zip(arr, dils):
            in_specs.append(pl.BlockSpec((d, tm // d, width), lambda i: (0, i, 0)))
            args.append(a)
    for d in dils[1:]:
        in_specs.append(pl.BlockSpec((tm, tm), lambda i: (0, 0)))
        args.append(_plane_perm(d, tm))
    return pl.pallas_call(
        functools.partial(_merge_kernel, dils=tuple(dils)),
        grid=(t // tm,),
        in_specs=in_specs,
        out_specs=pl.BlockSpec((tm, C_W), lambda i: (i, 0)),
        out_shape=jax.ShapeDtypeStruct((t, C_W), jnp.bfloat16),
        compiler_params=_cparams(("parallel",)),
        name="merge_branches",
    )(*args)


def attention_c(zs, slopes, segs):
    dils = tuple(d for _, d in C_BRANCHES)
    res = [attention_c_branch(z, slopes, d, segs) for z, d in zip(zs, dils)]
    return merge_branches([r[0] for r in res], [r[1] for r in res], dils)


NA_PAIR = 2
NA_WIN_ROWS = NA_KH + NA_PAIR
NA_WIN_BLOCKS = NA_WIN_ROWS // NA_PAIR
NA_PAIR_CLASSES = ((0, 3, 1, 3), (0, 7, 0, 6), (0, 5, 0, 4), (2, 3, 2, 2), (2, 1, 2, 0))


def _natten_kernel(*refs):
    q_ref = refs[0]
    k_refs = refs[1:1 + NA_WIN_BLOCKS]
    v_refs = refs[1 + NA_WIN_BLOCKS:1 + 2 * NA_WIN_BLOCKS]
    bias_ref, o_ref = refs[1 + 2 * NA_WIN_BLOCKS:]
    scale = HEAD_DIM ** -0.5
    for h in range(B_HEAD_GROUP):
        hsl = slice(h * HEAD_DIM, (h + 1) * HEAD_DIM)
        kwin = jnp.concatenate([r[:, hsl] for r in k_refs], axis=0)
        vwin = jnp.concatenate([r[:, hsl] for r in v_refs], axis=0)
        s = lax.dot_general(q_ref[:, hsl], kwin, _NT, preferred_element_type=jnp.float32) * scale + bias_ref[h]
        m = jnp.max(s, axis=-1, keepdims=True)
        p = jnp.exp(s - m)
        l = jnp.sum(p, axis=-1, keepdims=True)
        o = jnp.dot(p.astype(jnp.bfloat16), vwin, preferred_element_type=jnp.float32) / l
        o_ref[:, hsl] = o.astype(o_ref.dtype)


def natten_bias_table(rpb):
    ncls = len(NA_PAIR_CLASSES)
    sel = np.zeros((ncls, NA_PAIR, NA_WIN_ROWS, 2 * NA_KH - 1), np.float32)
    row_ok = np.zeros((ncls, NA_PAIR, NA_WIN_ROWS), bool)
    for ci, cls in enumerate(NA_PAIR_CLASSES):
        for a in range(NA_PAIR):
            off, o = cls[2 * a], cls[2 * a + 1]
            for i in range(NA_KH):
                sel[ci, a, off + i, o + i] = 1.0
                row_ok[ci, a, off + i] = True
    c = jnp.arange(GRID_W)
    cs = jnp.clip(c - NA_KW // 2, 0, GRID_W - NA_KW)
    col_ok = (c[None, :] >= cs[:, None]) & (c[None, :] < cs[:, None] + NA_KW)
    dc = jnp.clip(c[None, :] - c[:, None] + (NA_KW - 1), 0, 2 * NA_KW - 2)
    pick = (dc[:, :, None] == jnp.arange(2 * NA_KW - 1)[None, None, :]).astype(jnp.float32)
    hi = lax.Precision.HIGHEST
    rows = jnp.einsum("capd,hdx->caphx", jnp.asarray(sel), rpb.astype(jnp.float32), precision=hi)
    tab = jnp.einsum("caphx,qkx->chaqpk", rows, pick, precision=hi)
    ok = jnp.asarray(row_ok)[:, None, :, None, :, None] & col_ok[None, None, None, :, None, :]
    tab = jnp.where(ok, tab, NEG_INF)
    return tab.reshape(ncls, B_HEADS, NA_PAIR * GRID_W, NA_WIN_ROWS * GRID_W)


def attention_b(qkv, bias_tab, segs):
    t = qkv.shape[0]
    nrows = t // GRID_W
    bounds = _bounds(segs, GRID_W)
    hw = B_HEAD_GROUP * HEAD_DIM
    ngroups = B_HEADS // B_HEAD_GROUP
    q_col0 = (A_Q + 2 * A_KV) // hw
    k_col0 = q_col0 + B_W // hw
    v_col0 = k_col0 + B_W // hw

    assert all(re - rs >= 2 * NA_KH and (re - rs) % NA_PAIR == 0 for rs, re in bounds)
    blk = NA_PAIR * GRID_W

    def win_block(pr):
        rs, re = _seg_bounds(pr * NA_PAIR, bounds)
        return jnp.clip(pr * NA_PAIR - NA_KH // 2, rs, re - NA_WIN_ROWS) // NA_PAIR

    def pair_class(pr):
        rs, re = _seg_bounds(pr * NA_PAIR, bounds)
        from_start, to_end = pr * NA_PAIR - rs, re - pr * NA_PAIR
        return jnp.where(from_start == 0, 1, jnp.where(from_start == 2, 2,
                         jnp.where(to_end == 4, 3, jnp.where(to_end == 2, 4, 0))))

    def kv_spec(col0, i):
        return pl.BlockSpec((blk, hw), lambda h, pr: (win_block(pr) + i, col0 + h))

    in_specs = [pl.BlockSpec((blk, hw), lambda h, pr: (pr, q_col0 + h))]
    in_specs += [kv_spec(k_col0, i) for i in range(NA_WIN_BLOCKS)]
    in_specs += [kv_spec(v_col0, i) for i in range(NA_WIN_BLOCKS)]
    in_specs.append(pl.BlockSpec((None, B_HEAD_GROUP, blk, NA_WIN_ROWS * GRID_W),
                                 lambda h, pr: (pair_class(pr), h, 0, 0)))
    return pl.pallas_call(
        _natten_kernel,
        grid=(ngroups, nrows // NA_PAIR),
        in_specs=in_specs,
        out_specs=pl.BlockSpec((blk, hw), lambda h, pr: (pr, h)),
        out_shape=jax.ShapeDtypeStruct((t, B_W), jnp.bfloat16),
        compiler_params=_cparams(("parallel", "arbitrary")),
        name="attention_b",
    )(*([qkv] * (1 + 2 * NA_WIN_BLOCKS)), bias_tab)


def _top2(vals):
    v1, i1 = vals[0], jnp.zeros(vals[0].shape, jnp.int32)
    for j in range(1, len(vals)):
        better = vals[j] > v1
        v1 = jnp.where(better, vals[j], v1)
        i1 = jnp.where(better, j, i1)
    v2, i2 = None, None
    for j in range(len(vals)):
        cand = jnp.where(i1 == j, -1.0, vals[j])
        if v2 is None:
            v2, i2 = cand, jnp.zeros(vals[0].shape, jnp.int32)
        else:
            better = cand > v2
            v2 = jnp.where(better, cand, v2)
            i2 = jnp.where(better, j, i2)
    return v1, i1, v2, i2


def _router_kernel(x_ref, w2_ref, whi_ref, b_ref, idx_ref, gate_ref):
    x = x_ref[...]
    x_hi = x.astype(jnp.bfloat16)
    x_lo = (x - x_hi.astype(jnp.float32)).astype(jnp.bfloat16)
    r1 = lax.dot_general(w2_ref[...], x_hi, _NT, preferred_element_type=jnp.float32)
    r2 = lax.dot_general(whi_ref[...], x_lo, _NT, preferred_element_type=jnp.float32)
    logits = r1[:N_EXPERTS] + r1[N_EXPERTS:] + r2 + b_ref[...]
    m = jnp.max(logits, axis=0, keepdims=True)
    ex = jnp.exp(logits - m)
    probs = ex / jnp.sum(ex, axis=0, keepdims=True)
    rows = [probs[e:e + 1, :] for e in range(N_EXPERTS)]
    tops = [_top2(rows[g * EXPERTS_PER_GROUP:(g + 1) * EXPERTS_PER_GROUP]) for g in range(N_GROUPS)]
    score = [tp[0] + tp[2] for tp in tops]
    best, gsel = score[0], jnp.zeros(score[0].shape, jnp.int32)
    for g in range(1, N_GROUPS):
        better = score[g] > best
        best = jnp.where(better, score[g], best)
        gsel = jnp.where(better, g, gsel)
    v1, i1, v2, i2 = tops[0]
    for g in range(1, N_GROUPS):
        pick = gsel == g
        v1 = jnp.where(pick, tops[g][0], v1)
        i1 = jnp.where(pick, tops[g][1], i1)
        v2 = jnp.where(pick, tops[g][2], v2)
        i2 = jnp.where(pick, tops[g][3], i2)
    tot = v1 + v2
    idx_ref[0:1, :] = gsel * EXPERTS_PER_GROUP + i1
    idx_ref[1:2, :] = gsel * EXPERTS_PER_GROUP + i2
    gate_ref[0:1, :] = v1 / tot
    gate_ref[1:2, :] = v2 / tot


def router(x, router_w, router_b, tm=512):
    t, d = x.shape
    tm = min(tm, t)
    wt = router_w.astype(jnp.float32).T
    w_hi = wt.astype(jnp.bfloat16)
    w_lo = (wt - w_hi.astype(jnp.float32)).astype(jnp.bfloat16)
    w2 = jnp.concatenate([w_hi, w_lo], axis=0)
    return pl.pallas_call(
        _router_kernel,
        grid=(t // tm,),
        in_specs=[pl.BlockSpec((tm, d), lambda i: (i, 0)),
                  pl.BlockSpec((2 * N_EXPERTS, d), lambda i: (0, 0)),
                  pl.BlockSpec((N_EXPERTS, d), lambda i: (0, 0)),
                  pl.BlockSpec((N_EXPERTS, 1), lambda i: (0, 0))],
        out_specs=[pl.BlockSpec((2, tm), lambda i: (0, i))] * 2,
        out_shape=[jax.ShapeDtypeStruct((2, t), jnp.int32), jax.ShapeDtypeStruct((2, t), jnp.float32)],
        compiler_params=_cparams(("parallel",)),
        name="router",
    )(x, w2, w_hi, router_b.astype(jnp.float32).reshape(N_EXPERTS, 1))


def dispatch_plan(idx):
    _, t = idx.shape
    nt = t // MOE_TILE
    n_chunks = _round_up(nt * CHUNKS_PER_TILE + N_EXPERTS * (CHUNKS_PER_EXPERT_TILE - 1), CHUNKS_PER_EXPERT_TILE)
    i32 = jnp.int32
    e_tile = idx.reshape(2, nt, MOE_TILE).transpose(1, 0, 2).reshape(nt, 2 * MOE_TILE)
    hit = e_tile[:, :, None] == jnp.arange(N_EXPERTS, dtype=i32)[None, None, :]
    onehot = hit.astype(i32)
    tri = (jnp.arange(2 * MOE_TILE)[:, None] >= jnp.arange(2 * MOE_TILE)[None, :]).astype(jnp.bfloat16)
    csum = jnp.einsum("ab,nbe->nae", tri, hit.astype(jnp.bfloat16), preferred_element_type=jnp.float32).astype(i32)
    rank = jnp.sum(csum * onehot, axis=-1) - 1
    cnt = csum[:, -1, :]
    nc = (cnt + SLOT_CHUNK - 1) // SLOT_CHUNK
    lo = jnp.cumsum(nc, axis=1) - nc
    local_pos = jnp.sum(onehot * (lo * SLOT_CHUNK)[:, None, :], axis=-1) + rank
    pos = local_pos.reshape(nt, 2, MOE_TILE).transpose(1, 0, 2).reshape(2, t).astype(i32)

    nc_t = nc.T
    e_chunks = nc_t.sum(axis=1)
    e_pad = _round_up(e_chunks, CHUNKS_PER_EXPERT_TILE)
    e_end = jnp.cumsum(e_pad)
    run_start = (e_end - e_pad)[:, None] + jnp.cumsum(nc_t, axis=1) - nc_t
    src0 = jnp.arange(nt, dtype=i32)[None, :] * CHUNKS_PER_TILE + lo.T
    rs_f, nc_f, src0_f = run_start.reshape(-1), nc_t.reshape(-1), src0.reshape(-1)
    d = jnp.arange(n_chunks, dtype=i32)
    run = jnp.maximum(jnp.searchsorted(rs_f, d, side="right") - 1, 0)
    j = d - rs_f[run]
    fwd = jnp.where(j < nc_f[run], src0_f[run] + j, 0).astype(i32)

    lc = jnp.arange(CHUNKS_PER_TILE, dtype=i32)
    e_of = jnp.sum((lc[None, :, None] >= lo[:, None, :]).astype(i32), axis=-1) - 1
    jj = lc[None, :] - jnp.take_along_axis(lo, e_of, axis=1)
    ok = jj < jnp.take_along_axis(nc, e_of, axis=1)
    dst = jnp.take_along_axis(run_start.T, e_of, axis=1) + jj
    back = jnp.where(ok, dst, 0).reshape(nt * CHUNKS_PER_TILE).astype(i32)

    n_tiles = n_chunks // CHUNKS_PER_EXPERT_TILE
    tile_first = jnp.arange(n_tiles, dtype=i32) * CHUNKS_PER_EXPERT_TILE
    tile_expert = jnp.minimum(jnp.searchsorted(e_end, tile_first, side="right"), N_EXPERTS - 1).astype(i32)
    n_active = (e_end[-1] // CHUNKS_PER_EXPERT_TILE).astype(i32).reshape(1)
    return pos, fwd, back, tile_expert, n_active


def _permute_kernel(x_ref, pos_ref, gate_ref, xs_ref, gl_ref):
    tt, d = x_ref.shape
    slot = lax.broadcasted_iota(jnp.int32, (LOCAL_SLOTS, tt), 0)
    hit1 = slot == pos_ref[0:1, :]
    hit2 = slot == pos_ref[1:2, :]
    perm = jnp.where(hit1 | hit2, 1.0, 0.0).astype(jnp.bfloat16)
    cw = min(d, 512)
    for n in range(d // cw):
        sl = slice(n * cw, (n + 1) * cw)
        xs_ref[:, sl] = jnp.dot(perm, x_ref[:, sl], preferred_element_type=jnp.float32).astype(xs_ref.dtype)
    gates = jnp.where(hit1, gate_ref[0:1, :], 0.0) + jnp.where(hit2, gate_ref[1:2, :], 0.0)
    gl_ref[...] = jnp.broadcast_to(jnp.sum(gates, axis=1, keepdims=True), gl_ref.shape)


def permute_tokens(x_bf, pos, gate):
    t, d = x_bf.shape
    nt = t // MOE_TILE
    return pl.pallas_call(
        _permute_kernel,
        grid=(nt,),
        in_specs=[pl.BlockSpec((MOE_TILE, d), lambda i: (i, 0)),
                  pl.BlockSpec((2, MOE_TILE), lambda i: (0, i)),
                  pl.BlockSpec((2, MOE_TILE), lambda i: (0, i))],
        out_specs=[pl.BlockSpec((LOCAL_SLOTS, d), lambda i: (i, 0)),
                   pl.BlockSpec((LOCAL_SLOTS, LANES), lambda i: (i, 0))],
        out_shape=[jax.ShapeDtypeStruct((nt * LOCAL_SLOTS, d), jnp.bfloat16),
                   jax.ShapeDtypeStruct((nt * LOCAL_SLOTS, LANES), jnp.float32)],
        compiler_params=_cparams(("parallel",)),
        name="permute_tokens",
    )(x_bf, pos, gate)


def _unpermute_kernel(ys_ref, gl_ref, p1_ref, p2_ref, y_ref):
    tt, d = y_ref.shape
    slot = lax.broadcasted_iota(jnp.int32, (tt, LOCAL_SLOTS), 1)
    hit = (slot == p1_ref[:, 0:1]) | (slot == p2_ref[:, 0:1])
    perm = jnp.where(hit, 1.0, 0.0).astype(jnp.bfloat16)
    g = gl_ref[:, 0:1]
    cw = min(d, 512)
    for n in range(d // cw):
        sl = slice(n * cw, (n + 1) * cw)
        scaled = (ys_ref[:, sl].astype(jnp.float32) * g).astype(jnp.bfloat16)
        y_ref[:, sl] = jnp.dot(perm, scaled, preferred_element_type=jnp.float32)


def unpermute_tokens(ys1, gate_local, pos):
    _, t = pos.shape
    d = ys1.shape[1]
    nt = t // MOE_TILE
    p1 = jnp.broadcast_to(pos[0][:, None], (t, LANES))
    p2 = jnp.broadcast_to(pos[1][:, None], (t, LANES))
    return pl.pallas_call(
        _unpermute_kernel,
        grid=(nt,),
        in_specs=[pl.BlockSpec((LOCAL_SLOTS, d), lambda i: (i, 0)),
                  pl.BlockSpec((LOCAL_SLOTS, LANES), lambda i: (i, 0)),
                  pl.BlockSpec((MOE_TILE, LANES), lambda i: (i, 0)),
                  pl.BlockSpec((MOE_TILE, LANES), lambda i: (i, 0))],
        out_specs=pl.BlockSpec((MOE_TILE, d), lambda i: (i, 0)),
        out_shape=jax.ShapeDtypeStruct((t, d), jnp.float32),
        compiler_params=_cparams(("parallel",)),
        name="unpermute_tokens",
    )(ys1, gate_local, p1, p2)


def _gather_kernel(idx_ref, src_ref, dst_ref, sem):
    base = pl.program_id(0) * GATHER_WAVE
    copies = [pltpu.make_async_copy(src_ref.at[idx_ref[base + g]], dst_ref.at[g], sem) for g in range(GATHER_WAVE)]
    for cp in copies:
        cp.start()
    for cp in copies:
        cp.wait()


def gather_chunks(src, idx):
    rows, d = src.shape
    n = idx.shape[0]
    assert n % GATHER_WAVE == 0 and rows % SLOT_CHUNK == 0
    out = pl.pallas_call(
        _gather_kernel,
        grid_spec=pltpu.PrefetchScalarGridSpec(
            num_scalar_prefetch=1,
            grid=(n // GATHER_WAVE,),
            in_specs=[pl.BlockSpec(memory_space=pl.ANY)],
            out_specs=pl.BlockSpec((GATHER_WAVE, SLOT_CHUNK, d), lambda i, idx: (i, 0, 0)),
            scratch_shapes=[pltpu.SemaphoreType.DMA(())]),
        out_shape=jax.ShapeDtypeStruct((n, SLOT_CHUNK, d), src.dtype),
        compiler_params=_cparams(("arbitrary",)),
        name="gather_chunks",
    )(idx, src.reshape(rows // SLOT_CHUNK, SLOT_CHUNK, d))
    return out.reshape(n * SLOT_CHUNK, d)


def _expert_up_kernel(te_ref, na_ref, x_ref, w1_ref, w3_ref, h_ref):
    active = pl.program_id(1) < na_ref[0]

    @pl.when(active)
    def _():
        x = x_ref[...]
        h1 = jnp.dot(x, w1_ref[...].astype(jnp.bfloat16), preferred_element_type=jnp.float32)
        h3 = jnp.dot(x, w3_ref[...].astype(jnp.bfloat16), preferred_element_type=jnp.float32)
        h_ref[...] = (jax.nn.silu(h1) * h3).astype(h_ref.dtype)

    @pl.when(jnp.logical_not(active))
    def _():
        h_ref[...] = jnp.zeros(h_ref.shape, h_ref.dtype)


def _expert_down_kernel(te_ref, na_ref, h_ref, w2_ref, y_ref):
    active = pl.program_id(1) < na_ref[0]

    @pl.when(active)
    def _():
        y_ref[...] = jnp.dot(h_ref[...], w2_ref[...].astype(jnp.bfloat16),
                             preferred_element_type=jnp.float32).astype(y_ref.dtype)

    @pl.when(jnp.logical_not(active))
    def _():
        y_ref[...] = jnp.zeros(y_ref.shape, y_ref.dtype)


def expert_ffn(xs, w1, w3, w2, layer, tile_expert, n_active, tf=512, tn=1024):
    s, d = xs.shape
    f = w1.shape[3]
    tm = EXPERT_TILE
    tf, tn = min(tf, f), min(tn, d)
    n_tiles = s // tm

    def act(i, na):
        return jnp.minimum(i, na[0] - 1)

    up = pl.pallas_call(
        _expert_up_kernel,
        grid_spec=pltpu.PrefetchScalarGridSpec(
            num_scalar_prefetch=2,
            grid=(f // tf, n_tiles),
            in_specs=[pl.BlockSpec((tm, d), lambda j, i, te, na: (act(i, na), 0)),
                      pl.BlockSpec((None, None, d, tf), lambda j, i, te, na: (layer, te[act(i, na)], 0, j)),
                      pl.BlockSpec((None, None, d, tf), lambda j, i, te, na: (layer, te[act(i, na)], 0, j))],
            out_specs=pl.BlockSpec((tm, tf), lambda j, i, te, na: (i, j))),
        out_shape=jax.ShapeDtypeStruct((s, f), jnp.bfloat16),
        compiler_params=_cparams(("arbitrary", "arbitrary")),
        name="expert_up",
    )
    h = up(tile_expert, n_active, xs, w1, w3)
    down = pl.pallas_call(
        _expert_down_kernel,
        grid_spec=pltpu.PrefetchScalarGridSpec(
            num_scalar_prefetch=2,
            grid=(d // tn, n_tiles),
            in_specs=[pl.BlockSpec((tm, f), lambda j, i, te, na: (act(i, na), 0)),
                      pl.BlockSpec((None, None, f, tn), lambda j, i, te, na: (layer, te[act(i, na)], 0, j))],
            out_specs=pl.BlockSpec((tm, tn), lambda j, i, te, na: (i, j))),
        out_shape=jax.ShapeDtypeStruct((s, d), jnp.bfloat16),
        compiler_params=_cparams(("arbitrary", "arbitrary")),
        name="expert_down",
    )
    return down(tile_expert, n_active, h, w2)


def moe_layer(x, x_bf, router_w, router_b, w1, w3, w2, layer, g, b, out_rows=None):
    idx, gate = router(x, router_w, router_b)
    pos, fwd, back, tile_expert, n_active = dispatch_plan(idx)
    xs1, gate_local = permute_tokens(x_bf, pos, gate)
    xs = gather_chunks(xs1, fwd)
    ys = expert_ffn(xs, w1, w3, w2, layer, tile_expert, n_active)
    ys1 = gather_chunks(ys, back)
    y = unpermute_tokens(ys1, gate_local, pos)
    if out_rows is None:
        return resid_ln(x, y, g, b)
    return [resid_ln(x, y, g, b, rows=r)[0] for r in out_rows]


def alibi_slopes(n):
    return jnp.exp2(-8.0 * jnp.arange(1, n + 1, dtype=jnp.float32) / n)


def trunk(x, x_bf, segs, out_rows, w_in_ab, w_out_ab, a_sink, b_rpb, w_in_c, w_out_c, router_w, router_b,
          moe_w1, moe_w3, moe_w2, ln_g, ln_b):
    bf = jnp.bfloat16
    slopes_a = alibi_slopes(A_HEADS)
    slopes_c = alibi_slopes(C_HEADS)
    dils = tuple(d for _, d in C_BRANCHES)
    for layer in range(DEPTH):
        i = layer // 2
        if layer % 2 == 0:
            qkv = matmul([x_bf], w_in_ab[i].astype(bf), bf)
            oa = attention_a(qkv, slopes_a, a_sink[i].astype(jnp.float32), segs)
            ob = attention_b(qkv, natten_bias_table(b_rpb[i]), segs)
            mix = matmul([oa, ob], w_out_ab[i].astype(bf), jnp.float32)
        else:
            zs = matmul_dilated(x_bf, w_in_c[i].astype(bf), dils[1:])
            zs = [zs[0].reshape((1,) + zs[0].shape)] + list(zs[1:])
            oc = attention_c(zs, slopes_c, segs)
            mix = matmul([oc], w_out_c[i].astype(bf), jnp.float32)
        x, x_bf = resid_ln(x, mix, ln_g[layer, 0], ln_b[layer, 0])
        last = layer == DEPTH - 1
        res = moe_layer(x, x_bf, router_w, router_b, moe_w1, moe_w3, moe_w2, layer,
                        ln_g[layer, 1], ln_b[layer, 1], out_rows=out_rows if last else None)
        if last:
            return res
        x, x_bf = res


def kernel(x_prompt, x_sample, w_in_ab, w_out_ab, a_sink, b_rpb, w_in_c, w_out_c, router_w, router_b,
           moe_w1, moe_w3, moe_w2, ln_g, ln_b):
    bp, sp, d = x_prompt.shape
    bs, ss, _ = x_sample.shape
    segs = ((bp, sp), (bs, ss))
    x, x_bf = concat_cast(x_prompt.reshape(bp * sp, d), x_sample.reshape(bs * ss, d))
    out_rows = ((0, bp * sp), (bp * sp, bs * ss))
    yp, ys = trunk(x, x_bf, segs, out_rows, w_in_ab, w_out_ab, a_sink, b_rpb, w_in_c, w_out_c, router_w, router_b,
                   moe_w1, moe_w3, moe_w2, ln_g, ln_b)
    return (yp.reshape(bp, sp, d), ys.reshape(bs, ss, d))
```

```python
import functools

import numpy as np
import jax
import jax.numpy as jnp
from jax import lax
from jax.experimental import pallas as pl
from jax.experimental.pallas import tpu as pltpu

HEAD_DIM = 128
GRID_W = 64
A_HEADS = 16
A_KV_HEADS = 4
A_GROUP = A_HEADS // A_KV_HEADS
A_HALF_WINDOW = 128
B_HEADS = 16
NA_KH = 8
NA_KW = 16
C_HEADS = 32
C_BRANCHES = ((128, 1), (512, 4), (2048, 16))
C_HALF = 64
N_EXPERTS = 16
N_GROUPS = 4
EXPERTS_PER_GROUP = N_EXPERTS // N_GROUPS
DEPTH = 4
DEEPNORM_ALPHA = (2 * DEPTH) ** 0.25
LN_EPS = 1e-5
NEG_INF = -1e30
LOG2E = 1.4426950408889634
LN2 = 0.6931471805599453

A_Q = A_HEADS * HEAD_DIM
A_KV = A_KV_HEADS * HEAD_DIM
B_W = B_HEADS * HEAD_DIM
C_W = C_HEADS * HEAD_DIM

LANES = 128
BF16_SUBLANES = 16
Q_BLOCK = 128
C_HEAD_GROUP = 8
C_GROUPS = C_HEADS // C_HEAD_GROUP
B_HEAD_GROUP = 8
MERGE_TILE = 256
MOE_TILE = 512
SLOT_CHUNK = BF16_SUBLANES
LOCAL_SLOTS = 2 * MOE_TILE + N_EXPERTS * SLOT_CHUNK
CHUNKS_PER_TILE = LOCAL_SLOTS // SLOT_CHUNK
EXPERT_TILE = 512
CHUNKS_PER_EXPERT_TILE = EXPERT_TILE // SLOT_CHUNK
VMEM_LIMIT = 56 * 1024 * 1024

_NT = (((1,), (1,)), ((), ()))


def _cparams(sem):
    return pltpu.CompilerParams(dimension_semantics=sem, vmem_limit_bytes=VMEM_LIMIT)


def _round_up(x, m):
    return (x + m - 1) // m * m


def _seg_bounds(pos, bounds):
    s = jnp.int32(bounds[0][0])
    e = jnp.int32(bounds[0][1])
    for a, b in bounds[1:]:
        inside = pos >= a
        s = jnp.where(inside, jnp.int32(a), s)
        e = jnp.where(inside, jnp.int32(b), e)
    return s, e


def _bounds(segs, unit):
    out, pos = [], 0
    for n, length in segs:
        for _ in range(n):
            out.append((pos // unit, (pos + length) // unit))
            pos += length
    return tuple(out)


def _matmul_kernel(*refs, widths):
    xs, w_ref, o_ref = refs[:len(widths)], refs[len(widths)], refs[len(widths) + 1]
    acc, off = None, 0
    for x_ref, k in zip(xs, widths):
        part = jnp.dot(x_ref[...], w_ref[off:off + k, :], preferred_element_type=jnp.float32)
        acc = part if acc is None else acc + part
        off += k
    o_ref[...] = acc.astype(o_ref.dtype)


def matmul(xs, w, out_dtype, tm=1024, tn=512):
    t = xs[0].shape[0]
    k, n = w.shape
    widths = tuple(x.shape[1] for x in xs)
    assert sum(widths) == k
    tm, tn = min(tm, t), min(tn, n)
    assert t % tm == 0 and n % tn == 0
    in_specs = [pl.BlockSpec((tm, kw), lambda i, j: (i, 0)) for kw in widths]
    in_specs.append(pl.BlockSpec((k, tn), lambda i, j: (0, j)))
    return pl.pallas_call(
        functools.partial(_matmul_kernel, widths=widths),
        grid=(t // tm, n // tn),
        in_specs=in_specs,
        out_specs=pl.BlockSpec((tm, tn), lambda i, j: (i, j)),
        out_shape=jax.ShapeDtypeStruct((t, n), out_dtype),
        compiler_params=_cparams(("parallel", "arbitrary")),
        name="matmul",
    )(*xs, w)


def _matmul_dilated_kernel(x_ref, w_ref, o_ref, *rest, dils):
    z_refs, acc_ref = rest[:-1], rest[-1]
    tm, tn = o_ref.shape
    acc = jnp.dot(x_ref[...], w_ref[...], preferred_element_type=jnp.float32)
    o_ref[...] = acc.astype(o_ref.dtype)
    for s in range(tn // LANES):
        acc_ref[s] = acc[:, s * LANES:(s + 1) * LANES]
    for z_ref, d in zip(z_refs, dils):
        for rho in range(d):
            for s in range(tn // LANES):
                z_ref[rho, :, s * LANES:(s + 1) * LANES] = (
                    acc_ref[s, pl.ds(rho, tm // d, stride=d), :].astype(z_ref.dtype))


def matmul_dilated(x, w, dils, tm=1024, tn=512):
    t, k = x.shape
    n = w.shape[1]
    tm, tn = min(tm, t), min(tn, n)
    assert t % tm == 0 and n % tn == 0 and all(tm % (d * BF16_SUBLANES) == 0 for d in dils)
    out_specs = [pl.BlockSpec((tm, tn), lambda i, j: (i, j))]
    out_shape = [jax.ShapeDtypeStruct((t, n), jnp.bfloat16)]
    for d in dils:
        out_specs.append(pl.BlockSpec((d, tm // d, tn), lambda i, j: (0, i, j)))
        out_shape.append(jax.ShapeDtypeStruct((d, t // d, n), jnp.bfloat16))
    return pl.pallas_call(
        functools.partial(_matmul_dilated_kernel, dils=tuple(dils)),
        grid=(t // tm, n // tn),
        in_specs=[pl.BlockSpec((tm, k), lambda i, j: (i, 0)), pl.BlockSpec((k, tn), lambda i, j: (0, j))],
        out_specs=out_specs,
        out_shape=out_shape,
        scratch_shapes=[pltpu.VMEM((tn // LANES, tm, LANES), jnp.float32)],
        compiler_params=_cparams(("parallel", "arbitrary")),
        name="matmul_dilated",
    )(x, w)


def _resid_ln_kernel(x_ref, y_ref, g_ref, b_ref, o_ref, obf_ref):
    z = DEEPNORM_ALPHA * x_ref[...] + y_ref[...]
    mu = jnp.mean(z, axis=-1, keepdims=True)
    zc = z - mu
    var = jnp.mean(zc * zc, axis=-1, keepdims=True)
    out = zc * lax.rsqrt(var + LN_EPS) * g_ref[...] + b_ref[...]
    o_ref[...] = out
    obf_ref[...] = out.astype(jnp.bfloat16)


def resid_ln(x, y, g, b, rows=None, tm=128):
    t, d = x.shape
    start, count = (0, t) if rows is None else rows
    tm = min(tm, count)
    assert count % tm == 0 and start % tm == 0
    first = start // tm
    row_in = pl.BlockSpec((tm, d), lambda i: (first + i, 0))
    row_out = pl.BlockSpec((tm, d), lambda i: (i, 0))
    vec = pl.BlockSpec((1, d), lambda i: (0, 0))
    return pl.pallas_call(
        _resid_ln_kernel,
        grid=(count // tm,),
        in_specs=[row_in, row_in, vec, vec],
        out_specs=[row_out, row_out],
        out_shape=[jax.ShapeDtypeStruct((count, d), jnp.float32), jax.ShapeDtypeStruct((count, d), jnp.bfloat16)],
        compiler_params=_cparams(("parallel",)),
        name="resid_ln",
    )(x, y, g.reshape(1, d), b.reshape(1, d))


def _concat_cast_kernel(a_ref, b_ref, o_ref, obf_ref, *, a_blocks):
    i = pl.program_id(0)

    @pl.when(i < a_blocks)
    def _():
        o_ref[...] = a_ref[...]
        obf_ref[...] = a_ref[...].astype(jnp.bfloat16)

    @pl.when(i >= a_blocks)
    def _():
        o_ref[...] = b_ref[...]
        obf_ref[...] = b_ref[...].astype(jnp.bfloat16)


def concat_cast(a, b, tm=256):
    (ta, d), tb = a.shape, b.shape[0]
    tm = min(tm, ta, tb)
    assert ta % tm == 0 and tb % tm == 0
    a_blocks = ta // tm
    row = pl.BlockSpec((tm, d), lambda i: (i, 0))
    return pl.pallas_call(
        functools.partial(_concat_cast_kernel, a_blocks=a_blocks),
        grid=((ta + tb) // tm,),
        in_specs=[pl.BlockSpec((tm, d), lambda i: (jnp.minimum(i, a_blocks - 1), 0)),
                  pl.BlockSpec((tm, d), lambda i: (jnp.maximum(i - a_blocks, 0), 0))],
        out_specs=[row, row],
        out_shape=[jax.ShapeDtypeStruct((ta + tb, d), jnp.float32), jax.ShapeDtypeStruct((ta + tb, d), jnp.bfloat16)],
        compiler_params=_cparams(("arbitrary",)),
        name="concat_cast",
    )(a, b)


def _banded_kernel(*refs, half, nk, group, step, bounds, use_sink, emit_lse, row_axis, head_axis):
    refs = list(refs)
    slopes_ref = refs.pop(0)
    sink_ref = refs.pop(0) if use_sink else None
    q_ref, kp_ref, kc_ref, kn_ref, vp_ref, vc_ref, vn_ref = refs[:7]
    o_ref = refs[7]
    lse_ref = refs[8] if emit_lse else None

    ub = pl.program_id(row_axis)
    hgi = pl.program_id(head_axis)
    s0, e0 = _seg_bounds(ub * Q_BLOCK, bounds)
    has_prev = ub * Q_BLOCK > s0
    has_next = (ub + 1) * Q_BLOCK < e0
    win = Q_BLOCK + 2 * half
    row = lax.broadcasted_iota(jnp.int32, (Q_BLOCK, win), 0)
    col = lax.broadcasted_iota(jnp.int32, (Q_BLOCK, win), 1)
    dist = jnp.abs(col - half - row)
    valid = (dist <= half) & ((col >= half) | has_prev) & ((col < half + Q_BLOCK) | has_next)
    distf = (step * dist).astype(jnp.float32)
    scale2 = HEAD_DIM ** -0.5 * LOG2E
    lane = lax.broadcasted_iota(jnp.int32, (Q_BLOCK, LANES), 1)
    m_tile = jnp.zeros((Q_BLOCK, LANES), jnp.float32)
    l_tile = jnp.ones((Q_BLOCK, LANES), jnp.float32)

    for k in range(nk):
        ksl = slice(k * HEAD_DIM, (k + 1) * HEAD_DIM)
        kwin = jnp.concatenate([kp_ref[:, ksl], kc_ref[:, ksl], kn_ref[:, ksl]], axis=0)
        vwin = jnp.concatenate([vp_ref[:, ksl], vc_ref[:, ksl], vn_ref[:, ksl]], axis=0)
        for g in range(group):
            hidx = k * group + g
            hsl = slice(hidx * HEAD_DIM, (hidx + 1) * HEAD_DIM)
            head = hgi * (nk * group) + hidx
            s = lax.dot_general(q_ref[:, hsl], kwin, _NT, preferred_element_type=jnp.float32) * scale2
            s = jnp.where(valid, s - (slopes_ref[head] * LOG2E) * distf, NEG_INF)
            m = jnp.max(s, axis=-1, keepdims=True)
            if use_sink:
                sk = sink_ref[head] * LOG2E
                m = jnp.maximum(m, sk)
            p = jnp.exp2(s - m)
            l = jnp.sum(p, axis=-1, keepdims=True)
            if use_sink:
                l = l + jnp.exp2(sk - m)
            o = jnp.dot(p.astype(jnp.bfloat16), vwin, preferred_element_type=jnp.float32) / l
            if emit_lse:
                m_tile = jnp.where(lane == hidx, m, m_tile)
                l_tile = jnp.where(lane == hidx, l, l_tile)
            o_ref[:, hsl] = o.astype(o_ref.dtype)
    if emit_lse:
        lse_ref[...] = (m_tile + jnp.log2(l_tile)) * LN2


def attention_a(qkv, slopes, sink, segs):
    t = qkv.shape[0]
    nblk = t // Q_BLOCK
    bounds = _bounds(segs, 1)
    qw, kw = A_Q, A_KV
    k_col, v_col = A_Q // kw, (A_Q + A_KV) // kw
    prev = lambda b: jnp.maximum(b - 1, 0)
    nxt = lambda b: jnp.minimum(b + 1, nblk - 1)
    smem = pl.BlockSpec(memory_space=pltpu.SMEM)
    in_specs = [smem, smem,
                pl.BlockSpec((Q_BLOCK, qw), lambda b, h: (b, 0)),
                pl.BlockSpec((Q_BLOCK, kw), lambda b, h: (prev(b), k_col)),
                pl.BlockSpec((Q_BLOCK, kw), lambda b, h: (b, k_col)),
                pl.BlockSpec((Q_BLOCK, kw), lambda b, h: (nxt(b), k_col)),
                pl.BlockSpec((Q_BLOCK, kw), lambda b, h: (prev(b), v_col)),
                pl.BlockSpec((Q_BLOCK, kw), lambda b, h: (b, v_col)),
                pl.BlockSpec((Q_BLOCK, kw), lambda b, h: (nxt(b), v_col))]
    kern = functools.partial(_banded_kernel, half=A_HALF_WINDOW, nk=A_KV_HEADS, group=A_GROUP, step=1,
                             bounds=bounds, use_sink=True, emit_lse=False, row_axis=0, head_axis=1)
    return pl.pallas_call(
        kern,
        grid=(nblk, 1),
        in_specs=in_specs,
        out_specs=pl.BlockSpec((Q_BLOCK, qw), lambda b, h: (b, 0)),
        out_shape=jax.ShapeDtypeStruct((t, qw), jnp.bfloat16),
        compiler_params=_cparams(("parallel", "arbitrary")),
        name="attention_a",
    )(slopes, sink, *([qkv] * 7))


def attention_c_branch(z, slopes, dil, segs):
    rows = z.shape[1]
    nub = rows // Q_BLOCK
    half = C_HALF
    hw = C_HEAD_GROUP * HEAD_DIM
    bounds = _bounds(segs, dil)
    k_col0, v_col0 = C_W // hw, 2 * C_W // hw
    nhalf = rows // half
    prev = lambda u: jnp.maximum(2 * u - 1, 0)
    nxt = lambda u: jnp.minimum(2 * u + 2, nhalf - 1)
    smem = pl.BlockSpec(memory_space=pltpu.SMEM)
    in_specs = [smem,
                pl.BlockSpec((None, Q_BLOCK, hw), lambda r, u, h: (r, u, h)),
                pl.BlockSpec((None, half, hw), lambda r, u, h: (r, prev(u), k_col0 + h)),
                pl.BlockSpec((None, Q_BLOCK, hw), lambda r, u, h: (r, u, k_col0 + h)),
                pl.BlockSpec((None, half, hw), lambda r, u, h: (r, nxt(u), k_col0 + h)),
                pl.BlockSpec((None, half, hw), lambda r, u, h: (r, prev(u), v_col0 + h)),
                pl.BlockSpec((None, Q_BLOCK, hw), lambda r, u, h: (r, u, v_col0 + h)),
                pl.BlockSpec((None, half, hw), lambda r, u, h: (r, nxt(u), v_col0 + h))]
    kern = functools.partial(_banded_kernel, half=half, nk=C_HEAD_GROUP, group=1, step=dil, bounds=bounds,
                             use_sink=False, emit_lse=True, row_axis=1, head_axis=2)
    return pl.pallas_call(
        kern,
        grid=(dil, nub, C_GROUPS),
        in_specs=in_specs,
        out_specs=[pl.BlockSpec((None, Q_BLOCK, hw), lambda r, u, h: (r, u, h)),
                   pl.BlockSpec((None, Q_BLOCK, LANES), lambda r, u, h: (r, u, h))],
        out_shape=[jax.ShapeDtypeStruct((dil, rows, C_W), jnp.bfloat16),
                   jax.ShapeDtypeStruct((dil, rows, C_GROUPS * LANES), jnp.float32)],
        compiler_params=_cparams(("parallel", "parallel", "arbitrary")),
        name="attention_c_d%d" % dil,
    )(slopes, *([z] * 7))


def _plane_perm(dil, tm):
    p = np.zeros((tm, tm), np.float32)
    tok = np.arange(tm)
    p[tok, (tok % dil) * (tm // dil) + tok // dil] = 1.0
    return jnp.asarray(p, jnp.bfloat16)


def _split3(x):
    hi = x.astype(jnp.bfloat16)
    r = x - hi.astype(jnp.float32)
    mid = r.astype(jnp.bfloat16)
    lo = (r - mid.astype(jnp.float32)).astype(jnp.bfloat16)
    return hi, mid, lo


def _merge_kernel(*refs, dils):
    nb = len(dils)
    o_refs, l_refs = refs[:nb], refs[nb:2 * nb]
    p_refs = refs[2 * nb:2 * nb + nb - 1]
    out_ref = refs[-1]
    outs, lses = [o_refs[0][0].astype(jnp.float32)], [l_refs[0][0]]
    for b in range(1, nb):
        d = dils[b]
        perm = p_refs[b - 1][...]
        stacked = jnp.concatenate([o_refs[b][r] for r in range(d)], axis=0)
        outs.append(jnp.dot(perm, stacked, preferred_element_type=jnp.float32))
        lstack = jnp.concatenate([l_refs[b][r] for r in range(d)], axis=0)
        lses.append(sum(jnp.dot(perm, part, preferred_element_type=jnp.float32) for part in _split3(lstack)))
    for h in range(C_HEADS):
        hsl = slice(h * HEAD_DIM, (h + 1) * HEAD_DIM)
        col = (h // C_HEAD_GROUP) * LANES + h % C_HEAD_GROUP
        ls = [l[:, col:col + 1] for l in lses]
        m = functools.reduce(jnp.maximum, ls)
        es = [jnp.exp(l - m) for l in ls]
        tot = sum(es)
        acc = sum((e / tot) * o[:, hsl] for e, o in zip(es, outs))
        out_ref[:, hsl] = acc.astype(out_ref.dtype)


def merge_branches(os_, ls_, dils):
    t = os_[0].shape[0] * os_[0].shape[1]
    tm = min(MERGE_TILE, t)
    in_specs, args = [], []
    for arr, width in ((os_, C_W), (ls_, C_GROUPS * LANES)):
        for a, d in zip(arr, dils):
            in_specs.append(pl.BlockSpec((d, tm // d, width), lambda i: (0, i, 0)))
            args.append(a)
    for d in dils[1:]:
        in_specs.append(pl.BlockSpec((tm, tm), lambda i: (0, 0)))
        args.append(_plane_perm(d, tm))
    return pl.pallas_call(
        functools.partial(_merge_kernel, dils=tuple(dils)),
        grid=(t // tm,),
        in_specs=in_specs,
        out_specs=pl.BlockSpec((tm, C_W), lambda i: (i, 0)),
        out_shape=jax.ShapeDtypeStruct((t, C_W), jnp.bfloat16),
        compiler_params=_cparams(("parallel",)),
        name="merge_branches",
    )(*args)


def attention_c(zs, slopes, segs):
    dils = tuple(d for _, d in C_BRANCHES)
    res = [attention_c_branch(z, slopes, d, segs) for z, d in zip(zs, dils)]
    return merge_branches([r[0] for r in res], [r[1] for r in res], dils)


NA_PAIR = 2
NA_WIN_ROWS = NA_KH + NA_PAIR
NA_WIN_BLOCKS = NA_WIN_ROWS // NA_PAIR
NA_PAIR_CLASSES = ((0, 3, 1, 3), (0, 7, 0, 6), (0, 5, 0, 4), (2, 3, 2, 2), (2, 1, 2, 0))


def _natten_kernel(*refs):
    q_ref = refs[0]
    k_refs = refs[1:1 + NA_WIN_BLOCKS]
    v_refs = refs[1 + NA_WIN_BLOCKS:1 + 2 * NA_WIN_BLOCKS]
    bias_ref, o_ref = refs[1 + 2 * NA_WIN_BLOCKS:]
    scale2 = HEAD_DIM ** -0.5 * LOG2E
    for h in range(B_HEAD_GROUP):
        hsl = slice(h * HEAD_DIM, (h + 1) * HEAD_DIM)
        kwin = jnp.concatenate([r[:, hsl] for r in k_refs], axis=0)
        vwin = jnp.concatenate([r[:, hsl] for r in v_refs], axis=0)
        s = lax.dot_general(q_ref[:, hsl], kwin, _NT, preferred_element_type=jnp.float32) * scale2 + bias_ref[h]
        m = jnp.max(s, axis=-1, keepdims=True)
        p = jnp.exp2(s - m)
        l = jnp.sum(p, axis=-1, keepdims=True)
        o = jnp.dot(p.astype(jnp.bfloat16), vwin, preferred_element_type=jnp.float32) / l
        o_ref[:, hsl] = o.astype(o_ref.dtype)


def natten_bias_table(rpb):
    ncls = len(NA_PAIR_CLASSES)
    sel = np.zeros((ncls, NA_PAIR, NA_WIN_ROWS, 2 * NA_KH - 1), np.float32)
    row_ok = np.zeros((ncls, NA_PAIR, NA_WIN_ROWS), bool)
    for ci, cls in enumerate(NA_PAIR_CLASSES):
        for a in range(NA_PAIR):
            off, o = cls[2 * a], cls[2 * a + 1]
            for i in range(NA_KH):
                sel[ci, a, off + i, o + i] = 1.0
                row_ok[ci, a, off + i] = True
    c = jnp.arange(GRID_W)
    cs = jnp.clip(c - NA_KW // 2, 0, GRID_W - NA_KW)
    col_ok = (c[None, :] >= cs[:, None]) & (c[None, :] < cs[:, None] + NA_KW)
    dc = jnp.clip(c[None, :] - c[:, None] + (NA_KW - 1), 0, 2 * NA_KW - 2)
    pick = (dc[:, :, None] == jnp.arange(2 * NA_KW - 1)[None, None, :]).astype(jnp.float32)
    hi = lax.Precision.HIGHEST
    rows = jnp.einsum("capd,hdx->caphx", jnp.asarray(sel), rpb.astype(jnp.float32), precision=hi)
    tab = jnp.einsum("caphx,qkx->chaqpk", rows, pick, precision=hi)
    ok = jnp.asarray(row_ok)[:, None, :, None, :, None] & col_ok[None, None, None, :, None, :]
    tab = jnp.where(ok, tab * LOG2E, NEG_INF)
    return tab.reshape(ncls, B_HEADS, NA_PAIR * GRID_W, NA_WIN_ROWS * GRID_W)


def attention_b(qkv, bias_tab, segs):
    t = qkv.shape[0]
    nrows = t // GRID_W
    bounds = _bounds(segs, GRID_W)
    hw = B_HEAD_GROUP * HEAD_DIM
    ngroups = B_HEADS // B_HEAD_GROUP
    q_col0 = (A_Q + 2 * A_KV) // hw
    k_col0 = q_col0 + B_W // hw
    v_col0 = k_col0 + B_W // hw

    assert all(re - rs >= 2 * NA_KH and (re - rs) % NA_PAIR == 0 for rs, re in bounds)
    blk = NA_PAIR * GRID_W

    def win_block(pr):
        rs, re = _seg_bounds(pr * NA_PAIR, bounds)
        return jnp.clip(pr * NA_PAIR - NA_KH // 2, rs, re - NA_WIN_ROWS) // NA_PAIR

    def pair_class(pr):
        rs, re = _seg_bounds(pr * NA_PAIR, bounds)
        from_start, to_end = pr * NA_PAIR - rs, re - pr * NA_PAIR
        return jnp.where(from_start == 0, 1, jnp.where(from_start == 2, 2,
                         jnp.where(to_end == 4, 3, jnp.where(to_end == 2, 4, 0))))

    def kv_spec(col0, i):
        return pl.BlockSpec((blk, hw), lambda h, pr: (win_block(pr) + i, col0 + h))

    in_specs = [pl.BlockSpec((blk, hw), lambda h, pr: (pr, q_col0 + h))]
    in_specs += [kv_spec(k_col0, i) for i in range(NA_WIN_BLOCKS)]
    in_specs += [kv_spec(v_col0, i) for i in range(NA_WIN_BLOCKS)]
    in_specs.append(pl.BlockSpec((None, B_HEAD_GROUP, blk, NA_WIN_ROWS * GRID_W),
                                 lambda h, pr: (pair_class(pr), h, 0, 0)))
    return pl.pallas_call(
        _natten_kernel,
        grid=(ngroups, nrows // NA_PAIR),
        in_specs=in_specs,
        out_specs=pl.BlockSpec((blk, hw), lambda h, pr: (pr, h)),
        out_shape=jax.ShapeDtypeStruct((t, B_W), jnp.bfloat16),
        compiler_params=_cparams(("parallel", "arbitrary")),
        name="attention_b",
    )(*([qkv] * (1 + 2 * NA_WIN_BLOCKS)), bias_tab)


def _top2(vals):
    v1, i1 = vals[0], jnp.zeros(vals[0].shape, jnp.int32)
    for j in range(1, len(vals)):
        better = vals[j] > v1
        v1 = jnp.where(better, vals[j], v1)
        i1 = jnp.where(better, j, i1)
    v2, i2 = None, None
    for j in range(len(vals)):
        cand = jnp.where(i1 == j, -1.0, vals[j])
        if v2 is None:
            v2, i2 = cand, jnp.zeros(vals[0].shape, jnp.int32)
        else:
            better = cand > v2
            v2 = jnp.where(better, cand, v2)
            i2 = jnp.where(better, j, i2)
    return v1, i1, v2, i2


def _router_kernel(x_ref, w2_ref, whi_ref, b_ref, idx_ref, gate_ref):
    x = x_ref[...]
    x_hi = x.astype(jnp.bfloat16)
    x_lo = (x - x_hi.astype(jnp.float32)).astype(jnp.bfloat16)
    r1 = lax.dot_general(w2_ref[...], x_hi, _NT, preferred_element_type=jnp.float32)
    r2 = lax.dot_general(whi_ref[...], x_lo, _NT, preferred_element_type=jnp.float32)
    logits = r1[:N_EXPERTS] + r1[N_EXPERTS:] + r2 + b_ref[...]
    m = jnp.max(logits, axis=0, keepdims=True)
    ex = jnp.exp(logits - m)
    probs = ex / jnp.sum(ex, axis=0, keepdims=True)
    rows = [probs[e:e + 1, :] for e in range(N_EXPERTS)]
    tops = [_top2(rows[g * EXPERTS_PER_GROUP:(g + 1) * EXPERTS_PER_GROUP]) for g in range(N_GROUPS)]
    score = [tp[0] + tp[2] for tp in tops]
    best, gsel = score[0], jnp.zeros(score[0].shape, jnp.int32)
    for g in range(1, N_GROUPS):
        better = score[g] > best
        best = jnp.where(better, score[g], best)
        gsel = jnp.where(better, g, gsel)
    v1, i1, v2, i2 = tops[0]
    for g in range(1, N_GROUPS):
        pick = gsel == g
        v1 = jnp.where(pick, tops[g][0], v1)
        i1 = jnp.where(pick, tops[g][1], i1)
        v2 = jnp.where(pick, tops[g][2], v2)
        i2 = jnp.where(pick, tops[g][3], i2)
    tot = v1 + v2
    idx_ref[0:1, :] = gsel * EXPERTS_PER_GROUP + i1
    idx_ref[1:2, :] = gsel * EXPERTS_PER_GROUP + i2
    gate_ref[0:1, :] = v1 / tot
    gate_ref[1:2, :] = v2 / tot


def router(x, router_w, router_b, tm=512):
    t, d = x.shape
    tm = min(tm, t)
    wt = router_w.astype(jnp.float32).T
    w_hi = wt.astype(jnp.bfloat16)
    w_lo = (wt - w_hi.astype(jnp.float32)).astype(jnp.bfloat16)
    w2 = jnp.concatenate([w_hi, w_lo], axis=0)
    return pl.pallas_call(
        _router_kernel,
        grid=(t // tm,),
        in_specs=[pl.BlockSpec((tm, d), lambda i: (i, 0)),
                  pl.BlockSpec((2 * N_EXPERTS, d), lambda i: (0, 0)),
                  pl.BlockSpec((N_EXPERTS, d), lambda i: (0, 0)),
                  pl.BlockSpec((N_EXPERTS, 1), lambda i: (0, 0))],
        out_specs=[pl.BlockSpec((2, tm), lambda i: (0, i))] * 2,
        out_shape=[jax.ShapeDtypeStruct((2, t), jnp.int32), jax.ShapeDtypeStruct((2, t), jnp.float32)],
        compiler_params=_cparams(("parallel",)),
        name="router",
    )(x, w2, w_hi, router_b.astype(jnp.float32).reshape(N_EXPERTS, 1))


def dispatch_plan(idx):
    _, t = idx.shape
    nt = t // MOE_TILE
    n_chunks = _round_up(nt * CHUNKS_PER_TILE + N_EXPERTS * (CHUNKS_PER_EXPERT_TILE - 1), CHUNKS_PER_EXPERT_TILE)
    i32 = jnp.int32
    e_tile = idx.reshape(2, nt, MOE_TILE).transpose(1, 0, 2).reshape(nt, 2 * MOE_TILE)
    hit = e_tile[:, :, None] == jnp.arange(N_EXPERTS, dtype=i32)[None, None, :]
    onehot = hit.astype(i32)
    tri = (jnp.arange(2 * MOE_TILE)[:, None] >= jnp.arange(2 * MOE_TILE)[None, :]).astype(jnp.bfloat16)
    csum = jnp.einsum("ab,nbe->nae", tri, hit.astype(jnp.bfloat16), preferred_element_type=jnp.float32).astype(i32)
    rank = jnp.sum(csum * onehot, axis=-1) - 1
    cnt = csum[:, -1, :]
    nc = (cnt + SLOT_CHUNK - 1) // SLOT_CHUNK
    lo = jnp.cumsum(nc, axis=1) - nc
    local_pos = jnp.sum(onehot * (lo * SLOT_CHUNK)[:, None, :], axis=-1) + rank
    pos = local_pos.reshape(nt, 2, MOE_TILE).transpose(1, 0, 2).reshape(2, t).astype(i32)

    nc_t = nc.T
    e_chunks = nc_t.sum(axis=1)
    e_pad = _round_up(e_chunks, CHUNKS_PER_EXPERT_TILE)
    e_end = jnp.cumsum(e_pad)
    run_start = (e_end - e_pad)[:, None] + jnp.cumsum(nc_t, axis=1) - nc_t
    src0 = jnp.arange(nt, dtype=i32)[None, :] * CHUNKS_PER_TILE + lo.T
    rs_f, nc_f, src0_f = run_start.reshape(1, -1), nc_t.reshape(1, -1), src0.reshape(1, -1)
    d = jnp.arange(n_chunks, dtype=i32)[:, None]
    in_run = (d >= rs_f) & (d < rs_f + nc_f)
    fwd = jnp.sum(jnp.where(in_run, src0_f + d - rs_f, 0), axis=1).astype(i32)

    lc = jnp.arange(CHUNKS_PER_TILE, dtype=i32)[None, :, None]
    lo_b, nc_b = lo[:, None, :], nc[:, None, :]
    in_run_b = (lc >= lo_b) & (lc < lo_b + nc_b)
    back = jnp.sum(jnp.where(in_run_b, run_start.T[:, None, :] + lc - lo_b, 0), axis=-1)
    back = back.reshape(nt * CHUNKS_PER_TILE).astype(i32)

    n_tiles = n_chunks // CHUNKS_PER_EXPERT_TILE
    tile_first = jnp.arange(n_tiles, dtype=i32) * CHUNKS_PER_EXPERT_TILE
    tile_expert = jnp.minimum(jnp.sum((e_end[None, :] <= tile_first[:, None]).astype(i32), axis=1),
                              N_EXPERTS - 1).astype(i32)
    n_active = (e_end[-1] // CHUNKS_PER_EXPERT_TILE).astype(i32).reshape(1)
    return pos, fwd, back, tile_expert, n_active


def _permute_kernel(x_ref, pos_ref, gate_ref, xs_ref, gl_ref):
    tt, d = x_ref.shape
    slot = lax.broadcasted_iota(jnp.int32, (LOCAL_SLOTS, tt), 0)
    hit1 = slot == pos_ref[0:1, :]
    hit2 = slot == pos_ref[1:2, :]
    perm = jnp.where(hit1 | hit2, 1.0, 0.0).astype(jnp.bfloat16)
    cw = min(d, 512)
    for n in range(d // cw):
        sl = slice(n * cw, (n + 1) * cw)
        xs_ref[:, sl] = jnp.dot(perm, x_ref[:, sl], preferred_element_type=jnp.float32).astype(xs_ref.dtype)
    gates = jnp.where(hit1, gate_ref[0:1, :], 0.0) + jnp.where(hit2, gate_ref[1:2, :], 0.0)
    gl_ref[...] = jnp.broadcast_to(jnp.sum(gates, axis=1, keepdims=True), gl_ref.shape)


def permute_tokens(x_bf, pos, gate):
    t, d = x_bf.shape
    nt = t // MOE_TILE
    return pl.pallas_call(
        _permute_kernel,
        grid=(nt,),
        in_specs=[pl.BlockSpec((MOE_TILE, d), lambda i: (i, 0)),
                  pl.BlockSpec((2, MOE_TILE), lambda i: (0, i)),
                  pl.BlockSpec((2, MOE_TILE), lambda i: (0, i))],
        out_specs=[pl.BlockSpec((LOCAL_SLOTS, d), lambda i: (i, 0)),
                   pl.BlockSpec((LOCAL_SLOTS, LANES), lambda i: (i, 0))],
        out_shape=[jax.ShapeDtypeStruct((nt * LOCAL_SLOTS, d), jnp.bfloat16),
                   jax.ShapeDtypeStruct((nt * LOCAL_SLOTS, LANES), jnp.float32)],
        compiler_params=_cparams(("parallel",)),
        name="permute_tokens",
    )(x_bf, pos, gate)


def _unpermute_kernel(back_ref, ys_ref, gl_ref, p1_ref, p2_ref, y_ref, buf_ref, sems):
    i = pl.program_id(0)
    tt, d = y_ref.shape
    n = CHUNKS_PER_TILE
    buf_slot = i % 2

    @pl.when(i == 0)
    def _():
        _fetch_chunks(ys_ref, back_ref, 0, n, buf_ref, 0, sems)

    @pl.when(i + 1 < pl.num_programs(0))
    def _():
        _fetch_chunks(ys_ref, back_ref, (i + 1) * n, n, buf_ref, 1 - buf_slot, sems)

    slot = lax.broadcasted_iota(jnp.int32, (tt, LOCAL_SLOTS), 1)
    hit = (slot == p1_ref[:, 0:1]) | (slot == p2_ref[:, 0:1])
    perm = jnp.where(hit, 1.0, 0.0).astype(jnp.bfloat16)
    g = gl_ref[:, 0:1]
    _wait_chunks(ys_ref, n, buf_ref, buf_slot, sems)
    cw = min(d, 512)
    for c in range(d // cw):
        sl = slice(c * cw, (c + 1) * cw)
        scaled = (buf_ref[buf_slot, :, sl].astype(jnp.float32) * g).astype(jnp.bfloat16)
        y_ref[:, sl] = jnp.dot(perm, scaled, preferred_element_type=jnp.float32)


def unpermute_tokens(ys, back, gate_local, pos):
    _, t = pos.shape
    d = ys.shape[1]
    nt = t // MOE_TILE
    p1 = jnp.broadcast_to(pos[0][:, None], (t, LANES))
    p2 = jnp.broadcast_to(pos[1][:, None], (t, LANES))
    return pl.pallas_call(
        _unpermute_kernel,
        grid_spec=pltpu.PrefetchScalarGridSpec(
            num_scalar_prefetch=1,
            grid=(nt,),
            in_specs=[pl.BlockSpec(memory_space=pl.ANY),
                      pl.BlockSpec((LOCAL_SLOTS, LANES), lambda i, bk: (i, 0)),
                      pl.BlockSpec((MOE_TILE, LANES), lambda i, bk: (i, 0)),
                      pl.BlockSpec((MOE_TILE, LANES), lambda i, bk: (i, 0))],
            out_specs=pl.BlockSpec((MOE_TILE, d), lambda i, bk: (i, 0)),
            scratch_shapes=[pltpu.VMEM((2, LOCAL_SLOTS, d), ys.dtype), pltpu.SemaphoreType.DMA((2,))]),
        out_shape=jax.ShapeDtypeStruct((t, d), jnp.float32),
        compiler_params=_cparams(("arbitrary",)),
        name="unpermute_tokens",
    )(back, ys.reshape(ys.shape[0] // SLOT_CHUNK, SLOT_CHUNK, d), gate_local, p1, p2)


def _chunk_copy(src_ref, chunk, buf_ref, slot, g, sems):
    return pltpu.make_async_copy(src_ref.at[chunk], buf_ref.at[slot, pl.ds(g * SLOT_CHUNK, SLOT_CHUNK)],
                                 sems.at[slot])


def _fetch_chunks(src_ref, idx_ref, first, n, buf_ref, slot, sems):
    for g in range(n):
        _chunk_copy(src_ref, idx_ref[first + g], buf_ref, slot, g, sems).start()


def _wait_chunks(src_ref, n, buf_ref, slot, sems):
    for g in range(n):
        _chunk_copy(src_ref, 0, buf_ref, slot, g, sems).wait()


def _expert_up_kernel(te_ref, na_ref, fwd_ref, xs1_ref, w1_ref, w3_ref, h_ref, buf_ref, sems):
    j, i = pl.program_id(0), pl.program_id(1)
    na = na_ref[0]
    active = i < na

    @pl.when(active)
    def _():
        a = j * na + i
        slot = a % 2
        n = CHUNKS_PER_EXPERT_TILE

        @pl.when(a == 0)
        def _():
            _fetch_chunks(xs1_ref, fwd_ref, 0, n, buf_ref, 0, sems)

        @pl.when(a + 1 < pl.num_programs(0) * na)
        def _():
            nxt = jnp.where(i + 1 < na, i + 1, 0)
            _fetch_chunks(xs1_ref, fwd_ref, nxt * n, n, buf_ref, 1 - slot, sems)

        _wait_chunks(xs1_ref, n, buf_ref, slot, sems)
        x = buf_ref[slot]
        h1 = jnp.dot(x, w1_ref[...].astype(jnp.bfloat16), preferred_element_type=jnp.float32)
        h3 = jnp.dot(x, w3_ref[...].astype(jnp.bfloat16), preferred_element_type=jnp.float32)
        h_ref[...] = (jax.nn.silu(h1) * h3).astype(h_ref.dtype)

    @pl.when(jnp.logical_not(active))
    def _():
        h_ref[...] = jnp.zeros(h_ref.shape, h_ref.dtype)


def _expert_down_kernel(te_ref, na_ref, h_ref, w2_ref, y_ref):
    active = pl.program_id(1) < na_ref[0]

    @pl.when(active)
    def _():
        y_ref[...] = jnp.dot(h_ref[...], w2_ref[...].astype(jnp.bfloat16),
                             preferred_element_type=jnp.float32).astype(y_ref.dtype)

    @pl.when(jnp.logical_not(active))
    def _():
        y_ref[...] = jnp.zeros(y_ref.shape, y_ref.dtype)


def expert_ffn(xs1, fwd, w1, w3, w2, layer, tile_expert, n_active, tf=512, tn=2048):
    d = xs1.shape[1]
    f = w1.shape[3]
    tm = EXPERT_TILE
    tf, tn = min(tf, f), min(tn, d)
    n_tiles = fwd.shape[0] // CHUNKS_PER_EXPERT_TILE
    s = n_tiles * tm

    def act(i, na):
        return jnp.minimum(i, na[0] - 1)

    up = pl.pallas_call(
        _expert_up_kernel,
        grid_spec=pltpu.PrefetchScalarGridSpec(
            num_scalar_prefetch=3,
            grid=(f // tf, n_tiles),
            in_specs=[pl.BlockSpec(memory_space=pl.ANY),
                      pl.BlockSpec((None, None, d, tf), lambda j, i, te, na, fw: (layer, te[act(i, na)], 0, j)),
                      pl.BlockSpec((None, None, d, tf), lambda j, i, te, na, fw: (layer, te[act(i, na)], 0, j))],
            out_specs=pl.BlockSpec((tm, tf), lambda j, i, te, na, fw: (i, j)),
            scratch_shapes=[pltpu.VMEM((2, tm, d), xs1.dtype), pltpu.SemaphoreType.DMA((2,))]),
        out_shape=jax.ShapeDtypeStruct((s, f), jnp.bfloat16),
        compiler_params=_cparams(("arbitrary", "arbitrary")),
        name="expert_up",
    )
    h = up(tile_expert, n_active, fwd, xs1.reshape(xs1.shape[0] // SLOT_CHUNK, SLOT_CHUNK, d), w1, w3)
    down = pl.pallas_call(
        _expert_down_kernel,
        grid_spec=pltpu.PrefetchScalarGridSpec(
            num_scalar_prefetch=2,
            grid=(d // tn, n_tiles),
            in_specs=[pl.BlockSpec((tm, f), lambda j, i, te, na: (act(i, na), 0)),
                      pl.BlockSpec((None, None, f, tn), lambda j, i, te, na: (layer, te[act(i, na)], 0, j))],
            out_specs=pl.BlockSpec((tm, tn), lambda j, i, te, na: (i, j))),
        out_shape=jax.ShapeDtypeStruct((s, d), jnp.bfloat16),
        compiler_params=_cparams(("arbitrary", "arbitrary")),
        name="expert_down",
    )
    return down(tile_expert, n_active, h, w2)


def moe_layer(x, x_bf, router_w, router_b, w1, w3, w2, layer, g, b, out_rows=None):
    idx, gate = router(x, router_w, router_b)
    pos, fwd, back, tile_expert, n_active = dispatch_plan(idx)
    xs1, gate_local = permute_tokens(x_bf, pos, gate)
    ys = expert_ffn(xs1, fwd, w1, w3, w2, layer, tile_expert, n_active)
    y = unpermute_tokens(ys, back, gate_local, pos)
    if out_rows is None:
        return resid_ln(x, y, g, b)
    return [resid_ln(x, y, g, b, rows=r)[0] for r in out_rows]


def alibi_slopes(n):
    return jnp.exp2(-8.0 * jnp.arange(1, n + 1, dtype=jnp.float32) / n)


def trunk(x, x_bf, segs, out_rows, w_in_ab, w_out_ab, a_sink, b_rpb, w_in_c, w_out_c, router_w, router_b,
          moe_w1, moe_w3, moe_w2, ln_g, ln_b):
    bf = jnp.bfloat16
    slopes_a = alibi_slopes(A_HEADS)
    slopes_c = alibi_slopes(C_HEADS)
    dils = tuple(d for _, d in C_BRANCHES)
    for layer in range(DEPTH):
        i = layer // 2
        if layer % 2 == 0:
            qkv = matmul([x_bf], w_in_ab[i].astype(bf), bf)
            oa = attention_a(qkv, slopes_a, a_sink[i].astype(jnp.float32), segs)
            ob = attention_b(qkv, natten_bias_table(b_rpb[i]), segs)
            mix = matmul([oa, ob], w_out_ab[i].astype(bf), jnp.float32)
        else:
            zs = matmul_dilated(x_bf, w_in_c[i].astype(bf), dils[1:])
            zs = [zs[0].reshape((1,) + zs[0].shape)] + list(zs[1:])
            oc = attention_c(zs, slopes_c, segs)
            mix = matmul([oc], w_out_c[i].astype(bf), jnp.float32)
        x, x_bf = resid_ln(x, mix, ln_g[layer, 0], ln_b[layer, 0])
        last = layer == DEPTH - 1
        res = moe_layer(x, x_bf, router_w, router_b, moe_w1, moe_w3, moe_w2, layer,
                        ln_g[layer, 1], ln_b[layer, 1], out_rows=out_rows if last else None)
        if last:
            return res
        x, x_bf = res


def kernel(x_prompt, x_sample, w_in_ab, w_out_ab, a_sink, b_rpb, w_in_c, w_out_c, router_w, router_b,
           moe_w1, moe_w3, moe_w2, ln_g, ln_b):
    bp, sp, d = x_prompt.shape
    bs, ss, _ = x_sample.shape
    segs = ((bp, sp), (bs, ss))
    x, x_bf = concat_cast(x_prompt.reshape(bp * sp, d), x_sample.reshape(bs * ss, d))
    out_rows = ((0, bp * sp), (bp * sp, bs * ss))
    yp, ys = trunk(x, x_bf, segs, out_rows, w_in_ab, w_out_ab, a_sink, b_rpb, w_in_c, w_out_c, router_w, router_b,
                   moe_w1, moe_w3, moe_w2, ln_g, ln_b)
    return (yp.reshape(bp, sp, d), ys.reshape(bs, ss, d))
```

```python
import functools

import numpy as np
import jax
import jax.numpy as jnp
from jax import lax
from jax.experimental import pallas as pl
from jax.experimental.pallas import tpu as pltpu

HEAD_DIM = 128
GRID_W = 64
A_HEADS = 16
A_KV_HEADS = 4
A_GROUP = A_HEADS // A_KV_HEADS
A_HALF_WINDOW = 128
B_HEADS = 16
NA_KH = 8
NA_KW = 16
C_HEADS = 32
C_BRANCHES = ((128, 1), (512, 4), (2048, 16))
C_HALF = 64
N_EXPERTS = 16
N_GROUPS = 4
EXPERTS_PER_GROUP = N_EXPERTS // N_GROUPS
DEPTH = 4
DEEPNORM_ALPHA = (2 * DEPTH) ** 0.25
LN_EPS = 1e-5
NEG_INF = -1e30
LOG2E = 1.4426950408889634
LN2 = 0.6931471805599453

A_Q = A_HEADS * HEAD_DIM
A_KV = A_KV_HEADS * HEAD_DIM
B_W = B_HEADS * HEAD_DIM
C_W = C_HEADS * HEAD_DIM

LANES = 128
BF16_SUBLANES = 16
Q_BLOCK = 128
Q_SUB = 128
C_HEAD_GROUP = 16
C_GROUPS = C_HEADS // C_HEAD_GROUP
B_HEAD_GROUP = 8
MERGE_TILE = 256
MOE_TILE = 512
SLOT_CHUNK = BF16_SUBLANES
LOCAL_SLOTS = 2 * MOE_TILE + N_EXPERTS * SLOT_CHUNK
CHUNKS_PER_TILE = LOCAL_SLOTS // SLOT_CHUNK
EXPERT_TILE = 512
CHUNKS_PER_EXPERT_TILE = EXPERT_TILE // SLOT_CHUNK
VMEM_LIMIT = 56 * 1024 * 1024

_NT = (((1,), (1,)), ((), ()))


def _cparams(sem):
    return pltpu.CompilerParams(dimension_semantics=sem, vmem_limit_bytes=VMEM_LIMIT)


def _round_up(x, m):
    return (x + m - 1) // m * m


def _seg_bounds(pos, bounds):
    s = jnp.int32(bounds[0][0])
    e = jnp.int32(bounds[0][1])
    for a, b in bounds[1:]:
        inside = pos >= a
        s = jnp.where(inside, jnp.int32(a), s)
        e = jnp.where(inside, jnp.int32(b), e)
    return s, e


def _bounds(segs, unit):
    out, pos = [], 0
    for n, length in segs:
        for _ in range(n):
            out.append((pos // unit, (pos + length) // unit))
            pos += length
    return tuple(out)


def _matmul_kernel(*refs, widths):
    xs, w_ref, o_ref = refs[:len(widths)], refs[len(widths)], refs[len(widths) + 1]
    acc, off = None, 0
    for x_ref, k in zip(xs, widths):
        part = jnp.dot(x_ref[...], w_ref[off:off + k, :], preferred_element_type=jnp.float32)
        acc = part if acc is None else acc + part
        off += k
    o_ref[...] = acc.astype(o_ref.dtype)


def matmul(xs, w, out_dtype, tm=1024, tn=512):
    t = xs[0].shape[0]
    k, n = w.shape
    widths = tuple(x.shape[1] for x in xs)
    assert sum(widths) == k
    tm, tn = min(tm, t), min(tn, n)
    assert t % tm == 0 and n % tn == 0
    in_specs = [pl.BlockSpec((tm, kw), lambda i, j: (i, 0)) for kw in widths]
    in_specs.append(pl.BlockSpec((k, tn), lambda i, j: (0, j)))
    return pl.pallas_call(
        functools.partial(_matmul_kernel, widths=widths),
        grid=(t // tm, n // tn),
        in_specs=in_specs,
        out_specs=pl.BlockSpec((tm, tn), lambda i, j: (i, j)),
        out_shape=jax.ShapeDtypeStruct((t, n), out_dtype),
        compiler_params=_cparams(("parallel", "arbitrary")),
        name="matmul",
    )(*xs, w)


def _matmul_dilated_kernel(x_ref, w_ref, o_ref, *rest, dils):
    z_refs, acc_ref = rest[:-1], rest[-1]
    tm, tn = o_ref.shape
    acc = jnp.dot(x_ref[...], w_ref[...], preferred_element_type=jnp.float32)
    o_ref[...] = acc.astype(o_ref.dtype)
    for s in range(tn // LANES):
        acc_ref[s] = acc[:, s * LANES:(s + 1) * LANES]
    for z_ref, d in zip(z_refs, dils):
        for rho in range(d):
            for s in range(tn // LANES):
                z_ref[rho, :, s * LANES:(s + 1) * LANES] = (
                    acc_ref[s, pl.ds(rho, tm // d, stride=d), :].astype(z_ref.dtype))


def matmul_dilated(x, w, dils, tm=1024, tn=512):
    t, k = x.shape
    n = w.shape[1]
    tm, tn = min(tm, t), min(tn, n)
    assert t % tm == 0 and n % tn == 0 and all(tm % (d * BF16_SUBLANES) == 0 for d in dils)
    out_specs = [pl.BlockSpec((tm, tn), lambda i, j: (i, j))]
    out_shape = [jax.ShapeDtypeStruct((t, n), jnp.bfloat16)]
    for d in dils:
        out_specs.append(pl.BlockSpec((d, tm // d, tn), lambda i, j: (0, i, j)))
        out_shape.append(jax.ShapeDtypeStruct((d, t // d, n), jnp.bfloat16))
    return pl.pallas_call(
        functools.partial(_matmul_dilated_kernel, dils=tuple(dils)),
        grid=(t // tm, n // tn),
        in_specs=[pl.BlockSpec((tm, k), lambda i, j: (i, 0)), pl.BlockSpec((k, tn), lambda i, j: (0, j))],
        out_specs=out_specs,
        out_shape=out_shape,
        scratch_shapes=[pltpu.VMEM((tn // LANES, tm, LANES), jnp.float32)],
        compiler_params=_cparams(("parallel", "arbitrary")),
        name="matmul_dilated",
    )(x, w)


def _resid_ln_kernel(x_ref, y_ref, g_ref, b_ref, o_ref, obf_ref):
    out = _layer_norm_rows(DEEPNORM_ALPHA * x_ref[...] + y_ref[...], g_ref[...], b_ref[...])
    o_ref[...] = out
    obf_ref[...] = out.astype(jnp.bfloat16)


def resid_ln(x, y, g, b, rows=None, tm=128):
    t, d = x.shape
    start, count = (0, t) if rows is None else rows
    tm = min(tm, count)
    assert count % tm == 0 and start % tm == 0
    first = start // tm
    row_in = pl.BlockSpec((tm, d), lambda i: (first + i, 0))
    row_out = pl.BlockSpec((tm, d), lambda i: (i, 0))
    vec = pl.BlockSpec((1, d), lambda i: (0, 0))
    return pl.pallas_call(
        _resid_ln_kernel,
        grid=(count // tm,),
        in_specs=[row_in, row_in, vec, vec],
        out_specs=[row_out, row_out],
        out_shape=[jax.ShapeDtypeStruct((count, d), jnp.float32), jax.ShapeDtypeStruct((count, d), jnp.bfloat16)],
        compiler_params=_cparams(("parallel",)),
        name="resid_ln",
    )(x, y, g.reshape(1, d), b.reshape(1, d))


def _layer_norm_rows(z, g, b):
    mu = jnp.mean(z, axis=-1, keepdims=True)
    zc = z - mu
    var = jnp.mean(zc * zc, axis=-1, keepdims=True)
    return zc * lax.rsqrt(var + LN_EPS) * g + b


def _matmul_ln_kernel(*refs, widths, nj):
    nx = len(widths)
    xs, w_ref, r_ref, g_ref, b_ref, o_ref, obf_ref = refs[:nx], *refs[nx:]
    tn = w_ref.shape[1]
    j = pl.program_id(1)
    acc, off = None, 0
    for x_ref, k in zip(xs, widths):
        part = jnp.dot(x_ref[...], w_ref[off:off + k, :], preferred_element_type=jnp.float32)
        acc = part if acc is None else acc + part
        off += k
    for jj in range(nj):
        @pl.when(j == jj)
        def _():
            o_ref[:, jj * tn:(jj + 1) * tn] = acc

    @pl.when(j == nj - 1)
    def _():
        out = _layer_norm_rows(DEEPNORM_ALPHA * r_ref[...] + o_ref[...], g_ref[...], b_ref[...])
        o_ref[...] = out
        obf_ref[...] = out.astype(jnp.bfloat16)


def matmul_resid_ln(xs, w, resid, g, b, tm=256, tn=1024):
    t = xs[0].shape[0]
    k, n = w.shape
    widths = tuple(x.shape[1] for x in xs)
    assert sum(widths) == k
    tm, tn = min(tm, t), min(tn, n)
    assert t % tm == 0 and n % tn == 0
    row = pl.BlockSpec((tm, n), lambda i, j: (i, 0))
    vec = pl.BlockSpec((1, n), lambda i, j: (0, 0))
    in_specs = [pl.BlockSpec((tm, kw), lambda i, j: (i, 0)) for kw in widths]
    in_specs += [pl.BlockSpec((k, tn), lambda i, j: (0, j)), row, vec, vec]
    return pl.pallas_call(
        functools.partial(_matmul_ln_kernel, widths=widths, nj=n // tn),
        grid=(t // tm, n // tn),
        in_specs=in_specs,
        out_specs=[row, row],
        out_shape=[jax.ShapeDtypeStruct((t, n), jnp.float32), jax.ShapeDtypeStruct((t, n), jnp.bfloat16)],
        compiler_params=_cparams(("parallel", "arbitrary")),
        name="matmul_resid_ln",
    )(*xs, w, resid, g.reshape(1, n), b.reshape(1, n))


def _concat_cast_kernel(a_ref, b_ref, o_ref, obf_ref, *, a_blocks):
    i = pl.program_id(0)

    @pl.when(i < a_blocks)
    def _():
        o_ref[...] = a_ref[...]
        obf_ref[...] = a_ref[...].astype(jnp.bfloat16)

    @pl.when(i >= a_blocks)
    def _():
        o_ref[...] = b_ref[...]
        obf_ref[...] = b_ref[...].astype(jnp.bfloat16)


def concat_cast(a, b, tm=256):
    (ta, d), tb = a.shape, b.shape[0]
    tm = min(tm, ta, tb)
    assert ta % tm == 0 and tb % tm == 0
    a_blocks = ta // tm
    row = pl.BlockSpec((tm, d), lambda i: (i, 0))
    return pl.pallas_call(
        functools.partial(_concat_cast_kernel, a_blocks=a_blocks),
        grid=((ta + tb) // tm,),
        in_specs=[pl.BlockSpec((tm, d), lambda i: (jnp.minimum(i, a_blocks - 1), 0)),
                  pl.BlockSpec((tm, d), lambda i: (jnp.maximum(i - a_blocks, 0), 0))],
        out_specs=[row, row],
        out_shape=[jax.ShapeDtypeStruct((ta + tb, d), jnp.float32), jax.ShapeDtypeStruct((ta + tb, d), jnp.bfloat16)],
        compiler_params=_cparams(("arbitrary",)),
        name="concat_cast",
    )(a, b)


def _banded_kernel(*refs, half, nk, group, step, bounds, use_sink, emit_lse, row_axis, head_axis):
    refs = list(refs)
    slopes_ref = refs.pop(0)
    sink_ref = refs.pop(0) if use_sink else None
    q_ref, kp_ref, kc_ref, kn_ref, vp_ref, vc_ref, vn_ref = refs[:7]
    o_ref = refs[7]
    lse_ref = refs[8] if emit_lse else None

    ub = pl.program_id(row_axis)
    hgi = pl.program_id(head_axis)
    s0, e0 = _seg_bounds(ub * Q_BLOCK, bounds)
    has_prev = ub * Q_BLOCK > s0
    has_next = (ub + 1) * Q_BLOCK < e0
    nsub = Q_BLOCK // Q_SUB
    win = Q_BLOCK + 2 * half
    row = lax.broadcasted_iota(jnp.int32, (Q_SUB, win), 0)
    col = lax.broadcasted_iota(jnp.int32, (Q_SUB, win), 1)
    edge_ok = ((col >= half) | has_prev) & ((col < half + Q_BLOCK) | has_next)
    valids, distfs = [], []
    for sb in range(nsub):
        dist = jnp.abs(col - half - row - sb * Q_SUB)
        valids.append((dist <= half) & edge_ok)
        distfs.append((step * dist).astype(jnp.float32))
    scale2 = HEAD_DIM ** -0.5 * LOG2E
    lane = lax.broadcasted_iota(jnp.int32, (Q_SUB, LANES), 1)
    m_tiles = [jnp.zeros((Q_SUB, LANES), jnp.float32) for _ in range(nsub)]
    l_tiles = [jnp.ones((Q_SUB, LANES), jnp.float32) for _ in range(nsub)]

    for k in range(nk):
        ksl = slice(k * HEAD_DIM, (k + 1) * HEAD_DIM)
        kwin = jnp.concatenate([kp_ref[:, ksl], kc_ref[:, ksl], kn_ref[:, ksl]], axis=0)
        vwin = jnp.concatenate([vp_ref[:, ksl], vc_ref[:, ksl], vn_ref[:, ksl]], axis=0)
        for g in range(group):
            hidx = k * group + g
            hsl = slice(hidx * HEAD_DIM, (hidx + 1) * HEAD_DIM)
            head = hgi * (nk * group) + hidx
            slope2 = slopes_ref[head] * LOG2E
            for sb in range(nsub):
                rows = slice(sb * Q_SUB, (sb + 1) * Q_SUB)
                s = lax.dot_general(q_ref[rows, hsl], kwin, _NT, preferred_element_type=jnp.float32) * scale2
                s = jnp.where(valids[sb], s - slope2 * distfs[sb], NEG_INF)
                m = jnp.max(s, axis=-1, keepdims=True)
                if use_sink:
                    sk = sink_ref[head] * LOG2E
                    m = jnp.maximum(m, sk)
                p = jnp.exp2(s - m)
                l = jnp.sum(p, axis=-1, keepdims=True)
                if use_sink:
                    l = l + jnp.exp2(sk - m)
                o = jnp.dot(p.astype(jnp.bfloat16), vwin, preferred_element_type=jnp.float32) / l
                if emit_lse:
                    m_tiles[sb] = jnp.where(lane == hidx, m, m_tiles[sb])
                    l_tiles[sb] = jnp.where(lane == hidx, l, l_tiles[sb])
                o_ref[rows, hsl] = o.astype(o_ref.dtype)
    if emit_lse:
        for sb in range(nsub):
            lse_ref[sb * Q_SUB:(sb + 1) * Q_SUB, :] = (m_tiles[sb] + jnp.log2(l_tiles[sb])) * LN2


def attention_a(qkv, slopes, sink, segs):
    t = qkv.shape[0]
    nblk = t // Q_BLOCK
    bounds = _bounds(segs, 1)
    qw, kw = A_Q, A_KV
    k_col, v_col = A_Q // kw, (A_Q + A_KV) // kw
    prev = lambda b: jnp.maximum(b - 1, 0)
    nxt = lambda b: jnp.minimum(b + 1, nblk - 1)
    smem = pl.BlockSpec(memory_space=pltpu.SMEM)
    in_specs = [smem, smem,
                pl.BlockSpec((Q_BLOCK, qw), lambda b, h: (b, 0)),
                pl.BlockSpec((Q_BLOCK, kw), lambda b, h: (prev(b), k_col)),
                pl.BlockSpec((Q_BLOCK, kw), lambda b, h: (b, k_col)),
                pl.BlockSpec((Q_BLOCK, kw), lambda b, h: (nxt(b), k_col)),
                pl.BlockSpec((Q_BLOCK, kw), lambda b, h: (prev(b), v_col)),
                pl.BlockSpec((Q_BLOCK, kw), lambda b, h: (b, v_col)),
                pl.BlockSpec((Q_BLOCK, kw), lambda b, h: (nxt(b), v_col))]
    kern = functools.partial(_banded_kernel, half=A_HALF_WINDOW, nk=A_KV_HEADS, group=A_GROUP, step=1,
                             bounds=bounds, use_sink=True, emit_lse=False, row_axis=0, head_axis=1)
    return pl.pallas_call(
        kern,
        grid=(nblk, 1),
        in_specs=in_specs,
        out_specs=pl.BlockSpec((Q_BLOCK, qw), lambda b, h: (b, 0)),
        out_shape=jax.ShapeDtypeStruct((t, qw), jnp.bfloat16),
        compiler_params=_cparams(("parallel", "arbitrary")),
        name="attention_a",
    )(slopes, sink, *([qkv] * 7))


def attention_c_branch(z, slopes, dil, segs):
    rows = z.shape[1]
    nub = rows // Q_BLOCK
    half = C_HALF
    hw = C_HEAD_GROUP * HEAD_DIM
    bounds = _bounds(segs, dil)
    k_col0, v_col0 = C_W // hw, 2 * C_W // hw
    nhalf = rows // half
    prev = lambda u: jnp.maximum(2 * u - 1, 0)
    nxt = lambda u: jnp.minimum(2 * u + 2, nhalf - 1)
    smem = pl.BlockSpec(memory_space=pltpu.SMEM)
    in_specs = [smem,
                pl.BlockSpec((None, Q_BLOCK, hw), lambda r, u, h: (r, u, h)),
                pl.BlockSpec((None, half, hw), lambda r, u, h: (r, prev(u), k_col0 + h)),
                pl.BlockSpec((None, Q_BLOCK, hw), lambda r, u, h: (r, u, k_col0 + h)),
                pl.BlockSpec((None, half, hw), lambda r, u, h: (r, nxt(u), k_col0 + h)),
                pl.BlockSpec((None, half, hw), lambda r, u, h: (r, prev(u), v_col0 + h)),
                pl.BlockSpec((None, Q_BLOCK, hw), lambda r, u, h: (r, u, v_col0 + h)),
                pl.BlockSpec((None, half, hw), lambda r, u, h: (r, nxt(u), v_col0 + h))]
    kern = functools.partial(_banded_kernel, half=half, nk=C_HEAD_GROUP, group=1, step=dil, bounds=bounds,
                             use_sink=False, emit_lse=True, row_axis=1, head_axis=2)
    return pl.pallas_call(
        kern,
        grid=(dil, nub, C_GROUPS),
        in_specs=in_specs,
        out_specs=[pl.BlockSpec((None, Q_BLOCK, hw), lambda r, u, h: (r, u, h)),
                   pl.BlockSpec((None, Q_BLOCK, LANES), lambda r, u, h: (r, u, h))],
        out_shape=[jax.ShapeDtypeStruct((dil, rows, C_W), jnp.bfloat16),
                   jax.ShapeDtypeStruct((dil, rows, C_GROUPS * LANES), jnp.float32)],
        compiler_params=_cparams(("parallel", "parallel", "arbitrary")),
        name="attention_c_d%d" % dil,
    )(slopes, *([z] * 7))


def _plane_perm(dil, tm):
    p = np.zeros((tm, tm), np.float32)
    tok = np.arange(tm)
    p[tok, (tok % dil) * (tm // dil) + tok // dil] = 1.0
    return jnp.asarray(p, jnp.bfloat16)


def _split3(x):
    hi = x.astype(jnp.bfloat16)
    r = x - hi.astype(jnp.float32)
    mid = r.astype(jnp.bfloat16)
    lo = (r - mid.astype(jnp.float32)).astype(jnp.bfloat16)
    return hi, mid, lo


def _merge_kernel(*refs, dils):
    nb = len(dils)
    o_refs, l_refs = refs[:nb], refs[nb:2 * nb]
    p_refs = refs[2 * nb:2 * nb + nb - 1]
    out_ref = refs[-1]
    outs, lses = [o_refs[0][0].astype(jnp.float32)], [l_refs[0][0]]
    for b in range(1, nb):
        d = dils[b]
        perm = p_refs[b - 1][...]
        stacked = jnp.concatenate([o_refs[b][r] for r in range(d)], axis=0)
        outs.append(jnp.dot(perm, stacked, preferred_element_type=jnp.float32))
        lstack = jnp.concatenate([l_refs[b][r] for r in range(d)], axis=0)
        lses.append(sum(jnp.dot(perm, part, preferred_element_type=jnp.float32) for part in _split3(lstack)))
    for h in range(C_HEADS):
        hsl = slice(h * HEAD_DIM, (h + 1) * HEAD_DIM)
        col = (h // C_HEAD_GROUP) * LANES + h % C_HEAD_GROUP
        ls = [l[:, col:col + 1] for l in lses]
        m = functools.reduce(jnp.maximum, ls)
        es = [jnp.exp(l - m) for l in ls]
        tot = sum(es)
        acc = sum((e / tot) * o[:, hsl] for e, o in zip(es, outs))
        out_ref[:, hsl] = acc.astype(out_ref.dtype)


def merge_branches(os_, ls_, dils):
    t = os_[0].shape[0] * os_[0].shape[1]
    tm = min(MERGE_TILE, t)
    in_specs, args = [], []
    for arr, width in ((os_, C_W), (ls_, C_GROUPS * LANES)):
        for a, d in zip(arr, dils):
            in_specs.append(pl.BlockSpec((d, tm // d, width), lambda i: (0, i, 0)))
            args.append(a)
    for d in dils[1:]:
        in_specs.append(pl.BlockSpec((tm, tm), lambda i: (0, 0)))
        args.append(_plane_perm(d, tm))
    return pl.pallas_call(
        functools.partial(_merge_kernel, dils=tuple(dils)),
        grid=(t // tm,),
        in_specs=in_specs,
        out_specs=pl.BlockSpec((tm, C_W), lambda i: (i, 0)),
        out_shape=jax.ShapeDtypeStruct((t, C_W), jnp.bfloat16),
        compiler_params=_cparams(("parallel",)),
        name="merge_branches",
    )(*args)


def attention_c(zs, slopes, segs):
    dils = tuple(d for _, d in C_BRANCHES)
    res = [attention_c_branch(z, slopes, d, segs) for z, d in zip(zs, dils)]
    return merge_branches([r[0] for r in res], [r[1] for r in res], dils)


NA_PAIR = 2
NA_WIN_ROWS = NA_KH + NA_PAIR
NA_WIN_BLOCKS = NA_WIN_ROWS // NA_PAIR
NA_PAIR_CLASSES = ((0, 3, 1, 3), (0, 7, 0, 6), (0, 5, 0, 4), (2, 3, 2, 2), (2, 1, 2, 0))


def _natten_kernel(*refs):
    q_ref = refs[0]
    k_refs = refs[1:1 + NA_WIN_BLOCKS]
    v_refs = refs[1 + NA_WIN_BLOCKS:1 + 2 * NA_WIN_BLOCKS]
    bias_ref, o_ref = refs[1 + 2 * NA_WIN_BLOCKS:]
    scale2 = HEAD_DIM ** -0.5 * LOG2E
    for h in range(B_HEAD_GROUP):
        hsl = slice(h * HEAD_DIM, (h + 1) * HEAD_DIM)
        kwin = jnp.concatenate([r[:, hsl] for r in k_refs], axis=0)
        vwin = jnp.concatenate([r[:, hsl] for r in v_refs], axis=0)
        s = lax.dot_general(q_ref[:, hsl], kwin, _NT, preferred_element_type=jnp.float32) * scale2 + bias_ref[h]
        m = jnp.max(s, axis=-1, keepdims=True)
        p = jnp.exp2(s - m)
        l = jnp.sum(p, axis=-1, keepdims=True)
        o = jnp.dot(p.astype(jnp.bfloat16), vwin, preferred_element_type=jnp.float32) / l
        o_ref[:, hsl] = o.astype(o_ref.dtype)


def natten_bias_table(rpb):
    ncls = len(NA_PAIR_CLASSES)
    sel = np.zeros((ncls, NA_PAIR, NA_WIN_ROWS, 2 * NA_KH - 1), np.float32)
    row_ok = np.zeros((ncls, NA_PAIR, NA_WIN_ROWS), bool)
    for ci, cls in enumerate(NA_PAIR_CLASSES):
        for a in range(NA_PAIR):
            off, o = cls[2 * a], cls[2 * a + 1]
            for i in range(NA_KH):
                sel[ci, a, off + i, o + i] = 1.0
                row_ok[ci, a, off + i] = True
    c = jnp.arange(GRID_W)
    cs = jnp.clip(c - NA_KW // 2, 0, GRID_W - NA_KW)
    col_ok = (c[None, :] >= cs[:, None]) & (c[None, :] < cs[:, None] + NA_KW)
    dc = jnp.clip(c[None, :] - c[:, None] + (NA_KW - 1), 0, 2 * NA_KW - 2)
    pick = (dc[:, :, None] == jnp.arange(2 * NA_KW - 1)[None, None, :]).astype(jnp.float32)
    hi = lax.Precision.HIGHEST
    rows = jnp.einsum("capd,hdx->caphx", jnp.asarray(sel), rpb.astype(jnp.float32), precision=hi)
    tab = jnp.einsum("caphx,qkx->chaqpk", rows, pick, precision=hi)
    ok = jnp.asarray(row_ok)[:, None, :, None, :, None] & col_ok[None, None, None, :, None, :]
    tab = jnp.where(ok, tab * LOG2E, NEG_INF)
    return tab.reshape(ncls, B_HEADS, NA_PAIR * GRID_W, NA_WIN_ROWS * GRID_W)


def attention_b(qkv, bias_tab, segs):
    t = qkv.shape[0]
    nrows = t // GRID_W
    bounds = _bounds(segs, GRID_W)
    hw = B_HEAD_GROUP * HEAD_DIM
    ngroups = B_HEADS // B_HEAD_GROUP
    q_col0 = (A_Q + 2 * A_KV) // hw
    k_col0 = q_col0 + B_W // hw
    v_col0 = k_col0 + B_W // hw

    assert all(re - rs >= 2 * NA_KH and (re - rs) % NA_PAIR == 0 for rs, re in bounds)
    blk = NA_PAIR * GRID_W

    def win_block(pr):
        rs, re = _seg_bounds(pr * NA_PAIR, bounds)
        return jnp.clip(pr * NA_PAIR - NA_KH // 2, rs, re - NA_WIN_ROWS) // NA_PAIR

    def pair_class(pr):
        rs, re = _seg_bounds(pr * NA_PAIR, bounds)
        from_start, to_end = pr * NA_PAIR - rs, re - pr * NA_PAIR
        return jnp.where(from_start == 0, 1, jnp.where(from_start == 2, 2,
                         jnp.where(to_end == 4, 3, jnp.where(to_end == 2, 4, 0))))

    def kv_spec(col0, i):
        return pl.BlockSpec((blk, hw), lambda h, pr: (win_block(pr) + i, col0 + h))

    in_specs = [pl.BlockSpec((blk, hw), lambda h, pr: (pr, q_col0 + h))]
    in_specs += [kv_spec(k_col0, i) for i in range(NA_WIN_BLOCKS)]
    in_specs += [kv_spec(v_col0, i) for i in range(NA_WIN_BLOCKS)]
    in_specs.append(pl.BlockSpec((None, B_HEAD_GROUP, blk, NA_WIN_ROWS * GRID_W),
                                 lambda h, pr: (pair_class(pr), h, 0, 0)))
    return pl.pallas_call(
        _natten_kernel,
        grid=(ngroups, nrows // NA_PAIR),
        in_specs=in_specs,
        out_specs=pl.BlockSpec((blk, hw), lambda h, pr: (pr, h)),
        out_shape=jax.ShapeDtypeStruct((t, B_W), jnp.bfloat16),
        compiler_params=_cparams(("parallel", "arbitrary")),
        name="attention_b",
    )(*([qkv] * (1 + 2 * NA_WIN_BLOCKS)), bias_tab)


def _top2(vals):
    v1, i1 = vals[0], jnp.zeros(vals[0].shape, jnp.int32)
    for j in range(1, len(vals)):
        better = vals[j] > v1
        v1 = jnp.where(better, vals[j], v1)
        i1 = jnp.where(better, j, i1)
    v2, i2 = None, None
    for j in range(len(vals)):
        cand = jnp.where(i1 == j, -1.0, vals[j])
        if v2 is None:
            v2, i2 = cand, jnp.zeros(vals[0].shape, jnp.int32)
        else:
            better = cand > v2
            v2 = jnp.where(better, cand, v2)
            i2 = jnp.where(better, j, i2)
    return v1, i1, v2, i2


def _router_kernel(x_ref, w2_ref, whi_ref, b_ref, idx_ref, gate_ref):
    x = x_ref[...]
    x_hi = x.astype(jnp.bfloat16)
    x_lo = (x - x_hi.astype(jnp.float32)).astype(jnp.bfloat16)
    r1 = lax.dot_general(w2_ref[...], x_hi, _NT, preferred_element_type=jnp.float32)
    r2 = lax.dot_general(whi_ref[...], x_lo, _NT, preferred_element_type=jnp.float32)
    logits = r1[:N_EXPERTS] + r1[N_EXPERTS:] + r2 + b_ref[...]
    m = jnp.max(logits, axis=0, keepdims=True)
    ex = jnp.exp(logits - m)
    probs = ex / jnp.sum(ex, axis=0, keepdims=True)
    rows = [probs[e:e + 1, :] for e in range(N_EXPERTS)]
    tops = [_top2(rows[g * EXPERTS_PER_GROUP:(g + 1) * EXPERTS_PER_GROUP]) for g in range(N_GROUPS)]
    score = [tp[0] + tp[2] for tp in tops]
    best, gsel = score[0], jnp.zeros(score[0].shape, jnp.int32)
    for g in range(1, N_GROUPS):
        better = score[g] > best
        best = jnp.where(better, score[g], best)
        gsel = jnp.where(better, g, gsel)
    v1, i1, v2, i2 = tops[0]
    for g in range(1, N_GROUPS):
        pick = gsel == g
        v1 = jnp.where(pick, tops[g][0], v1)
        i1 = jnp.where(pick, tops[g][1], i1)
        v2 = jnp.where(pick, tops[g][2], v2)
        i2 = jnp.where(pick, tops[g][3], i2)
    tot = v1 + v2
    idx_ref[0:1, :] = gsel * EXPERTS_PER_GROUP + i1
    idx_ref[1:2, :] = gsel * EXPERTS_PER_GROUP + i2
    gate_ref[0:1, :] = v1 / tot
    gate_ref[1:2, :] = v2 / tot


def router(x, router_w, router_b, tm=512):
    t, d = x.shape
    tm = min(tm, t)
    wt = router_w.astype(jnp.float32).T
    w_hi = wt.astype(jnp.bfloat16)
    w_lo = (wt - w_hi.astype(jnp.float32)).astype(jnp.bfloat16)
    w2 = jnp.concatenate([w_hi, w_lo], axis=0)
    return pl.pallas_call(
        _router_kernel,
        grid=(t // tm,),
        in_specs=[pl.BlockSpec((tm, d), lambda i: (i, 0)),
                  pl.BlockSpec((2 * N_EXPERTS, d), lambda i: (0, 0)),
                  pl.BlockSpec((N_EXPERTS, d), lambda i: (0, 0)),
                  pl.BlockSpec((N_EXPERTS, 1), lambda i: (0, 0))],
        out_specs=[pl.BlockSpec((2, tm), lambda i: (0, i))] * 2,
        out_shape=[jax.ShapeDtypeStruct((2, t), jnp.int32), jax.ShapeDtypeStruct((2, t), jnp.float32)],
        compiler_params=_cparams(("parallel",)),
        name="router",
    )(x, w2, w_hi, router_b.astype(jnp.float32).reshape(N_EXPERTS, 1))


def dispatch_plan(idx):
    _, t = idx.shape
    nt = t // MOE_TILE
    n_chunks = _round_up(nt * CHUNKS_PER_TILE + N_EXPERTS * (CHUNKS_PER_EXPERT_TILE - 1), CHUNKS_PER_EXPERT_TILE)
    i32 = jnp.int32
    e_tile = idx.reshape(2, nt, MOE_TILE).transpose(1, 0, 2).reshape(nt, 2 * MOE_TILE)
    hit = e_tile[:, :, None] == jnp.arange(N_EXPERTS, dtype=i32)[None, None, :]
    onehot = hit.astype(i32)
    tri = (jnp.arange(2 * MOE_TILE)[:, None] >= jnp.arange(2 * MOE_TILE)[None, :]).astype(jnp.bfloat16)
    csum = jnp.einsum("ab,nbe->nae", tri, hit.astype(jnp.bfloat16), preferred_element_type=jnp.float32).astype(i32)
    rank = jnp.sum(csum * onehot, axis=-1) - 1
    cnt = csum[:, -1, :]
    nc = (cnt + SLOT_CHUNK - 1) // SLOT_CHUNK
    lo = jnp.cumsum(nc, axis=1) - nc
    local_pos = jnp.sum(onehot * (lo * SLOT_CHUNK)[:, None, :], axis=-1) + rank
    pos = local_pos.reshape(nt, 2, MOE_TILE).transpose(1, 0, 2).reshape(2, t).astype(i32)

    nc_t = nc.T
    e_chunks = nc_t.sum(axis=1)
    e_pad = _round_up(e_chunks, CHUNKS_PER_EXPERT_TILE)
    e_end = jnp.cumsum(e_pad)
    run_start = (e_end - e_pad)[:, None] + jnp.cumsum(nc_t, axis=1) - nc_t
    src0 = jnp.arange(nt, dtype=i32)[None, :] * CHUNKS_PER_TILE + lo.T
    rs_f, nc_f, src0_f = run_start.reshape(1, -1), nc_t.reshape(1, -1), src0.reshape(1, -1)
    d = jnp.arange(n_chunks, dtype=i32)[:, None]
    in_run = (d >= rs_f) & (d < rs_f + nc_f)
    fwd = jnp.sum(jnp.where(in_run, src0_f + d - rs_f, 0), axis=1).astype(i32)

    lc = jnp.arange(CHUNKS_PER_TILE, dtype=i32)[None, :, None]
    lo_b, nc_b = lo[:, None, :], nc[:, None, :]
    in_run_b = (lc >= lo_b) & (lc < lo_b + nc_b)
    back = jnp.sum(jnp.where(in_run_b, run_start.T[:, None, :] + lc - lo_b, 0), axis=-1)
    back = back.reshape(nt * CHUNKS_PER_TILE).astype(i32)

    n_tiles = n_chunks // CHUNKS_PER_EXPERT_TILE
    tile_first = jnp.arange(n_tiles, dtype=i32) * CHUNKS_PER_EXPERT_TILE
    tile_expert = jnp.minimum(jnp.sum((e_end[None, :] <= tile_first[:, None]).astype(i32), axis=1),
                              N_EXPERTS - 1).astype(i32)
    n_active = (e_end[-1] // CHUNKS_PER_EXPERT_TILE).astype(i32).reshape(1)
    return pos, fwd, back, tile_expert, n_active


def _permute_kernel(x_ref, pos_ref, gate_ref, xs_ref, gl_ref):
    tt, d = x_ref.shape
    slot = lax.broadcasted_iota(jnp.int32, (LOCAL_SLOTS, tt), 0)
    hit1 = slot == pos_ref[0:1, :]
    hit2 = slot == pos_ref[1:2, :]
    perm = jnp.where(hit1 | hit2, 1.0, 0.0).astype(jnp.bfloat16)
    cw = min(d, 512)
    for n in range(d // cw):
        sl = slice(n * cw, (n + 1) * cw)
        xs_ref[:, sl] = jnp.dot(perm, x_ref[:, sl], preferred_element_type=jnp.float32).astype(xs_ref.dtype)
    gates = jnp.where(hit1, gate_ref[0:1, :], 0.0) + jnp.where(hit2, gate_ref[1:2, :], 0.0)
    gl_ref[...] = jnp.broadcast_to(jnp.sum(gates, axis=1, keepdims=True), gl_ref.shape)


def permute_tokens(x_bf, pos, gate):
    t, d = x_bf.shape
    nt = t // MOE_TILE
    return pl.pallas_call(
        _permute_kernel,
        grid=(nt,),
        in_specs=[pl.BlockSpec((MOE_TILE, d), lambda i: (i, 0)),
                  pl.BlockSpec((2, MOE_TILE), lambda i: (0, i)),
                  pl.BlockSpec((2, MOE_TILE), lambda i: (0, i))],
        out_specs=[pl.BlockSpec((LOCAL_SLOTS, d), lambda i: (i, 0)),
                   pl.BlockSpec((LOCAL_SLOTS, LANES), lambda i: (i, 0))],
        out_shape=[jax.ShapeDtypeStruct((nt * LOCAL_SLOTS, d), jnp.bfloat16),
                   jax.ShapeDtypeStruct((nt * LOCAL_SLOTS, LANES), jnp.float32)],
        compiler_params=_cparams(("parallel",)),
        name="permute_tokens",
    )(x_bf, pos, gate)


def _unpermute_kernel(back_ref, ys_ref, gl_ref, p1_ref, p2_ref, y_ref, buf_ref, sems):
    i = pl.program_id(0)
    tt, d = y_ref.shape
    n = CHUNKS_PER_TILE
    buf_slot = i % 2

    @pl.when(i == 0)
    def _():
        _fetch_chunks(ys_ref, back_ref, 0, n, buf_ref, 0, sems)

    @pl.when(i + 1 < pl.num_programs(0))
    def _():
        _fetch_chunks(ys_ref, back_ref, (i + 1) * n, n, buf_ref, 1 - buf_slot, sems)

    slot = lax.broadcasted_iota(jnp.int32, (tt, LOCAL_SLOTS), 1)
    hit = (slot == p1_ref[:, 0:1]) | (slot == p2_ref[:, 0:1])
    perm = jnp.where(hit, 1.0, 0.0).astype(jnp.bfloat16)
    g = gl_ref[:, 0:1]
    _wait_chunks(ys_ref, n, buf_ref, buf_slot, sems)
    cw = min(d, 512)
    for c in range(d // cw):
        sl = slice(c * cw, (c + 1) * cw)
        scaled = (buf_ref[buf_slot, :, sl].astype(jnp.float32) * g).astype(jnp.bfloat16)
        y_ref[:, sl] = jnp.dot(perm, scaled, preferred_element_type=jnp.float32)


UNPERMUTE_SPLIT = 2


def _unpermute_ln_kernel(back_ref, ys_ref, gl_ref, p1_ref, p2_ref, x_ref, g_ref, b_ref, o_ref, obf_ref,
                         buf_ref, sems):
    i, part = pl.program_id(0), pl.program_id(1)
    rows, d = o_ref.shape
    n = CHUNKS_PER_TILE
    buf_slot = i % 2
    cw = min(d, 512)

    @pl.when((i == 0) & (part == 0))
    def _():
        _fetch_chunks(ys_ref, back_ref, 0, n, buf_ref, 0, sems)

    @pl.when((part == 0) & (i + 1 < pl.num_programs(0)))
    def _():
        _fetch_chunks(ys_ref, back_ref, (i + 1) * n, n, buf_ref, 1 - buf_slot, sems)

    @pl.when(part == 0)
    def _():
        _wait_chunks(ys_ref, n, buf_ref, buf_slot, sems)
        gate = gl_ref[:, 0:1]
        for c in range(d // cw):
            sl = slice(c * cw, (c + 1) * cw)
            buf_ref[buf_slot, :, sl] = (buf_ref[buf_slot, :, sl].astype(jnp.float32) * gate).astype(buf_ref.dtype)

    slot = lax.broadcasted_iota(jnp.int32, (rows, LOCAL_SLOTS), 1)
    hit = (slot == p1_ref[:, 0:1]) | (slot == p2_ref[:, 0:1])
    perm = jnp.where(hit, 1.0, 0.0).astype(jnp.bfloat16)
    for c in range(d // cw):
        sl = slice(c * cw, (c + 1) * cw)
        o_ref[:, sl] = jnp.dot(perm, buf_ref[buf_slot, :, sl], preferred_element_type=jnp.float32)
    out = _layer_norm_rows(DEEPNORM_ALPHA * x_ref[...] + o_ref[...], g_ref[...], b_ref[...])
    o_ref[...] = out
    obf_ref[...] = out.astype(jnp.bfloat16)


def unpermute_resid_ln(ys, back, gate_local, pos, x, g, b):
    t, d = x.shape
    nt = t // MOE_TILE
    rows = MOE_TILE // UNPERMUTE_SPLIT
    p1 = jnp.broadcast_to(pos[0][:, None], (t, LANES))
    p2 = jnp.broadcast_to(pos[1][:, None], (t, LANES))
    row = lambda width: pl.BlockSpec((rows, width), lambda i, p, bk: (i * UNPERMUTE_SPLIT + p, 0))
    vec = pl.BlockSpec((1, d), lambda i, p, bk: (0, 0))
    return pl.pallas_call(
        _unpermute_ln_kernel,
        grid_spec=pltpu.PrefetchScalarGridSpec(
            num_scalar_prefetch=1,
            grid=(nt, UNPERMUTE_SPLIT),
            in_specs=[pl.BlockSpec(memory_space=pl.ANY),
                      pl.BlockSpec((LOCAL_SLOTS, LANES), lambda i, p, bk: (i, 0)),
                      row(LANES), row(LANES), row(d), vec, vec],
            out_specs=[row(d), row(d)],
            scratch_shapes=[pltpu.VMEM((2, LOCAL_SLOTS, d), ys.dtype), pltpu.SemaphoreType.DMA((2,))]),
        out_shape=[jax.ShapeDtypeStruct((t, d), jnp.float32), jax.ShapeDtypeStruct((t, d), jnp.bfloat16)],
        compiler_params=_cparams(("arbitrary", "arbitrary")),
        name="unpermute_resid_ln",
    )(back, ys.reshape(ys.shape[0] // SLOT_CHUNK, SLOT_CHUNK, d), gate_local, p1, p2, x,
      g.reshape(1, d), b.reshape(1, d))


def unpermute_tokens(ys, back, gate_local, pos):
    _, t = pos.shape
    d = ys.shape[1]
    nt = t // MOE_TILE
    p1 = jnp.broadcast_to(pos[0][:, None], (t, LANES))
    p2 = jnp.broadcast_to(pos[1][:, None], (t, LANES))
    return pl.pallas_call(
        _unpermute_kernel,
        grid_spec=pltpu.PrefetchScalarGridSpec(
            num_scalar_prefetch=1,
            grid=(nt,),
            in_specs=[pl.BlockSpec(memory_space=pl.ANY),
                      pl.BlockSpec((LOCAL_SLOTS, LANES), lambda i, bk: (i, 0)),
                      pl.BlockSpec((MOE_TILE, LANES), lambda i, bk: (i, 0)),
                      pl.BlockSpec((MOE_TILE, LANES), lambda i, bk: (i, 0))],
            out_specs=pl.BlockSpec((MOE_TILE, d), lambda i, bk: (i, 0)),
            scratch_shapes=[pltpu.VMEM((2, LOCAL_SLOTS, d), ys.dtype), pltpu.SemaphoreType.DMA((2,))]),
        out_shape=jax.ShapeDtypeStruct((t, d), jnp.float32),
        compiler_params=_cparams(("arbitrary",)),
        name="unpermute_tokens",
    )(back, ys.reshape(ys.shape[0] // SLOT_CHUNK, SLOT_CHUNK, d), gate_local, p1, p2)


def _chunk_copy(src_ref, chunk, buf_ref, slot, g, sems):
    return pltpu.make_async_copy(src_ref.at[chunk], buf_ref.at[slot, pl.ds(g * SLOT_CHUNK, SLOT_CHUNK)],
                                 sems.at[slot])


def _fetch_chunks(src_ref, idx_ref, first, n, buf_ref, slot, sems):
    for g in range(n):
        _chunk_copy(src_ref, idx_ref[first + g], buf_ref, slot, g, sems).start()


def _wait_chunks(src_ref, n, buf_ref, slot, sems):
    for g in range(n):
        _chunk_copy(src_ref, 0, buf_ref, slot, g, sems).wait()


def _expert_up_kernel(te_ref, na_ref, fwd_ref, xs1_ref, w1_ref, w3_ref, h_ref, buf_ref, sems):
    j, i = pl.program_id(0), pl.program_id(1)
    na = na_ref[0]
    active = i < na

    @pl.when(active)
    def _():
        a = j * na + i
        slot = a % 2
        n = CHUNKS_PER_EXPERT_TILE

        @pl.when(a == 0)
        def _():
            _fetch_chunks(xs1_ref, fwd_ref, 0, n, buf_ref, 0, sems)

        @pl.when(a + 1 < pl.num_programs(0) * na)
        def _():
            nxt = jnp.where(i + 1 < na, i + 1, 0)
            _fetch_chunks(xs1_ref, fwd_ref, nxt * n, n, buf_ref, 1 - slot, sems)

        _wait_chunks(xs1_ref, n, buf_ref, slot, sems)
        x = buf_ref[slot]
        h1 = jnp.dot(x, w1_ref[...].astype(jnp.bfloat16), preferred_element_type=jnp.float32)
        h3 = jnp.dot(x, w3_ref[...].astype(jnp.bfloat16), preferred_element_type=jnp.float32)
        h_ref[...] = (jax.nn.silu(h1) * h3).astype(h_ref.dtype)

    @pl.when(jnp.logical_not(active))
    def _():
        h_ref[...] = jnp.zeros(h_ref.shape, h_ref.dtype)


def _expert_down_kernel(te_ref, na_ref, h_ref, w2_ref, y_ref):
    active = pl.program_id(1) < na_ref[0]

    @pl.when(active)
    def _():
        y_ref[...] = jnp.dot(h_ref[...], w2_ref[...].astype(jnp.bfloat16),
                             preferred_element_type=jnp.float32).astype(y_ref.dtype)

    @pl.when(jnp.logical_not(active))
    def _():
        y_ref[...] = jnp.zeros(y_ref.shape, y_ref.dtype)


def expert_ffn(xs1, fwd, w1, w3, w2, layer, tile_expert, n_active, tf=512, tn=2048):
    d = xs1.shape[1]
    f = w1.shape[3]
    tm = EXPERT_TILE
    tf, tn = min(tf, f), min(tn, d)
    n_tiles = fwd.shape[0] // CHUNKS_PER_EXPERT_TILE
    s = n_tiles * tm

    def act(i, na):
        return jnp.minimum(i, na[0] - 1)

    up = pl.pallas_call(
        _expert_up_kernel,
        grid_spec=pltpu.PrefetchScalarGridSpec(
            num_scalar_prefetch=3,
            grid=(f // tf, n_tiles),
            in_specs=[pl.BlockSpec(memory_space=pl.ANY),
                      pl.BlockSpec((None, None, d, tf), lambda j, i, te, na, fw: (layer, te[act(i, na)], 0, j)),
                      pl.BlockSpec((None, None, d, tf), lambda j, i, te, na, fw: (layer, te[act(i, na)], 0, j))],
            out_specs=pl.BlockSpec((tm, tf), lambda j, i, te, na, fw: (i, j)),
            scratch_shapes=[pltpu.VMEM((2, tm, d), xs1.dtype), pltpu.SemaphoreType.DMA((2,))]),
        out_shape=jax.ShapeDtypeStruct((s, f), jnp.bfloat16),
        compiler_params=_cparams(("arbitrary", "arbitrary")),
        name="expert_up",
    )
    h = up(tile_expert, n_active, fwd, xs1.reshape(xs1.shape[0] // SLOT_CHUNK, SLOT_CHUNK, d), w1, w3)
    down = pl.pallas_call(
        _expert_down_kernel,
        grid_spec=pltpu.PrefetchScalarGridSpec(
            num_scalar_prefetch=2,
            grid=(d // tn, n_tiles),
            in_specs=[pl.BlockSpec((tm, f), lambda j, i, te, na: (act(i, na), 0)),
                      pl.BlockSpec((None, None, f, tn), lambda j, i, te, na: (layer, te[act(i, na)], 0, j))],
            out_specs=pl.BlockSpec((tm, tn), lambda j, i, te, na: (i, j))),
        out_shape=jax.ShapeDtypeStruct((s, d), jnp.bfloat16),
        compiler_params=_cparams(("arbitrary", "arbitrary")),
        name="expert_down",
    )
    return down(tile_expert, n_active, h, w2)


def moe_layer(x, x_bf, router_w, router_b, w1, w3, w2, layer, g, b, out_rows=None):
    idx, gate = router(x, router_w, router_b)
    pos, fwd, back, tile_expert, n_active = dispatch_plan(idx)
    xs1, gate_local = permute_tokens(x_bf, pos, gate)
    ys = expert_ffn(xs1, fwd, w1, w3, w2, layer, tile_expert, n_active)
    if out_rows is None:
        return unpermute_resid_ln(ys, back, gate_local, pos, x, g, b)
    y = unpermute_tokens(ys, back, gate_local, pos)
    return [resid_ln(x, y, g, b, rows=r)[0] for r in out_rows]


def alibi_slopes(n):
    return jnp.exp2(-8.0 * jnp.arange(1, n + 1, dtype=jnp.float32) / n)


def trunk(x, x_bf, segs, out_rows, w_in_ab, w_out_ab, a_sink, b_rpb, w_in_c, w_out_c, router_w, router_b,
          moe_w1, moe_w3, moe_w2, ln_g, ln_b):
    bf = jnp.bfloat16
    slopes_a = alibi_slopes(A_HEADS)
    slopes_c = alibi_slopes(C_HEADS)
    dils = tuple(d for _, d in C_BRANCHES)
    for layer in range(DEPTH):
        i = layer // 2
        if layer % 2 == 0:
            qkv = matmul([x_bf], w_in_ab[i].astype(bf), bf)
            oa = attention_a(qkv, slopes_a, a_sink[i].astype(jnp.float32), segs)
            ob = attention_b(qkv, natten_bias_table(b_rpb[i]), segs)
            mixed, w_out = [oa, ob], w_out_ab[i]
        else:
            zs = matmul_dilated(x_bf, w_in_c[i].astype(bf), dils[1:])
            zs = [zs[0].reshape((1,) + zs[0].shape)] + list(zs[1:])
            mixed, w_out = [attention_c(zs, slopes_c, segs)], w_out_c[i]
        x, x_bf = matmul_resid_ln(mixed, w_out.astype(bf), x, ln_g[layer, 0], ln_b[layer, 0])
        last = layer == DEPTH - 1
        res = moe_layer(x, x_bf, router_w, router_b, moe_w1, moe_w3, moe_w2, layer,
                        ln_g[layer, 1], ln_b[layer, 1], out_rows=out_rows if last else None)
        if last:
            return res
        x, x_bf = res


def kernel(x_prompt, x_sample, w_in_ab, w_out_ab, a_sink, b_rpb, w_in_c, w_out_c, router_w, router_b,
           moe_w1, moe_w3, moe_w2, ln_g, ln_b):
    bp, sp, d = x_prompt.shape
    bs, ss, _ = x_sample.shape
    segs = ((bp, sp), (bs, ss))
    x, x_bf = concat_cast(x_prompt.reshape(bp * sp, d), x_sample.reshape(bs * ss, d))
    out_rows = ((0, bp * sp), (bp * sp, bs * ss))
    yp, ys = trunk(x, x_bf, segs, out_rows, w_in_ab, w_out_ab, a_sink, b_rpb, w_in_c, w_out_c, router_w, router_b,
                   moe_w1, moe_w3, moe_w2, ln_g, ln_b)
    return (yp.reshape(bp, sp, d), ys.reshape(bs, ss, d))
```

```python
import functools

import numpy as np
import jax
import jax.numpy as jnp
from jax import lax
from jax.experimental import pallas as pl
from jax.experimental.pallas import tpu as pltpu

HEAD_DIM = 128
GRID_W = 64
A_HEADS = 16
A_KV_HEADS = 4
A_GROUP = A_HEADS // A_KV_HEADS
A_HALF_WINDOW = 128
B_HEADS = 16
NA_KH = 8
NA_KW = 16
C_HEADS = 32
C_BRANCHES = ((128, 1), (512, 4), (2048, 16))
C_HALF = 64
N_EXPERTS = 16
N_GROUPS = 4
EXPERTS_PER_GROUP = N_EXPERTS // N_GROUPS
DEPTH = 4
DEEPNORM_ALPHA = (2 * DEPTH) ** 0.25
LN_EPS = 1e-5
NEG_INF = -1e30
LOG2E = 1.4426950408889634
LN2 = 0.6931471805599453

A_Q = A_HEADS * HEAD_DIM
A_KV = A_KV_HEADS * HEAD_DIM
B_W = B_HEADS * HEAD_DIM
C_W = C_HEADS * HEAD_DIM

LANES = 128
BF16_SUBLANES = 16
Q_BLOCK = 128
Q_SUB = 128
C_HEAD_GROUP = 16
C_GROUPS = C_HEADS // C_HEAD_GROUP
B_HEAD_GROUP = 8
MERGE_TILE = 256
MOE_TILE = 512
SLOT_CHUNK = BF16_SUBLANES
LOCAL_SLOTS = 2 * MOE_TILE + N_EXPERTS * SLOT_CHUNK
CHUNKS_PER_TILE = LOCAL_SLOTS // SLOT_CHUNK
EXPERT_TILE = 512
CHUNKS_PER_EXPERT_TILE = EXPERT_TILE // SLOT_CHUNK
VMEM_LIMIT = 56 * 1024 * 1024

_NT = (((1,), (1,)), ((), ()))


def _cparams(sem):
    return pltpu.CompilerParams(dimension_semantics=sem, vmem_limit_bytes=VMEM_LIMIT)


def _round_up(x, m):
    return (x + m - 1) // m * m


def _seg_bounds(pos, bounds):
    s = jnp.int32(bounds[0][0])
    e = jnp.int32(bounds[0][1])
    for a, b in bounds[1:]:
        inside = pos >= a
        s = jnp.where(inside, jnp.int32(a), s)
        e = jnp.where(inside, jnp.int32(b), e)
    return s, e


def _bounds(segs, unit):
    out, pos = [], 0
    for n, length in segs:
        for _ in range(n):
            out.append((pos // unit, (pos + length) // unit))
            pos += length
    return tuple(out)


def _matmul_kernel(*refs, widths):
    xs, w_ref, o_ref = refs[:len(widths)], refs[len(widths)], refs[len(widths) + 1]
    acc, off = None, 0
    for x_ref, k in zip(xs, widths):
        part = jnp.dot(x_ref[...], w_ref[off:off + k, :], preferred_element_type=jnp.float32)
        acc = part if acc is None else acc + part
        off += k
    o_ref[...] = acc.astype(o_ref.dtype)


def matmul(xs, w, out_dtype, tm=1024, tn=1024):
    t = xs[0].shape[0]
    k, n = w.shape
    widths = tuple(x.shape[1] for x in xs)
    assert sum(widths) == k
    tm, tn = min(tm, t), min(tn, n)
    assert t % tm == 0 and n % tn == 0
    in_specs = [pl.BlockSpec((tm, kw), lambda i, j: (i, 0)) for kw in widths]
    in_specs.append(pl.BlockSpec((k, tn), lambda i, j: (0, j)))
    return pl.pallas_call(
        functools.partial(_matmul_kernel, widths=widths),
        grid=(t // tm, n // tn),
        in_specs=in_specs,
        out_specs=pl.BlockSpec((tm, tn), lambda i, j: (i, j)),
        out_shape=jax.ShapeDtypeStruct((t, n), out_dtype),
        compiler_params=_cparams(("parallel", "arbitrary")),
        name="matmul",
    )(*xs, w)


def _matmul_dilated_kernel(x_ref, w_ref, o_ref, *rest, dils):
    z_refs, acc_ref = rest[:-1], rest[-1]
    tm, tn = o_ref.shape
    acc = jnp.dot(x_ref[...], w_ref[...], preferred_element_type=jnp.float32)
    o_ref[...] = acc.astype(o_ref.dtype)
    for s in range(tn // LANES):
        acc_ref[s] = acc[:, s * LANES:(s + 1) * LANES]
    for z_ref, d in zip(z_refs, dils):
        for rho in range(d):
            for s in range(tn // LANES):
                z_ref[rho, :, s * LANES:(s + 1) * LANES] = (
                    acc_ref[s, pl.ds(rho, tm // d, stride=d), :].astype(z_ref.dtype))


def matmul_dilated(x, w, dils, tm=1024, tn=512):
    t, k = x.shape
    n = w.shape[1]
    tm, tn = min(tm, t), min(tn, n)
    assert t % tm == 0 and n % tn == 0 and all(tm % (d * BF16_SUBLANES) == 0 for d in dils)
    out_specs = [pl.BlockSpec((tm, tn), lambda i, j: (i, j))]
    out_shape = [jax.ShapeDtypeStruct((t, n), jnp.bfloat16)]
    for d in dils:
        out_specs.append(pl.BlockSpec((d, tm // d, tn), lambda i, j: (0, i, j)))
        out_shape.append(jax.ShapeDtypeStruct((d, t // d, n), jnp.bfloat16))
    return pl.pallas_call(
        functools.partial(_matmul_dilated_kernel, dils=tuple(dils)),
        grid=(t // tm, n // tn),
        in_specs=[pl.BlockSpec((tm, k), lambda i, j: (i, 0)), pl.BlockSpec((k, tn), lambda i, j: (0, j))],
        out_specs=out_specs,
        out_shape=out_shape,
        scratch_shapes=[pltpu.VMEM((tn // LANES, tm, LANES), jnp.float32)],
        compiler_params=_cparams(("parallel", "arbitrary")),
        name="matmul_dilated",
    )(x, w)


def _resid_ln_kernel(x_ref, y_ref, g_ref, b_ref, o_ref, obf_ref):
    out = _layer_norm_rows(DEEPNORM_ALPHA * x_ref[...] + y_ref[...], g_ref[...], b_ref[...])
    o_ref[...] = out
    obf_ref[...] = out.astype(jnp.bfloat16)


def resid_ln(x, y, g, b, rows=None, tm=128):
    t, d = x.shape
    start, count = (0, t) if rows is None else rows
    tm = min(tm, count)
    assert count % tm == 0 and start % tm == 0
    first = start // tm
    row_in = pl.BlockSpec((tm, d), lambda i: (first + i, 0))
    row_out = pl.BlockSpec((tm, d), lambda i: (i, 0))
    vec = pl.BlockSpec((1, d), lambda i: (0, 0))
    return pl.pallas_call(
        _resid_ln_kernel,
        grid=(count // tm,),
        in_specs=[row_in, row_in, vec, vec],
        out_specs=[row_out, row_out],
        out_shape=[jax.ShapeDtypeStruct((count, d), jnp.float32), jax.ShapeDtypeStruct((count, d), jnp.bfloat16)],
        compiler_params=_cparams(("parallel",)),
        name="resid_ln",
    )(x, y, g.reshape(1, d), b.reshape(1, d))


def _layer_norm_rows(z, g, b):
    mu = jnp.mean(z, axis=-1, keepdims=True)
    zc = z - mu
    var = jnp.mean(zc * zc, axis=-1, keepdims=True)
    return zc * lax.rsqrt(var + LN_EPS) * g + b


def _concat_cast_kernel(a_ref, b_ref, o_ref, obf_ref, *, a_blocks):
    i = pl.program_id(0)

    @pl.when(i < a_blocks)
    def _():
        o_ref[...] = a_ref[...]
        obf_ref[...] = a_ref[...].astype(jnp.bfloat16)

    @pl.when(i >= a_blocks)
    def _():
        o_ref[...] = b_ref[...]
        obf_ref[...] = b_ref[...].astype(jnp.bfloat16)


def concat_cast(a, b, tm=256):
    (ta, d), tb = a.shape, b.shape[0]
    tm = min(tm, ta, tb)
    assert ta % tm == 0 and tb % tm == 0
    a_blocks = ta // tm
    row = pl.BlockSpec((tm, d), lambda i: (i, 0))
    return pl.pallas_call(
        functools.partial(_concat_cast_kernel, a_blocks=a_blocks),
        grid=((ta + tb) // tm,),
        in_specs=[pl.BlockSpec((tm, d), lambda i: (jnp.minimum(i, a_blocks - 1), 0)),
                  pl.BlockSpec((tm, d), lambda i: (jnp.maximum(i - a_blocks, 0), 0))],
        out_specs=[row, row],
        out_shape=[jax.ShapeDtypeStruct((ta + tb, d), jnp.float32), jax.ShapeDtypeStruct((ta + tb, d), jnp.bfloat16)],
        compiler_params=_cparams(("arbitrary",)),
        name="concat_cast",
    )(a, b)


def _banded_kernel(*refs, half, nk, group, step, bounds, use_sink, emit_lse, row_axis, head_axis):
    refs = list(refs)
    slopes_ref = refs.pop(0)
    sink_ref = refs.pop(0) if use_sink else None
    q_ref, kp_ref, kc_ref, kn_ref, vp_ref, vc_ref, vn_ref = refs[:7]
    o_ref = refs[7]
    lse_ref = refs[8] if emit_lse else None

    ub = pl.program_id(row_axis)
    hgi = pl.program_id(head_axis)
    s0, e0 = _seg_bounds(ub * Q_BLOCK, bounds)
    has_prev = ub * Q_BLOCK > s0
    has_next = (ub + 1) * Q_BLOCK < e0
    nsub = Q_BLOCK // Q_SUB
    win = Q_BLOCK + 2 * half
    row = lax.broadcasted_iota(jnp.int32, (Q_SUB, win), 0)
    col = lax.broadcasted_iota(jnp.int32, (Q_SUB, win), 1)
    edge_ok = ((col >= half) | has_prev) & ((col < half + Q_BLOCK) | has_next)
    valids, distfs = [], []
    for sb in range(nsub):
        dist = jnp.abs(col - half - row - sb * Q_SUB)
        valids.append((dist <= half) & edge_ok)
        distfs.append((step * dist).astype(jnp.float32))
    scale2 = HEAD_DIM ** -0.5 * LOG2E
    lane = lax.broadcasted_iota(jnp.int32, (Q_SUB, LANES), 1)
    m_tiles = [jnp.zeros((Q_SUB, LANES), jnp.float32) for _ in range(nsub)]
    l_tiles = [jnp.ones((Q_SUB, LANES), jnp.float32) for _ in range(nsub)]

    for k in range(nk):
        ksl = slice(k * HEAD_DIM, (k + 1) * HEAD_DIM)
        kwin = jnp.concatenate([kp_ref[:, ksl], kc_ref[:, ksl], kn_ref[:, ksl]], axis=0)
        vwin = jnp.concatenate([vp_ref[:, ksl], vc_ref[:, ksl], vn_ref[:, ksl]], axis=0)
        for g in range(group):
            hidx = k * group + g
            hsl = slice(hidx * HEAD_DIM, (hidx + 1) * HEAD_DIM)
            head = hgi * (nk * group) + hidx
            slope2 = slopes_ref[head] * LOG2E
            for sb in range(nsub):
                rows = slice(sb * Q_SUB, (sb + 1) * Q_SUB)
                s = lax.dot_general(q_ref[rows, hsl], kwin, _NT, preferred_element_type=jnp.float32) * scale2
                s = jnp.where(valids[sb], s - slope2 * distfs[sb], NEG_INF)
                m = jnp.max(s, axis=-1, keepdims=True)
                if use_sink:
                    sk = sink_ref[head] * LOG2E
                    m = jnp.maximum(m, sk)
                p = jnp.exp2(s - m)
                l = jnp.sum(p, axis=-1, keepdims=True)
                if use_sink:
                    l = l + jnp.exp2(sk - m)
                o = jnp.dot(p.astype(jnp.bfloat16), vwin, preferred_element_type=jnp.float32) / l
                if emit_lse:
                    m_tiles[sb] = jnp.where(lane == hidx, m, m_tiles[sb])
                    l_tiles[sb] = jnp.where(lane == hidx, l, l_tiles[sb])
                o_ref[rows, hsl] = o.astype(o_ref.dtype)
    if emit_lse:
        for sb in range(nsub):
            lse_ref[sb * Q_SUB:(sb + 1) * Q_SUB, :] = (m_tiles[sb] + jnp.log2(l_tiles[sb])) * LN2


def attention_a(qkv, slopes, sink, segs):
    t = qkv.shape[0]
    nblk = t // Q_BLOCK
    bounds = _bounds(segs, 1)
    qw, kw = A_Q, A_KV
    k_col, v_col = A_Q // kw, (A_Q + A_KV) // kw
    prev = lambda b: jnp.maximum(b - 1, 0)
    nxt = lambda b: jnp.minimum(b + 1, nblk - 1)
    smem = pl.BlockSpec(memory_space=pltpu.SMEM)
    in_specs = [smem, smem,
                pl.BlockSpec((Q_BLOCK, qw), lambda b, h: (b, 0)),
                pl.BlockSpec((Q_BLOCK, kw), lambda b, h: (prev(b), k_col)),
                pl.BlockSpec((Q_BLOCK, kw), lambda b, h: (b, k_col)),
                pl.BlockSpec((Q_BLOCK, kw), lambda b, h: (nxt(b), k_col)),
                pl.BlockSpec((Q_BLOCK, kw), lambda b, h: (prev(b), v_col)),
                pl.BlockSpec((Q_BLOCK, kw), lambda b, h: (b, v_col)),
                pl.BlockSpec((Q_BLOCK, kw), lambda b, h: (nxt(b), v_col))]
    kern = functools.partial(_banded_kernel, half=A_HALF_WINDOW, nk=A_KV_HEADS, group=A_GROUP, step=1,
                             bounds=bounds, use_sink=True, emit_lse=False, row_axis=0, head_axis=1)
    return pl.pallas_call(
        kern,
        grid=(nblk, 1),
        in_specs=in_specs,
        out_specs=pl.BlockSpec((Q_BLOCK, qw), lambda b, h: (b, 0)),
        out_shape=jax.ShapeDtypeStruct((t, qw), jnp.bfloat16),
        compiler_params=_cparams(("parallel", "arbitrary")),
        name="attention_a",
    )(slopes, sink, *([qkv] * 7))


def attention_c_branch(z, slopes, dil, segs):
    rows = z.shape[1]
    nub = rows // Q_BLOCK
    half = C_HALF
    hw = C_HEAD_GROUP * HEAD_DIM
    bounds = _bounds(segs, dil)
    k_col0, v_col0 = C_W // hw, 2 * C_W // hw
    nhalf = rows // half
    prev = lambda u: jnp.maximum(2 * u - 1, 0)
    nxt = lambda u: jnp.minimum(2 * u + 2, nhalf - 1)
    smem = pl.BlockSpec(memory_space=pltpu.SMEM)
    in_specs = [smem,
                pl.BlockSpec((None, Q_BLOCK, hw), lambda r, u, h: (r, u, h)),
                pl.BlockSpec((None, half, hw), lambda r, u, h: (r, prev(u), k_col0 + h)),
                pl.BlockSpec((None, Q_BLOCK, hw), lambda r, u, h: (r, u, k_col0 + h)),
                pl.BlockSpec((None, half, hw), lambda r, u, h: (r, nxt(u), k_col0 + h)),
                pl.BlockSpec((None, half, hw), lambda r, u, h: (r, prev(u), v_col0 + h)),
                pl.BlockSpec((None, Q_BLOCK, hw), lambda r, u, h: (r, u, v_col0 + h)),
                pl.BlockSpec((None, half, hw), lambda r, u, h: (r, nxt(u), v_col0 + h))]
    kern = functools.partial(_banded_kernel, half=half, nk=C_HEAD_GROUP, group=1, step=dil, bounds=bounds,
                             use_sink=False, emit_lse=True, row_axis=1, head_axis=2)
    return pl.pallas_call(
        kern,
        grid=(dil, nub, C_GROUPS),
        in_specs=in_specs,
        out_specs=[pl.BlockSpec((None, Q_BLOCK, hw), lambda r, u, h: (r, u, h)),
                   pl.BlockSpec((None, Q_BLOCK, LANES), lambda r, u, h: (r, u, h))],
        out_shape=[jax.ShapeDtypeStruct((dil, rows, C_W), jnp.bfloat16),
                   jax.ShapeDtypeStruct((dil, rows, C_GROUPS * LANES), jnp.float32)],
        compiler_params=_cparams(("parallel", "parallel", "arbitrary")),
        name="attention_c_d%d" % dil,
    )(slopes, *([z] * 7))


def _plane_perm(dil, tm):
    p = np.zeros((tm, tm), np.float32)
    tok = np.arange(tm)
    p[tok, (tok % dil) * (tm // dil) + tok // dil] = 1.0
    return jnp.asarray(p, jnp.bfloat16)


def _split3(x):
    hi = x.astype(jnp.bfloat16)
    r = x - hi.astype(jnp.float32)
    mid = r.astype(jnp.bfloat16)
    lo = (r - mid.astype(jnp.float32)).astype(jnp.bfloat16)
    return hi, mid, lo


def _merge_kernel(*refs, dils):
    nb = len(dils)
    o_refs, l_refs = refs[:nb], refs[nb:2 * nb]
    p_refs = refs[2 * nb:2 * nb + nb - 1]
    out_ref = refs[-1]
    outs, lses = [o_refs[0][0].astype(jnp.float32)], [l_refs[0][0]]
    for b in range(1, nb):
        d = dils[b]
        perm = p_refs[b - 1][...]
        stacked = jnp.concatenate([o_refs[b][r] for r in range(d)], axis=0)
        outs.append(jnp.dot(perm, stacked, preferred_element_type=jnp.float32))
        lstack = jnp.concatenate([l_refs[b][r] for r in range(d)], axis=0)
        lses.append(sum(jnp.dot(perm, part, preferred_element_type=jnp.float32) for part in _split3(lstack)))
    for h in range(C_HEADS):
        hsl = slice(h * HEAD_DIM, (h + 1) * HEAD_DIM)
        col = (h // C_HEAD_GROUP) * LANES + h % C_HEAD_GROUP
        ls = [l[:, col:col + 1] for l in lses]
        m = functools.reduce(jnp.maximum, ls)
        es = [jnp.exp(l - m) for l in ls]
        tot = sum(es)
        acc = sum((e / tot) * o[:, hsl] for e, o in zip(es, outs))
        out_ref[:, hsl] = acc.astype(out_ref.dtype)


def merge_branches(os_, ls_, dils):
    t = os_[0].shape[0] * os_[0].shape[1]
    tm = min(MERGE_TILE, t)
    in_specs, args = [], []
    for arr, width in ((os_, C_W), (ls_, C_GROUPS * LANES)):
        for a, d in zip(arr, dils):
            in_specs.append(pl.BlockSpec((d, tm // d, width), lambda i: (0, i, 0)))
            args.append(a)
    for d in dils[1:]:
        in_specs.append(pl.BlockSpec((tm, tm), lambda i: (0, 0)))
        args.append(_plane_perm(d, tm))
    return pl.pallas_call(
        functools.partial(_merge_kernel, dils=tuple(dils)),
        grid=(t // tm,),
        in_specs=in_specs,
        out_specs=pl.BlockSpec((tm, C_W), lambda i: (i, 0)),
        out_shape=jax.ShapeDtypeStruct((t, C_W), jnp.bfloat16),
        compiler_params=_cparams(("parallel",)),
        name="merge_branches",
    )(*args)


def attention_c(zs, slopes, segs):
    dils = tuple(d for _, d in C_BRANCHES)
    res = [attention_c_branch(z, slopes, d, segs) for z, d in zip(zs, dils)]
    return merge_branches([r[0] for r in res], [r[1] for r in res], dils)


NA_PAIR = 2
NA_WIN_ROWS = NA_KH + NA_PAIR
NA_WIN_BLOCKS = NA_WIN_ROWS // NA_PAIR
NA_PAIR_CLASSES = ((0, 3, 1, 3), (0, 7, 0, 6), (0, 5, 0, 4), (2, 3, 2, 2), (2, 1, 2, 0))


def _natten_kernel(*refs):
    q_ref = refs[0]
    k_refs = refs[1:1 + NA_WIN_BLOCKS]
    v_refs = refs[1 + NA_WIN_BLOCKS:1 + 2 * NA_WIN_BLOCKS]
    bias_ref, o_ref = refs[1 + 2 * NA_WIN_BLOCKS:]
    scale2 = HEAD_DIM ** -0.5 * LOG2E
    for h in range(B_HEAD_GROUP):
        hsl = slice(h * HEAD_DIM, (h + 1) * HEAD_DIM)
        kwin = jnp.concatenate([r[:, hsl] for r in k_refs], axis=0)
        vwin = jnp.concatenate([r[:, hsl] for r in v_refs], axis=0)
        s = lax.dot_general(q_ref[:, hsl], kwin, _NT, preferred_element_type=jnp.float32) * scale2 + bias_ref[h]
        m = jnp.max(s, axis=-1, keepdims=True)
        p = jnp.exp2(s - m)
        l = jnp.sum(p, axis=-1, keepdims=True)
        o = jnp.dot(p.astype(jnp.bfloat16), vwin, preferred_element_type=jnp.float32) / l
        o_ref[:, hsl] = o.astype(o_ref.dtype)


def natten_bias_table(rpb):
    ncls = len(NA_PAIR_CLASSES)
    sel = np.zeros((ncls, NA_PAIR, NA_WIN_ROWS, 2 * NA_KH - 1), np.float32)
    row_ok = np.zeros((ncls, NA_PAIR, NA_WIN_ROWS), bool)
    for ci, cls in enumerate(NA_PAIR_CLASSES):
        for a in range(NA_PAIR):
            off, o = cls[2 * a], cls[2 * a + 1]
            for i in range(NA_KH):
                sel[ci, a, off + i, o + i] = 1.0
                row_ok[ci, a, off + i] = True
    c = jnp.arange(GRID_W)
    cs = jnp.clip(c - NA_KW // 2, 0, GRID_W - NA_KW)
    col_ok = (c[None, :] >= cs[:, None]) & (c[None, :] < cs[:, None] + NA_KW)
    dc = jnp.clip(c[None, :] - c[:, None] + (NA_KW - 1), 0, 2 * NA_KW - 2)
    pick = (dc[:, :, None] == jnp.arange(2 * NA_KW - 1)[None, None, :]).astype(jnp.float32)
    hi = lax.Precision.HIGHEST
    rows = jnp.einsum("capd,hdx->caphx", jnp.asarray(sel), rpb.astype(jnp.float32), precision=hi)
    tab = jnp.einsum("caphx,qkx->chaqpk", rows, pick, precision=hi)
    ok = jnp.asarray(row_ok)[:, None, :, None, :, None] & col_ok[None, None, None, :, None, :]
    tab = jnp.where(ok, tab * LOG2E, NEG_INF)
    return tab.reshape(ncls, B_HEADS, NA_PAIR * GRID_W, NA_WIN_ROWS * GRID_W)


def attention_b(qkv, bias_tab, segs):
    t = qkv.shape[0]
    nrows = t // GRID_W
    bounds = _bounds(segs, GRID_W)
    hw = B_HEAD_GROUP * HEAD_DIM
    ngroups = B_HEADS // B_HEAD_GROUP
    q_col0 = (A_Q + 2 * A_KV) // hw
    k_col0 = q_col0 + B_W // hw
    v_col0 = k_col0 + B_W // hw

    assert all(re - rs >= 2 * NA_KH and (re - rs) % NA_PAIR == 0 for rs, re in bounds)
    blk = NA_PAIR * GRID_W

    def win_block(pr):
        rs, re = _seg_bounds(pr * NA_PAIR, bounds)
        return jnp.clip(pr * NA_PAIR - NA_KH // 2, rs, re - NA_WIN_ROWS) // NA_PAIR

    def pair_class(pr):
        rs, re = _seg_bounds(pr * NA_PAIR, bounds)
        from_start, to_end = pr * NA_PAIR - rs, re - pr * NA_PAIR
        return jnp.where(from_start == 0, 1, jnp.where(from_start == 2, 2,
                         jnp.where(to_end == 4, 3, jnp.where(to_end == 2, 4, 0))))

    def kv_spec(col0, i):
        return pl.BlockSpec((blk, hw), lambda h, pr: (win_block(pr) + i, col0 + h))

    in_specs = [pl.BlockSpec((blk, hw), lambda h, pr: (pr, q_col0 + h))]
    in_specs += [kv_spec(k_col0, i) for i in range(NA_WIN_BLOCKS)]
    in_specs += [kv_spec(v_col0, i) for i in range(NA_WIN_BLOCKS)]
    in_specs.append(pl.BlockSpec((None, B_HEAD_GROUP, blk, NA_WIN_ROWS * GRID_W),
                                 lambda h, pr: (pair_class(pr), h, 0, 0)))
    return pl.pallas_call(
        _natten_kernel,
        grid=(ngroups, nrows // NA_PAIR),
        in_specs=in_specs,
        out_specs=pl.BlockSpec((blk, hw), lambda h, pr: (pr, h)),
        out_shape=jax.ShapeDtypeStruct((t, B_W), jnp.bfloat16),
        compiler_params=_cparams(("parallel", "arbitrary")),
        name="attention_b",
    )(*([qkv] * (1 + 2 * NA_WIN_BLOCKS)), bias_tab)


def _top2(vals):
    v1, i1 = vals[0], jnp.zeros(vals[0].shape, jnp.int32)
    for j in range(1, len(vals)):
        better = vals[j] > v1
        v1 = jnp.where(better, vals[j], v1)
        i1 = jnp.where(better, j, i1)
    v2, i2 = None, None
    for j in range(len(vals)):
        cand = jnp.where(i1 == j, -1.0, vals[j])
        if v2 is None:
            v2, i2 = cand, jnp.zeros(vals[0].shape, jnp.int32)
        else:
            better = cand > v2
            v2 = jnp.where(better, cand, v2)
            i2 = jnp.where(better, j, i2)
    return v1, i1, v2, i2


def _router_kernel(x_ref, w2_ref, whi_ref, b_ref, idx_ref, gate_ref):
    x = x_ref[...]
    x_hi = x.astype(jnp.bfloat16)
    x_lo = (x - x_hi.astype(jnp.float32)).astype(jnp.bfloat16)
    r1 = lax.dot_general(w2_ref[...], x_hi, _NT, preferred_element_type=jnp.float32)
    r2 = lax.dot_general(whi_ref[...], x_lo, _NT, preferred_element_type=jnp.float32)
    logits = r1[:N_EXPERTS] + r1[N_EXPERTS:] + r2 + b_ref[...]
    m = jnp.max(logits, axis=0, keepdims=True)
    ex = jnp.exp(logits - m)
    probs = ex / jnp.sum(ex, axis=0, keepdims=True)
    rows = [probs[e:e + 1, :] for e in range(N_EXPERTS)]
    tops = [_top2(rows[g * EXPERTS_PER_GROUP:(g + 1) * EXPERTS_PER_GROUP]) for g in range(N_GROUPS)]
    score = [tp[0] + tp[2] for tp in tops]
    best, gsel = score[0], jnp.zeros(score[0].shape, jnp.int32)
    for g in range(1, N_GROUPS):
        better = score[g] > best
        best = jnp.where(better, score[g], best)
        gsel = jnp.where(better, g, gsel)
    v1, i1, v2, i2 = tops[0]
    for g in range(1, N_GROUPS):
        pick = gsel == g
        v1 = jnp.where(pick, tops[g][0], v1)
        i1 = jnp.where(pick, tops[g][1], i1)
        v2 = jnp.where(pick, tops[g][2], v2)
        i2 = jnp.where(pick, tops[g][3], i2)
    tot = v1 + v2
    idx_ref[0:1, :] = gsel * EXPERTS_PER_GROUP + i1
    idx_ref[1:2, :] = gsel * EXPERTS_PER_GROUP + i2
    gate_ref[0:1, :] = v1 / tot
    gate_ref[1:2, :] = v2 / tot


def router(x, router_w, router_b, tm=512):
    t, d = x.shape
    tm = min(tm, t)
    wt = router_w.astype(jnp.float32).T
    w_hi = wt.astype(jnp.bfloat16)
    w_lo = (wt - w_hi.astype(jnp.float32)).astype(jnp.bfloat16)
    w2 = jnp.concatenate([w_hi, w_lo], axis=0)
    return pl.pallas_call(
        _router_kernel,
        grid=(t // tm,),
        in_specs=[pl.BlockSpec((tm, d), lambda i: (i, 0)),
                  pl.BlockSpec((2 * N_EXPERTS, d), lambda i: (0, 0)),
                  pl.BlockSpec((N_EXPERTS, d), lambda i: (0, 0)),
                  pl.BlockSpec((N_EXPERTS, 1), lambda i: (0, 0))],
        out_specs=[pl.BlockSpec((2, tm), lambda i: (0, i))] * 2,
        out_shape=[jax.ShapeDtypeStruct((2, t), jnp.int32), jax.ShapeDtypeStruct((2, t), jnp.float32)],
        compiler_params=_cparams(("parallel",)),
        name="router",
    )(x, w2, w_hi, router_b.astype(jnp.float32).reshape(N_EXPERTS, 1))


def dispatch_plan(idx):
    _, t = idx.shape
    nt = t // MOE_TILE
    n_chunks = _round_up(nt * CHUNKS_PER_TILE + N_EXPERTS * (CHUNKS_PER_EXPERT_TILE - 1), CHUNKS_PER_EXPERT_TILE)
    i32 = jnp.int32
    e_tile = idx.reshape(2, nt, MOE_TILE).transpose(1, 0, 2).reshape(nt, 2 * MOE_TILE)
    hit = e_tile[:, :, None] == jnp.arange(N_EXPERTS, dtype=i32)[None, None, :]
    onehot = hit.astype(i32)
    tri = (jnp.arange(2 * MOE_TILE)[:, None] >= jnp.arange(2 * MOE_TILE)[None, :]).astype(jnp.bfloat16)
    csum = jnp.einsum("ab,nbe->nae", tri, hit.astype(jnp.bfloat16), preferred_element_type=jnp.float32).astype(i32)
    rank = jnp.sum(csum * onehot, axis=-1) - 1
    cnt = csum[:, -1, :]
    nc = (cnt + SLOT_CHUNK - 1) // SLOT_CHUNK
    lo = jnp.cumsum(nc, axis=1) - nc
    local_pos = jnp.sum(onehot * (lo * SLOT_CHUNK)[:, None, :], axis=-1) + rank
    pos = local_pos.reshape(nt, 2, MOE_TILE).transpose(1, 0, 2).reshape(2, t).astype(i32)

    nc_t = nc.T
    e_chunks = nc_t.sum(axis=1)
    e_pad = _round_up(e_chunks, CHUNKS_PER_EXPERT_TILE)
    e_end = jnp.cumsum(e_pad)
    run_start = (e_end - e_pad)[:, None] + jnp.cumsum(nc_t, axis=1) - nc_t
    src0 = jnp.arange(nt, dtype=i32)[None, :] * CHUNKS_PER_TILE + lo.T
    rs_f, nc_f, src0_f = run_start.reshape(1, -1), nc_t.reshape(1, -1), src0.reshape(1, -1)
    d = jnp.arange(n_chunks, dtype=i32)[:, None]
    in_run = (d >= rs_f) & (d < rs_f + nc_f)
    fwd = jnp.sum(jnp.where(in_run, src0_f + d - rs_f, 0), axis=1).astype(i32)

    lc = jnp.arange(CHUNKS_PER_TILE, dtype=i32)[None, :, None]
    lo_b, nc_b = lo[:, None, :], nc[:, None, :]
    in_run_b = (lc >= lo_b) & (lc < lo_b + nc_b)
    back = jnp.sum(jnp.where(in_run_b, run_start.T[:, None, :] + lc - lo_b, 0), axis=-1)
    back = back.reshape(nt * CHUNKS_PER_TILE).astype(i32)

    n_tiles = n_chunks // CHUNKS_PER_EXPERT_TILE
    tile_first = jnp.arange(n_tiles, dtype=i32) * CHUNKS_PER_EXPERT_TILE
    tile_expert = jnp.minimum(jnp.sum((e_end[None, :] <= tile_first[:, None]).astype(i32), axis=1),
                              N_EXPERTS - 1).astype(i32)
    n_active = (e_end[-1] // CHUNKS_PER_EXPERT_TILE).astype(i32).reshape(1)
    return pos, fwd, back, tile_expert, n_active


def _permute_kernel(x_ref, pos_ref, gate_ref, xs_ref, gl_ref):
    tt, d = x_ref.shape
    slot = lax.broadcasted_iota(jnp.int32, (LOCAL_SLOTS, tt), 0)
    hit1 = slot == pos_ref[0:1, :]
    hit2 = slot == pos_ref[1:2, :]
    perm = jnp.where(hit1 | hit2, 1.0, 0.0).astype(jnp.bfloat16)
    cw = min(d, 512)
    for n in range(d // cw):
        sl = slice(n * cw, (n + 1) * cw)
        xs_ref[:, sl] = jnp.dot(perm, x_ref[:, sl], preferred_element_type=jnp.float32).astype(xs_ref.dtype)
    gates = jnp.where(hit1, gate_ref[0:1, :], 0.0) + jnp.where(hit2, gate_ref[1:2, :], 0.0)
    gl_ref[...] = jnp.broadcast_to(jnp.sum(gates, axis=1, keepdims=True), gl_ref.shape)


def permute_tokens(x_bf, pos, gate):
    t, d = x_bf.shape
    nt = t // MOE_TILE
    return pl.pallas_call(
        _permute_kernel,
        grid=(nt,),
        in_specs=[pl.BlockSpec((MOE_TILE, d), lambda i: (i, 0)),
                  pl.BlockSpec((2, MOE_TILE), lambda i: (0, i)),
                  pl.BlockSpec((2, MOE_TILE), lambda i: (0, i))],
        out_specs=[pl.BlockSpec((LOCAL_SLOTS, d), lambda i: (i, 0)),
                   pl.BlockSpec((LOCAL_SLOTS, LANES), lambda i: (i, 0))],
        out_shape=[jax.ShapeDtypeStruct((nt * LOCAL_SLOTS, d), jnp.bfloat16),
                   jax.ShapeDtypeStruct((nt * LOCAL_SLOTS, LANES), jnp.float32)],
        compiler_params=_cparams(("parallel",)),
        name="permute_tokens",
    )(x_bf, pos, gate)


def _unpermute_kernel(back_ref, ys_ref, gl_ref, p1_ref, p2_ref, y_ref, buf_ref, sems):
    i = pl.program_id(0)
    tt, d = y_ref.shape
    n = CHUNKS_PER_TILE
    buf_slot = i % 2

    @pl.when(i == 0)
    def _():
        _fetch_chunks(ys_ref, back_ref, 0, n, buf_ref, 0, sems)

    @pl.when(i + 1 < pl.num_programs(0))
    def _():
        _fetch_chunks(ys_ref, back_ref, (i + 1) * n, n, buf_ref, 1 - buf_slot, sems)

    slot = lax.broadcasted_iota(jnp.int32, (tt, LOCAL_SLOTS), 1)
    hit = (slot == p1_ref[:, 0:1]) | (slot == p2_ref[:, 0:1])
    perm = jnp.where(hit, 1.0, 0.0).astype(jnp.bfloat16)
    g = gl_ref[:, 0:1]
    _wait_chunks(ys_ref, n, buf_ref, buf_slot, sems)
    cw = min(d, 512)
    for c in range(d // cw):
        sl = slice(c * cw, (c + 1) * cw)
        scaled = (buf_ref[buf_slot, :, sl].astype(jnp.float32) * g).astype(jnp.bfloat16)
        y_ref[:, sl] = jnp.dot(perm, scaled, preferred_element_type=jnp.float32)


UNPERMUTE_SPLIT = 2


def _unpermute_ln_kernel(back_ref, ys_ref, gl_ref, p1_ref, p2_ref, x_ref, g_ref, b_ref, o_ref, obf_ref,
                         buf_ref, sems):
    i, part = pl.program_id(0), pl.program_id(1)
    rows, d = o_ref.shape
    n = CHUNKS_PER_TILE
    buf_slot = i % 2
    cw = min(d, 512)

    @pl.when((i == 0) & (part == 0))
    def _():
        _fetch_chunks(ys_ref, back_ref, 0, n, buf_ref, 0, sems)

    @pl.when((part == 0) & (i + 1 < pl.num_programs(0)))
    def _():
        _fetch_chunks(ys_ref, back_ref, (i + 1) * n, n, buf_ref, 1 - buf_slot, sems)

    @pl.when(part == 0)
    def _():
        _wait_chunks(ys_ref, n, buf_ref, buf_slot, sems)
        gate = gl_ref[:, 0:1]
        for c in range(d // cw):
            sl = slice(c * cw, (c + 1) * cw)
            buf_ref[buf_slot, :, sl] = (buf_ref[buf_slot, :, sl].astype(jnp.float32) * gate).astype(buf_ref.dtype)

    slot = lax.broadcasted_iota(jnp.int32, (rows, LOCAL_SLOTS), 1)
    hit = (slot == p1_ref[:, 0:1]) | (slot == p2_ref[:, 0:1])
    perm = jnp.where(hit, 1.0, 0.0).astype(jnp.bfloat16)
    for c in range(d // cw):
        sl = slice(c * cw, (c + 1) * cw)
        o_ref[:, sl] = jnp.dot(perm, buf_ref[buf_slot, :, sl], preferred_element_type=jnp.float32)
    out = _layer_norm_rows(DEEPNORM_ALPHA * x_ref[...] + o_ref[...], g_ref[...], b_ref[...])
    o_ref[...] = out
    obf_ref[...] = out.astype(jnp.bfloat16)


def unpermute_resid_ln(ys, back, gate_local, pos, x, g, b):
    t, d = x.shape
    nt = t // MOE_TILE
    rows = MOE_TILE // UNPERMUTE_SPLIT
    p1 = jnp.broadcast_to(pos[0][:, None], (t, LANES))
    p2 = jnp.broadcast_to(pos[1][:, None], (t, LANES))
    row = lambda width: pl.BlockSpec((rows, width), lambda i, p, bk: (i * UNPERMUTE_SPLIT + p, 0))
    vec = pl.BlockSpec((1, d), lambda i, p, bk: (0, 0))
    return pl.pallas_call(
        _unpermute_ln_kernel,
        grid_spec=pltpu.PrefetchScalarGridSpec(
            num_scalar_prefetch=1,
            grid=(nt, UNPERMUTE_SPLIT),
            in_specs=[pl.BlockSpec(memory_space=pl.ANY),
                      pl.BlockSpec((LOCAL_SLOTS, LANES), lambda i, p, bk: (i, 0)),
                      row(LANES), row(LANES), row(d), vec, vec],
            out_specs=[row(d), row(d)],
            scratch_shapes=[pltpu.VMEM((2, LOCAL_SLOTS, d), ys.dtype), pltpu.SemaphoreType.DMA((2,))]),
        out_shape=[jax.ShapeDtypeStruct((t, d), jnp.float32), jax.ShapeDtypeStruct((t, d), jnp.bfloat16)],
        compiler_params=_cparams(("arbitrary", "arbitrary")),
        name="unpermute_resid_ln",
    )(back, ys.reshape(ys.shape[0] // SLOT_CHUNK, SLOT_CHUNK, d), gate_local, p1, p2, x,
      g.reshape(1, d), b.reshape(1, d))


def unpermute_tokens(ys, back, gate_local, pos):
    _, t = pos.shape
    d = ys.shape[1]
    nt = t // MOE_TILE
    p1 = jnp.broadcast_to(pos[0][:, None], (t, LANES))
    p2 = jnp.broadcast_to(pos[1][:, None], (t, LANES))
    return pl.pallas_call(
        _unpermute_kernel,
        grid_spec=pltpu.PrefetchScalarGridSpec(
            num_scalar_prefetch=1,
            grid=(nt,),
            in_specs=[pl.BlockSpec(memory_space=pl.ANY),
                      pl.BlockSpec((LOCAL_SLOTS, LANES), lambda i, bk: (i, 0)),
                      pl.BlockSpec((MOE_TILE, LANES), lambda i, bk: (i, 0)),
                      pl.BlockSpec((MOE_TILE, LANES), lambda i, bk: (i, 0))],
            out_specs=pl.BlockSpec((MOE_TILE, d), lambda i, bk: (i, 0)),
            scratch_shapes=[pltpu.VMEM((2, LOCAL_SLOTS, d), ys.dtype), pltpu.SemaphoreType.DMA((2,))]),
        out_shape=jax.ShapeDtypeStruct((t, d), jnp.float32),
        compiler_params=_cparams(("arbitrary",)),
        name="unpermute_tokens",
    )(back, ys.reshape(ys.shape[0] // SLOT_CHUNK, SLOT_CHUNK, d), gate_local, p1, p2)


def _chunk_copy(src_ref, chunk, buf_ref, slot, g, sems):
    return pltpu.make_async_copy(src_ref.at[chunk], buf_ref.at[slot, pl.ds(g * SLOT_CHUNK, SLOT_CHUNK)],
                                 sems.at[slot])


def _fetch_chunks(src_ref, idx_ref, first, n, buf_ref, slot, sems):
    for g in range(n):
        _chunk_copy(src_ref, idx_ref[first + g], buf_ref, slot, g, sems).start()


def _wait_chunks(src_ref, n, buf_ref, slot, sems):
    for g in range(n):
        _chunk_copy(src_ref, 0, buf_ref, slot, g, sems).wait()


def _expert_up_kernel(te_ref, na_ref, fwd_ref, xs1_ref, w1_ref, w3_ref, h_ref, buf_ref, sems):
    j, i = pl.program_id(0), pl.program_id(1)
    na = na_ref[0]
    active = i < na

    @pl.when(active)
    def _():
        a = j * na + i
        slot = a % 2
        n = CHUNKS_PER_EXPERT_TILE

        @pl.when(a == 0)
        def _():
            _fetch_chunks(xs1_ref, fwd_ref, 0, n, buf_ref, 0, sems)

        @pl.when(a + 1 < pl.num_programs(0) * na)
        def _():
            nxt = jnp.where(i + 1 < na, i + 1, 0)
            _fetch_chunks(xs1_ref, fwd_ref, nxt * n, n, buf_ref, 1 - slot, sems)

        _wait_chunks(xs1_ref, n, buf_ref, slot, sems)
        x = buf_ref[slot]
        h1 = jnp.dot(x, w1_ref[...].astype(jnp.bfloat16), preferred_element_type=jnp.float32)
        h3 = jnp.dot(x, w3_ref[...].astype(jnp.bfloat16), preferred_element_type=jnp.float32)
        h_ref[...] = (jax.nn.silu(h1) * h3).astype(h_ref.dtype)

    @pl.when(jnp.logical_not(active))
    def _():
        h_ref[...] = jnp.zeros(h_ref.shape, h_ref.dtype)


def _expert_down_kernel(te_ref, na_ref, h_ref, w2_ref, y_ref):
    active = pl.program_id(1) < na_ref[0]

    @pl.when(active)
    def _():
        y_ref[...] = jnp.dot(h_ref[...], w2_ref[...].astype(jnp.bfloat16),
                             preferred_element_type=jnp.float32).astype(y_ref.dtype)

    @pl.when(jnp.logical_not(active))
    def _():
        y_ref[...] = jnp.zeros(y_ref.shape, y_ref.dtype)


def expert_ffn(xs1, fwd, w1, w3, w2, layer, tile_expert, n_active, tf=512, tn=2048):
    d = xs1.shape[1]
    f = w1.shape[3]
    tm = EXPERT_TILE
    tf, tn = min(tf, f), min(tn, d)
    n_tiles = fwd.shape[0] // CHUNKS_PER_EXPERT_TILE
    s = n_tiles * tm

    def act(i, na):
        return jnp.minimum(i, na[0] - 1)

    up = pl.pallas_call(
        _expert_up_kernel,
        grid_spec=pltpu.PrefetchScalarGridSpec(
            num_scalar_prefetch=3,
            grid=(f // tf, n_tiles),
            in_specs=[pl.BlockSpec(memory_space=pl.ANY),
                      pl.BlockSpec((None, None, d, tf), lambda j, i, te, na, fw: (layer, te[act(i, na)], 0, j)),
                      pl.BlockSpec((None, None, d, tf), lambda j, i, te, na, fw: (layer, te[act(i, na)], 0, j))],
            out_specs=pl.BlockSpec((tm, tf), lambda j, i, te, na, fw: (i, j)),
            scratch_shapes=[pltpu.VMEM((2, tm, d), xs1.dtype), pltpu.SemaphoreType.DMA((2,))]),
        out_shape=jax.ShapeDtypeStruct((s, f), jnp.bfloat16),
        compiler_params=_cparams(("arbitrary", "arbitrary")),
        name="expert_up",
    )
    h = up(tile_expert, n_active, fwd, xs1.reshape(xs1.shape[0] // SLOT_CHUNK, SLOT_CHUNK, d), w1, w3)
    down = pl.pallas_call(
        _expert_down_kernel,
        grid_spec=pltpu.PrefetchScalarGridSpec(
            num_scalar_prefetch=2,
            grid=(d // tn, n_tiles),
            in_specs=[pl.BlockSpec((tm, f), lambda j, i, te, na: (act(i, na), 0)),
                      pl.BlockSpec((None, None, f, tn), lambda j, i, te, na: (layer, te[act(i, na)], 0, j))],
            out_specs=pl.BlockSpec((tm, tn), lambda j, i, te, na: (i, j))),
        out_shape=jax.ShapeDtypeStruct((s, d), jnp.bfloat16),
        compiler_params=_cparams(("arbitrary", "arbitrary")),
        name="expert_down",
    )
    return down(tile_expert, n_active, h, w2)


def moe_layer(x, x_bf, router_w, router_b, w1, w3, w2, layer, g, b, out_rows=None):
    idx, gate = router(x, router_w, router_b)
    pos, fwd, back, tile_expert, n_active = dispatch_plan(idx)
    xs1, gate_local = permute_tokens(x_bf, pos, gate)
    ys = expert_ffn(xs1, fwd, w1, w3, w2, layer, tile_expert, n_active)
    if out_rows is None:
        return unpermute_resid_ln(ys, back, gate_local, pos, x, g, b)
    y = unpermute_tokens(ys, back, gate_local, pos)
    return [resid_ln(x, y, g, b, rows=r)[0] for r in out_rows]


def alibi_slopes(n):
    return jnp.exp2(-8.0 * jnp.arange(1, n + 1, dtype=jnp.float32) / n)


def trunk(x, x_bf, segs, out_rows, w_in_ab, w_out_ab, a_sink, b_rpb, w_in_c, w_out_c, router_w, router_b,
          moe_w1, moe_w3, moe_w2, ln_g, ln_b):
    bf = jnp.bfloat16
    slopes_a = alibi_slopes(A_HEADS)
    slopes_c = alibi_slopes(C_HEADS)
    dils = tuple(d for _, d in C_BRANCHES)
    for layer in range(DEPTH):
        i = layer // 2
        if layer % 2 == 0:
            qkv = matmul([x_bf], w_in_ab[i].astype(bf), bf)
            oa = attention_a(qkv, slopes_a, a_sink[i].astype(jnp.float32), segs)
            ob = attention_b(qkv, natten_bias_table(b_rpb[i]), segs)
            mixed, w_out = [oa, ob], w_out_ab[i]
        else:
            zs = matmul_dilated(x_bf, w_in_c[i].astype(bf), dils[1:])
            zs = [zs[0].reshape((1,) + zs[0].shape)] + list(zs[1:])
            mixed, w_out = [attention_c(zs, slopes_c, segs)], w_out_c[i]
        mix = matmul(mixed, w_out.astype(bf), jnp.float32)
        x, x_bf = resid_ln(x, mix, ln_g[layer, 0], ln_b[layer, 0])
        last = layer == DEPTH - 1
        res = moe_layer(x, x_bf, router_w, router_b, moe_w1, moe_w3, moe_w2, layer,
                        ln_g[layer, 1], ln_b[layer, 1], out_rows=out_rows if last else None)
        if last:
            return res
        x, x_bf = res


def kernel(x_prompt, x_sample, w_in_ab, w_out_ab, a_sink, b_rpb, w_in_c, w_out_c, router_w, router_b,
           moe_w1, moe_w3, moe_w2, ln_g, ln_b):
    bp, sp, d = x_prompt.shape
    bs, ss, _ = x_sample.shape
    segs = ((bp, sp), (bs, ss))
    x, x_bf = concat_cast(x_prompt.reshape(bp * sp, d), x_sample.reshape(bs * ss, d))
    out_rows = ((0, bp * sp), (bp * sp, bs * ss))
    yp, ys = trunk(x, x_bf, segs, out_rows, w_in_ab, w_out_ab, a_sink, b_rpb, w_in_c, w_out_c, router_w, router_b,
                   moe_w1, moe_w3, moe_w2, ln_g, ln_b)
    return (yp.reshape(bp, sp, d), ys.reshape(bs, ss, d))
```

```python
import functools

import numpy as np
import jax
import jax.numpy as jnp
from jax import lax
from jax.experimental import pallas as pl
from jax.experimental.pallas import tpu as pltpu

HEAD_DIM = 128
GRID_W = 64
A_HEADS = 16
A_KV_HEADS = 4
A_GROUP = A_HEADS // A_KV_HEADS
A_HALF_WINDOW = 128
B_HEADS = 16
NA_KH = 8
NA_KW = 16
C_HEADS = 32
C_BRANCHES = ((128, 1), (512, 4), (2048, 16))
C_HALF = 64
N_EXPERTS = 16
N_GROUPS = 4
EXPERTS_PER_GROUP = N_EXPERTS // N_GROUPS
DEPTH = 4
DEEPNORM_ALPHA = (2 * DEPTH) ** 0.25
LN_EPS = 1e-5
NEG_INF = -1e30
LOG2E = 1.4426950408889634
LN2 = 0.6931471805599453

A_Q = A_HEADS * HEAD_DIM
A_KV = A_KV_HEADS * HEAD_DIM
B_W = B_HEADS * HEAD_DIM
C_W = C_HEADS * HEAD_DIM

LANES = 128
BF16_SUBLANES = 16
Q_BLOCK = 128
Q_SUB = 128
C_HEAD_GROUP = 16
C_GROUPS = C_HEADS // C_HEAD_GROUP
B_HEAD_GROUP = 8
MERGE_TILE = 256
MOE_TILE = 512
SLOT_CHUNK = BF16_SUBLANES
LOCAL_SLOTS = 2 * MOE_TILE + N_EXPERTS * SLOT_CHUNK
CHUNKS_PER_TILE = LOCAL_SLOTS // SLOT_CHUNK
EXPERT_TILE = 512
CHUNKS_PER_EXPERT_TILE = EXPERT_TILE // SLOT_CHUNK
VMEM_LIMIT = 56 * 1024 * 1024

_NT = (((1,), (1,)), ((), ()))


def _cparams(sem):
    return pltpu.CompilerParams(dimension_semantics=sem, vmem_limit_bytes=VMEM_LIMIT)


def _round_up(x, m):
    return (x + m - 1) // m * m


def _seg_bounds(pos, bounds):
    s = jnp.int32(bounds[0][0])
    e = jnp.int32(bounds[0][1])
    for a, b in bounds[1:]:
        inside = pos >= a
        s = jnp.where(inside, jnp.int32(a), s)
        e = jnp.where(inside, jnp.int32(b), e)
    return s, e


def _bounds(segs, unit):
    out, pos = [], 0
    for n, length in segs:
        for _ in range(n):
            out.append((pos // unit, (pos + length) // unit))
            pos += length
    return tuple(out)


def _matmul_kernel(*refs, widths):
    xs, w_ref, o_ref = refs[:len(widths)], refs[len(widths)], refs[len(widths) + 1]
    acc, off = None, 0
    for x_ref, k in zip(xs, widths):
        part = jnp.dot(x_ref[...], w_ref[off:off + k, :], preferred_element_type=jnp.float32)
        acc = part if acc is None else acc + part
        off += k
    o_ref[...] = acc.astype(o_ref.dtype)


def matmul(xs, w, out_dtype, tm=1024, tn=1024):
    t = xs[0].shape[0]
    k, n = w.shape
    widths = tuple(x.shape[1] for x in xs)
    assert sum(widths) == k
    tm, tn = min(tm, t), min(tn, n)
    assert t % tm == 0 and n % tn == 0
    in_specs = [pl.BlockSpec((tm, kw), lambda i, j: (i, 0)) for kw in widths]
    in_specs.append(pl.BlockSpec((k, tn), lambda i, j: (0, j)))
    return pl.pallas_call(
        functools.partial(_matmul_kernel, widths=widths),
        grid=(t // tm, n // tn),
        in_specs=in_specs,
        out_specs=pl.BlockSpec((tm, tn), lambda i, j: (i, j)),
        out_shape=jax.ShapeDtypeStruct((t, n), out_dtype),
        compiler_params=_cparams(("parallel", "arbitrary")),
        name="matmul",
    )(*xs, w)


def _matmul_dilated_kernel(x_ref, w_ref, o_ref, *rest, dils):
    z_refs, acc_ref = rest[:-1], rest[-1]
    tm, tn = o_ref.shape
    acc = jnp.dot(x_ref[...], w_ref[...], preferred_element_type=jnp.float32)
    o_ref[...] = acc.astype(o_ref.dtype)
    for s in range(tn // LANES):
        acc_ref[s] = acc[:, s * LANES:(s + 1) * LANES]
    for z_ref, d in zip(z_refs, dils):
        for rho in range(d):
            for s in range(tn // LANES):
                z_ref[rho, :, s * LANES:(s + 1) * LANES] = (
                    acc_ref[s, pl.ds(rho, tm // d, stride=d), :].astype(z_ref.dtype))


def matmul_dilated(x, w, dils, tm=1024, tn=512):
    t, k = x.shape
    n = w.shape[1]
    tm, tn = min(tm, t), min(tn, n)
    assert t % tm == 0 and n % tn == 0 and all(tm % (d * BF16_SUBLANES) == 0 for d in dils)
    out_specs = [pl.BlockSpec((tm, tn), lambda i, j: (i, j))]
    out_shape = [jax.ShapeDtypeStruct((t, n), jnp.bfloat16)]
    for d in dils:
        out_specs.append(pl.BlockSpec((d, tm // d, tn), lambda i, j: (0, i, j)))
        out_shape.append(jax.ShapeDtypeStruct((d, t // d, n), jnp.bfloat16))
    return pl.pallas_call(
        functools.partial(_matmul_dilated_kernel, dils=tuple(dils)),
        grid=(t // tm, n // tn),
        in_specs=[pl.BlockSpec((tm, k), lambda i, j: (i, 0)), pl.BlockSpec((k, tn), lambda i, j: (0, j))],
        out_specs=out_specs,
        out_shape=out_shape,
        scratch_shapes=[pltpu.VMEM((tn // LANES, tm, LANES), jnp.float32)],
        compiler_params=_cparams(("parallel", "arbitrary")),
        name="matmul_dilated",
    )(x, w)


def _resid_ln_kernel(x_ref, y_ref, g_ref, b_ref, o_ref, obf_ref):
    out = _layer_norm_rows(DEEPNORM_ALPHA * x_ref[...] + y_ref[...], g_ref[...], b_ref[...])
    o_ref[...] = out
    obf_ref[...] = out.astype(jnp.bfloat16)


def resid_ln(x, y, g, b, rows=None, tm=256):
    t, d = x.shape
    start, count = (0, t) if rows is None else rows
    tm = min(tm, count)
    assert count % tm == 0 and start % tm == 0
    first = start // tm
    row_in = pl.BlockSpec((tm, d), lambda i: (first + i, 0))
    row_out = pl.BlockSpec((tm, d), lambda i: (i, 0))
    vec = pl.BlockSpec((1, d), lambda i: (0, 0))
    return pl.pallas_call(
        _resid_ln_kernel,
        grid=(count // tm,),
        in_specs=[row_in, row_in, vec, vec],
        out_specs=[row_out, row_out],
        out_shape=[jax.ShapeDtypeStruct((count, d), jnp.float32), jax.ShapeDtypeStruct((count, d), jnp.bfloat16)],
        compiler_params=_cparams(("parallel",)),
        name="resid_ln",
    )(x, y, g.reshape(1, d), b.reshape(1, d))


def _layer_norm_rows(z, g, b):
    mu = jnp.mean(z, axis=-1, keepdims=True)
    zc = z - mu
    var = jnp.mean(zc * zc, axis=-1, keepdims=True)
    return zc * lax.rsqrt(var + LN_EPS) * g + b


def _concat_cast_kernel(a_ref, b_ref, o_ref, obf_ref, *, a_blocks):
    i = pl.program_id(0)

    @pl.when(i < a_blocks)
    def _():
        o_ref[...] = a_ref[...]
        obf_ref[...] = a_ref[...].astype(jnp.bfloat16)

    @pl.when(i >= a_blocks)
    def _():
        o_ref[...] = b_ref[...]
        obf_ref[...] = b_ref[...].astype(jnp.bfloat16)


def concat_cast(a, b, tm=256):
    (ta, d), tb = a.shape, b.shape[0]
    tm = min(tm, ta, tb)
    assert ta % tm == 0 and tb % tm == 0
    a_blocks = ta // tm
    row = pl.BlockSpec((tm, d), lambda i: (i, 0))
    return pl.pallas_call(
        functools.partial(_concat_cast_kernel, a_blocks=a_blocks),
        grid=((ta + tb) // tm,),
        in_specs=[pl.BlockSpec((tm, d), lambda i: (jnp.minimum(i, a_blocks - 1), 0)),
                  pl.BlockSpec((tm, d), lambda i: (jnp.maximum(i - a_blocks, 0), 0))],
        out_specs=[row, row],
        out_shape=[jax.ShapeDtypeStruct((ta + tb, d), jnp.float32), jax.ShapeDtypeStruct((ta + tb, d), jnp.bfloat16)],
        compiler_params=_cparams(("arbitrary",)),
        name="concat_cast",
    )(a, b)


def _banded_kernel(*refs, half, nk, group, step, bounds, use_sink, emit_lse, row_axis, head_axis):
    refs = list(refs)
    slopes_ref = refs.pop(0)
    sink_ref = refs.pop(0) if use_sink else None
    q_ref, kp_ref, kc_ref, kn_ref, vp_ref, vc_ref, vn_ref = refs[:7]
    o_ref = refs[7]
    lse_ref = refs[8] if emit_lse else None

    ub = pl.program_id(row_axis)
    hgi = pl.program_id(head_axis)
    s0, e0 = _seg_bounds(ub * Q_BLOCK, bounds)
    has_prev = ub * Q_BLOCK > s0
    has_next = (ub + 1) * Q_BLOCK < e0
    nsub = Q_BLOCK // Q_SUB
    win = Q_BLOCK + 2 * half
    row = lax.broadcasted_iota(jnp.int32, (Q_SUB, win), 0)
    col = lax.broadcasted_iota(jnp.int32, (Q_SUB, win), 1)
    edge_ok = ((col >= half) | has_prev) & ((col < half + Q_BLOCK) | has_next)
    valids, distfs = [], []
    for sb in range(nsub):
        dist = jnp.abs(col - half - row - sb * Q_SUB)
        valids.append((dist <= half) & edge_ok)
        distfs.append((step * dist).astype(jnp.float32))
    scale2 = HEAD_DIM ** -0.5 * LOG2E
    lane = lax.broadcasted_iota(jnp.int32, (Q_SUB, LANES), 1)
    m_tiles = [jnp.zeros((Q_SUB, LANES), jnp.float32) for _ in range(nsub)]
    l_tiles = [jnp.ones((Q_SUB, LANES), jnp.float32) for _ in range(nsub)]

    for k in range(nk):
        ksl = slice(k * HEAD_DIM, (k + 1) * HEAD_DIM)
        kwin = jnp.concatenate([kp_ref[:, ksl], kc_ref[:, ksl], kn_ref[:, ksl]], axis=0)
        vwin = jnp.concatenate([vp_ref[:, ksl], vc_ref[:, ksl], vn_ref[:, ksl]], axis=0)
        for g in range(group):
            hidx = k * group + g
            hsl = slice(hidx * HEAD_DIM, (hidx + 1) * HEAD_DIM)
            head = hgi * (nk * group) + hidx
            slope2 = slopes_ref[head] * LOG2E
            for sb in range(nsub):
                rows = slice(sb * Q_SUB, (sb + 1) * Q_SUB)
                s = lax.dot_general(q_ref[rows, hsl], kwin, _NT, preferred_element_type=jnp.float32) * scale2
                s = jnp.where(valids[sb], s - slope2 * distfs[sb], NEG_INF)
                m = jnp.max(s, axis=-1, keepdims=True)
                if use_sink:
                    sk = sink_ref[head] * LOG2E
                    m = jnp.maximum(m, sk)
                p = jnp.exp2(s - m)
                l = jnp.sum(p, axis=-1, keepdims=True)
                if use_sink:
                    l = l + jnp.exp2(sk - m)
                o = jnp.dot(p.astype(jnp.bfloat16), vwin, preferred_element_type=jnp.float32) / l
                if emit_lse:
                    m_tiles[sb] = jnp.where(lane == hidx, m, m_tiles[sb])
                    l_tiles[sb] = jnp.where(lane == hidx, l, l_tiles[sb])
                o_ref[rows, hsl] = o.astype(o_ref.dtype)
    if emit_lse:
        for sb in range(nsub):
            lse_ref[sb * Q_SUB:(sb + 1) * Q_SUB, :] = (m_tiles[sb] + jnp.log2(l_tiles[sb])) * LN2


def attention_a(qkv, slopes, sink, segs):
    t = qkv.shape[0]
    nblk = t // Q_BLOCK
    bounds = _bounds(segs, 1)
    qw, kw = A_Q, A_KV
    k_col, v_col = A_Q // kw, (A_Q + A_KV) // kw
    prev = lambda b: jnp.maximum(b - 1, 0)
    nxt = lambda b: jnp.minimum(b + 1, nblk - 1)
    smem = pl.BlockSpec(memory_space=pltpu.SMEM)
    in_specs = [smem, smem,
                pl.BlockSpec((Q_BLOCK, qw), lambda b, h: (b, 0)),
                pl.BlockSpec((Q_BLOCK, kw), lambda b, h: (prev(b), k_col)),
                pl.BlockSpec((Q_BLOCK, kw), lambda b, h: (b, k_col)),
                pl.BlockSpec((Q_BLOCK, kw), lambda b, h: (nxt(b), k_col)),
                pl.BlockSpec((Q_BLOCK, kw), lambda b, h: (prev(b), v_col)),
                pl.BlockSpec((Q_BLOCK, kw), lambda b, h: (b, v_col)),
                pl.BlockSpec((Q_BLOCK, kw), lambda b, h: (nxt(b), v_col))]
    kern = functools.partial(_banded_kernel, half=A_HALF_WINDOW, nk=A_KV_HEADS, group=A_GROUP, step=1,
                             bounds=bounds, use_sink=True, emit_lse=False, row_axis=0, head_axis=1)
    return pl.pallas_call(
        kern,
        grid=(nblk, 1),
        in_specs=in_specs,
        out_specs=pl.BlockSpec((Q_BLOCK, qw), lambda b, h: (b, 0)),
        out_shape=jax.ShapeDtypeStruct((t, qw), jnp.bfloat16),
        compiler_params=_cparams(("parallel", "arbitrary")),
        name="attention_a",
    )(slopes, sink, *([qkv] * 7))


def attention_c_branch(z, slopes, dil, segs):
    rows = z.shape[1]
    nub = rows // Q_BLOCK
    half = C_HALF
    hw = C_HEAD_GROUP * HEAD_DIM
    bounds = _bounds(segs, dil)
    k_col0, v_col0 = C_W // hw, 2 * C_W // hw
    nhalf = rows // half
    prev = lambda u: jnp.maximum(2 * u - 1, 0)
    nxt = lambda u: jnp.minimum(2 * u + 2, nhalf - 1)
    smem = pl.BlockSpec(memory_space=pltpu.SMEM)
    in_specs = [smem,
                pl.BlockSpec((None, Q_BLOCK, hw), lambda r, u, h: (r, u, h)),
                pl.BlockSpec((None, half, hw), lambda r, u, h: (r, prev(u), k_col0 + h)),
                pl.BlockSpec((None, Q_BLOCK, hw), lambda r, u, h: (r, u, k_col0 + h)),
                pl.BlockSpec((None, half, hw), lambda r, u, h: (r, nxt(u), k_col0 + h)),
                pl.BlockSpec((None, half, hw), lambda r, u, h: (r, prev(u), v_col0 + h)),
                pl.BlockSpec((None, Q_BLOCK, hw), lambda r, u, h: (r, u, v_col0 + h)),
                pl.BlockSpec((None, half, hw), lambda r, u, h: (r, nxt(u), v_col0 + h))]
    kern = functools.partial(_banded_kernel, half=half, nk=C_HEAD_GROUP, group=1, step=dil, bounds=bounds,
                             use_sink=False, emit_lse=True, row_axis=1, head_axis=2)
    return pl.pallas_call(
        kern,
        grid=(dil, nub, C_GROUPS),
        in_specs=in_specs,
        out_specs=[pl.BlockSpec((None, Q_BLOCK, hw), lambda r, u, h: (r, u, h)),
                   pl.BlockSpec((None, Q_BLOCK, LANES), lambda r, u, h: (r, u, h))],
        out_shape=[jax.ShapeDtypeStruct((dil, rows, C_W), jnp.bfloat16),
                   jax.ShapeDtypeStruct((dil, rows, C_GROUPS * LANES), jnp.float32)],
        compiler_params=_cparams(("parallel", "parallel", "arbitrary")),
        name="attention_c_d%d" % dil,
    )(slopes, *([z] * 7))


def _plane_perm(dil, tm):
    p = np.zeros((tm, tm), np.float32)
    tok = np.arange(tm)
    p[tok, (tok % dil) * (tm // dil) + tok // dil] = 1.0
    return jnp.asarray(p, jnp.bfloat16)


def _split3(x):
    hi = x.astype(jnp.bfloat16)
    r = x - hi.astype(jnp.float32)
    mid = r.astype(jnp.bfloat16)
    lo = (r - mid.astype(jnp.float32)).astype(jnp.bfloat16)
    return hi, mid, lo


def _merge_kernel(*refs, dils):
    nb = len(dils)
    o_refs, l_refs = refs[:nb], refs[nb:2 * nb]
    p_refs = refs[2 * nb:2 * nb + nb - 1]
    out_ref = refs[-1]
    outs, lses = [o_refs[0][0].astype(jnp.float32)], [l_refs[0][0]]
    for b in range(1, nb):
        d = dils[b]
        perm = p_refs[b - 1][...]
        stacked = jnp.concatenate([o_refs[b][r] for r in range(d)], axis=0)
        outs.append(jnp.dot(perm, stacked, preferred_element_type=jnp.float32))
        lstack = jnp.concatenate([l_refs[b][r] for r in range(d)], axis=0)
        lses.append(sum(jnp.dot(perm, part, preferred_element_type=jnp.float32) for part in _split3(lstack)))
    for h in range(C_HEADS):
        hsl = slice(h * HEAD_DIM, (h + 1) * HEAD_DIM)
        col = (h // C_HEAD_GROUP) * LANES + h % C_HEAD_GROUP
        ls = [l[:, col:col + 1] for l in lses]
        m = functools.reduce(jnp.maximum, ls)
        es = [jnp.exp(l - m) for l in ls]
        tot = sum(es)
        acc = sum((e / tot) * o[:, hsl] for e, o in zip(es, outs))
        out_ref[:, hsl] = acc.astype(out_ref.dtype)


def merge_branches(os_, ls_, dils):
    t = os_[0].shape[0] * os_[0].shape[1]
    tm = min(MERGE_TILE, t)
    in_specs, args = [], []
    for arr, width in ((os_, C_W), (ls_, C_GROUPS * LANES)):
        for a, d in zip(arr, dils):
            in_specs.append(pl.BlockSpec((d, tm // d, width), lambda i: (0, i, 0)))
            args.append(a)
    for d in dils[1:]:
        in_specs.append(pl.BlockSpec((tm, tm), lambda i: (0, 0)))
        args.append(_plane_perm(d, tm))
    return pl.pallas_call(
        functools.partial(_merge_kernel, dils=tuple(dils)),
        grid=(t // tm,),
        in_specs=in_specs,
        out_specs=pl.BlockSpec((tm, C_W), lambda i: (i, 0)),
        out_shape=jax.ShapeDtypeStruct((t, C_W), jnp.bfloat16),
        compiler_params=_cparams(("parallel",)),
        name="merge_branches",
    )(*args)


def attention_c(zs, slopes, segs):
    dils = tuple(d for _, d in C_BRANCHES)
    res = [attention_c_branch(z, slopes, d, segs) for z, d in zip(zs, dils)]
    return merge_branches([r[0] for r in res], [r[1] for r in res], dils)


NA_PAIR = 2
NA_WIN_ROWS = NA_KH + NA_PAIR
NA_WIN_BLOCKS = NA_WIN_ROWS // NA_PAIR
NA_PAIR_CLASSES = ((0, 3, 1, 3), (0, 7, 0, 6), (0, 5, 0, 4), (2, 3, 2, 2), (2, 1, 2, 0))


def _natten_kernel(*refs):
    q_ref = refs[0]
    k_refs = refs[1:1 + NA_WIN_BLOCKS]
    v_refs = refs[1 + NA_WIN_BLOCKS:1 + 2 * NA_WIN_BLOCKS]
    bias_ref, o_ref = refs[1 + 2 * NA_WIN_BLOCKS:]
    scale2 = HEAD_DIM ** -0.5 * LOG2E
    for h in range(B_HEAD_GROUP):
        hsl = slice(h * HEAD_DIM, (h + 1) * HEAD_DIM)
        kwin = jnp.concatenate([r[:, hsl] for r in k_refs], axis=0)
        vwin = jnp.concatenate([r[:, hsl] for r in v_refs], axis=0)
        s = lax.dot_general(q_ref[:, hsl], kwin, _NT, preferred_element_type=jnp.float32) * scale2 + bias_ref[h]
        m = jnp.max(s, axis=-1, keepdims=True)
        p = jnp.exp2(s - m)
        l = jnp.sum(p, axis=-1, keepdims=True)
        o = jnp.dot(p.astype(jnp.bfloat16), vwin, preferred_element_type=jnp.float32) / l
        o_ref[:, hsl] = o.astype(o_ref.dtype)


def natten_bias_table(rpb):
    ncls = len(NA_PAIR_CLASSES)
    sel = np.zeros((ncls, NA_PAIR, NA_WIN_ROWS, 2 * NA_KH - 1), np.float32)
    row_ok = np.zeros((ncls, NA_PAIR, NA_WIN_ROWS), bool)
    for ci, cls in enumerate(NA_PAIR_CLASSES):
        for a in range(NA_PAIR):
            off, o = cls[2 * a], cls[2 * a + 1]
            for i in range(NA_KH):
                sel[ci, a, off + i, o + i] = 1.0
                row_ok[ci, a, off + i] = True
    c = jnp.arange(GRID_W)
    cs = jnp.clip(c - NA_KW // 2, 0, GRID_W - NA_KW)
    col_ok = (c[None, :] >= cs[:, None]) & (c[None, :] < cs[:, None] + NA_KW)
    dc = jnp.clip(c[None, :] - c[:, None] + (NA_KW - 1), 0, 2 * NA_KW - 2)
    pick = (dc[:, :, None] == jnp.arange(2 * NA_KW - 1)[None, None, :]).astype(jnp.float32)
    hi = lax.Precision.HIGHEST
    rows = jnp.einsum("capd,hdx->caphx", jnp.asarray(sel), rpb.astype(jnp.float32), precision=hi)
    tab = jnp.einsum("caphx,qkx->chaqpk", rows, pick, precision=hi)
    ok = jnp.asarray(row_ok)[:, None, :, None, :, None] & col_ok[None, None, None, :, None, :]
    tab = jnp.where(ok, tab * LOG2E, NEG_INF)
    return tab.reshape(ncls, B_HEADS, NA_PAIR * GRID_W, NA_WIN_ROWS * GRID_W)


def attention_b(qkv, bias_tab, segs):
    t = qkv.shape[0]
    nrows = t // GRID_W
    bounds = _bounds(segs, GRID_W)
    hw = B_HEAD_GROUP * HEAD_DIM
    ngroups = B_HEADS // B_HEAD_GROUP
    q_col0 = (A_Q + 2 * A_KV) // hw
    k_col0 = q_col0 + B_W // hw
    v_col0 = k_col0 + B_W // hw

    assert all(re - rs >= 2 * NA_KH and (re - rs) % NA_PAIR == 0 for rs, re in bounds)
    blk = NA_PAIR * GRID_W

    def win_block(pr):
        rs, re = _seg_bounds(pr * NA_PAIR, bounds)
        return jnp.clip(pr * NA_PAIR - NA_KH // 2, rs, re - NA_WIN_ROWS) // NA_PAIR

    def pair_class(pr):
        rs, re = _seg_bounds(pr * NA_PAIR, bounds)
        from_start, to_end = pr * NA_PAIR - rs, re - pr * NA_PAIR
        return jnp.where(from_start == 0, 1, jnp.where(from_start == 2, 2,
                         jnp.where(to_end == 4, 3, jnp.where(to_end == 2, 4, 0))))

    def kv_spec(col0, i):
        return pl.BlockSpec((blk, hw), lambda h, pr: (win_block(pr) + i, col0 + h))

    in_specs = [pl.BlockSpec((blk, hw), lambda h, pr: (pr, q_col0 + h))]
    in_specs += [kv_spec(k_col0, i) for i in range(NA_WIN_BLOCKS)]
    in_specs += [kv_spec(v_col0, i) for i in range(NA_WIN_BLOCKS)]
    in_specs.append(pl.BlockSpec((None, B_HEAD_GROUP, blk, NA_WIN_ROWS * GRID_W),
                                 lambda h, pr: (pair_class(pr), h, 0, 0)))
    return pl.pallas_call(
        _natten_kernel,
        grid=(ngroups, nrows // NA_PAIR),
        in_specs=in_specs,
        out_specs=pl.BlockSpec((blk, hw), lambda h, pr: (pr, h)),
        out_shape=jax.ShapeDtypeStruct((t, B_W), jnp.bfloat16),
        compiler_params=_cparams(("parallel", "arbitrary")),
        name="attention_b",
    )(*([qkv] * (1 + 2 * NA_WIN_BLOCKS)), bias_tab)


def _top2(vals):
    v1, i1 = vals[0], jnp.zeros(vals[0].shape, jnp.int32)
    for j in range(1, len(vals)):
        better = vals[j] > v1
        v1 = jnp.where(better, vals[j], v1)
        i1 = jnp.where(better, j, i1)
    v2, i2 = None, None
    for j in range(len(vals)):
        cand = jnp.where(i1 == j, -1.0, vals[j])
        if v2 is None:
            v2, i2 = cand, jnp.zeros(vals[0].shape, jnp.int32)
        else:
            better = cand > v2
            v2 = jnp.where(better, cand, v2)
            i2 = jnp.where(better, j, i2)
    return v1, i1, v2, i2


def _router_kernel(x_ref, w2_ref, whi_ref, b_ref, idx_ref, gate_ref):
    x = x_ref[...]
    x_hi = x.astype(jnp.bfloat16)
    x_lo = (x - x_hi.astype(jnp.float32)).astype(jnp.bfloat16)
    r1 = lax.dot_general(w2_ref[...], x_hi, _NT, preferred_element_type=jnp.float32)
    r2 = lax.dot_general(whi_ref[...], x_lo, _NT, preferred_element_type=jnp.float32)
    logits = r1[:N_EXPERTS] + r1[N_EXPERTS:] + r2 + b_ref[...]
    m = jnp.max(logits, axis=0, keepdims=True)
    ex = jnp.exp(logits - m)
    probs = ex / jnp.sum(ex, axis=0, keepdims=True)
    rows = [probs[e:e + 1, :] for e in range(N_EXPERTS)]
    tops = [_top2(rows[g * EXPERTS_PER_GROUP:(g + 1) * EXPERTS_PER_GROUP]) for g in range(N_GROUPS)]
    score = [tp[0] + tp[2] for tp in tops]
    best, gsel = score[0], jnp.zeros(score[0].shape, jnp.int32)
    for g in range(1, N_GROUPS):
        better = score[g] > best
        best = jnp.where(better, score[g], best)
        gsel = jnp.where(better, g, gsel)
    v1, i1, v2, i2 = tops[0]
    for g in range(1, N_GROUPS):
        pick = gsel == g
        v1 = jnp.where(pick, tops[g][0], v1)
        i1 = jnp.where(pick, tops[g][1], i1)
        v2 = jnp.where(pick, tops[g][2], v2)
        i2 = jnp.where(pick, tops[g][3], i2)
    tot = v1 + v2
    idx_ref[0:1, :] = gsel * EXPERTS_PER_GROUP + i1
    idx_ref[1:2, :] = gsel * EXPERTS_PER_GROUP + i2
    gate_ref[0:1, :] = v1 / tot
    gate_ref[1:2, :] = v2 / tot


def router(x, router_w, router_b, tm=512):
    t, d = x.shape
    tm = min(tm, t)
    wt = router_w.astype(jnp.float32).T
    w_hi = wt.astype(jnp.bfloat16)
    w_lo = (wt - w_hi.astype(jnp.float32)).astype(jnp.bfloat16)
    w2 = jnp.concatenate([w_hi, w_lo], axis=0)
    return pl.pallas_call(
        _router_kernel,
        grid=(t // tm,),
        in_specs=[pl.BlockSpec((tm, d), lambda i: (i, 0)),
                  pl.BlockSpec((2 * N_EXPERTS, d), lambda i: (0, 0)),
                  pl.BlockSpec((N_EXPERTS, d), lambda i: (0, 0)),
                  pl.BlockSpec((N_EXPERTS, 1), lambda i: (0, 0))],
        out_specs=[pl.BlockSpec((2, tm), lambda i: (0, i))] * 2,
        out_shape=[jax.ShapeDtypeStruct((2, t), jnp.int32), jax.ShapeDtypeStruct((2, t), jnp.float32)],
        compiler_params=_cparams(("parallel",)),
        name="router",
    )(x, w2, w_hi, router_b.astype(jnp.float32).reshape(N_EXPERTS, 1))


def dispatch_plan(idx):
    _, t = idx.shape
    nt = t // MOE_TILE
    n_chunks = _round_up(nt * CHUNKS_PER_TILE + N_EXPERTS * (CHUNKS_PER_EXPERT_TILE - 1), CHUNKS_PER_EXPERT_TILE)
    i32 = jnp.int32
    e_tile = idx.reshape(2, nt, MOE_TILE).transpose(1, 0, 2).reshape(nt, 2 * MOE_TILE)
    hit = e_tile[:, :, None] == jnp.arange(N_EXPERTS, dtype=i32)[None, None, :]
    onehot = hit.astype(i32)
    tri = (jnp.arange(2 * MOE_TILE)[:, None] >= jnp.arange(2 * MOE_TILE)[None, :]).astype(jnp.bfloat16)
    csum = jnp.einsum("ab,nbe->nae", tri, hit.astype(jnp.bfloat16), preferred_element_type=jnp.float32).astype(i32)
    rank = jnp.sum(csum * onehot, axis=-1) - 1
    cnt = csum[:, -1, :]
    nc = (cnt + SLOT_CHUNK - 1) // SLOT_CHUNK
    lo = jnp.cumsum(nc, axis=1) - nc
    local_pos = jnp.sum(onehot * (lo * SLOT_CHUNK)[:, None, :], axis=-1) + rank
    pos = local_pos.reshape(nt, 2, MOE_TILE).transpose(1, 0, 2).reshape(2, t).astype(i32)

    nc_t = nc.T
    e_chunks = nc_t.sum(axis=1)
    e_pad = _round_up(e_chunks, CHUNKS_PER_EXPERT_TILE)
    e_end = jnp.cumsum(e_pad)
    run_start = (e_end - e_pad)[:, None] + jnp.cumsum(nc_t, axis=1) - nc_t
    src0 = jnp.arange(nt, dtype=i32)[None, :] * CHUNKS_PER_TILE + lo.T
    rs_f, nc_f, src0_f = run_start.reshape(1, -1), nc_t.reshape(1, -1), src0.reshape(1, -1)
    d = jnp.arange(n_chunks, dtype=i32)[:, None]
    in_run = (d >= rs_f) & (d < rs_f + nc_f)
    fwd = jnp.sum(jnp.where(in_run, src0_f + d - rs_f, 0), axis=1).astype(i32)

    lc = jnp.arange(CHUNKS_PER_TILE, dtype=i32)[None, :, None]
    lo_b, nc_b = lo[:, None, :], nc[:, None, :]
    in_run_b = (lc >= lo_b) & (lc < lo_b + nc_b)
    back = jnp.sum(jnp.where(in_run_b, run_start.T[:, None, :] + lc - lo_b, 0), axis=-1)
    back = back.reshape(nt * CHUNKS_PER_TILE).astype(i32)

    n_tiles = n_chunks // CHUNKS_PER_EXPERT_TILE
    tile_first = jnp.arange(n_tiles, dtype=i32) * CHUNKS_PER_EXPERT_TILE
    tile_expert = jnp.minimum(jnp.sum((e_end[None, :] <= tile_first[:, None]).astype(i32), axis=1),
                              N_EXPERTS - 1).astype(i32)
    n_active = (e_end[-1] // CHUNKS_PER_EXPERT_TILE).astype(i32).reshape(1)
    return pos, fwd, back, tile_expert, n_active


def _permute_kernel(x_ref, pos_ref, gate_ref, xs_ref, gl_ref):
    tt, d = x_ref.shape
    slot = lax.broadcasted_iota(jnp.int32, (LOCAL_SLOTS, tt), 0)
    hit1 = slot == pos_ref[0:1, :]
    hit2 = slot == pos_ref[1:2, :]
    perm = jnp.where(hit1 | hit2, 1.0, 0.0).astype(jnp.bfloat16)
    cw = min(d, 512)
    for n in range(d // cw):
        sl = slice(n * cw, (n + 1) * cw)
        xs_ref[:, sl] = jnp.dot(perm, x_ref[:, sl], preferred_element_type=jnp.float32).astype(xs_ref.dtype)
    gates = jnp.where(hit1, gate_ref[0:1, :], 0.0) + jnp.where(hit2, gate_ref[1:2, :], 0.0)
    gl_ref[...] = jnp.broadcast_to(jnp.sum(gates, axis=1, keepdims=True), gl_ref.shape)


def permute_tokens(x_bf, pos, gate):
    t, d = x_bf.shape
    nt = t // MOE_TILE
    return pl.pallas_call(
        _permute_kernel,
        grid=(nt,),
        in_specs=[pl.BlockSpec((MOE_TILE, d), lambda i: (i, 0)),
                  pl.BlockSpec((2, MOE_TILE), lambda i: (0, i)),
                  pl.BlockSpec((2, MOE_TILE), lambda i: (0, i))],
        out_specs=[pl.BlockSpec((LOCAL_SLOTS, d), lambda i: (i, 0)),
                   pl.BlockSpec((LOCAL_SLOTS, LANES), lambda i: (i, 0))],
        out_shape=[jax.ShapeDtypeStruct((nt * LOCAL_SLOTS, d), jnp.bfloat16),
                   jax.ShapeDtypeStruct((nt * LOCAL_SLOTS, LANES), jnp.float32)],
        compiler_params=_cparams(("parallel",)),
        name="permute_tokens",
    )(x_bf, pos, gate)


def _unpermute_kernel(back_ref, ys_ref, gl_ref, p1_ref, p2_ref, y_ref, buf_ref, sems):
    i = pl.program_id(0)
    tt, d = y_ref.shape
    n = CHUNKS_PER_TILE
    buf_slot = i % 2

    @pl.when(i == 0)
    def _():
        _fetch_chunks(ys_ref, back_ref, 0, n, buf_ref, 0, sems)

    @pl.when(i + 1 < pl.num_programs(0))
    def _():
        _fetch_chunks(ys_ref, back_ref, (i + 1) * n, n, buf_ref, 1 - buf_slot, sems)

    slot = lax.broadcasted_iota(jnp.int32, (tt, LOCAL_SLOTS), 1)
    hit = (slot == p1_ref[:, 0:1]) | (slot == p2_ref[:, 0:1])
    perm = jnp.where(hit, 1.0, 0.0).astype(jnp.bfloat16)
    g = gl_ref[:, 0:1]
    _wait_chunks(ys_ref, n, buf_ref, buf_slot, sems)
    cw = min(d, 512)
    for c in range(d // cw):
        sl = slice(c * cw, (c + 1) * cw)
        scaled = (buf_ref[buf_slot, :, sl].astype(jnp.float32) * g).astype(jnp.bfloat16)
        y_ref[:, sl] = jnp.dot(perm, scaled, preferred_element_type=jnp.float32)


UNPERMUTE_SPLIT = 2


def _unpermute_ln_kernel(back_ref, ys_ref, gl_ref, p1_ref, p2_ref, x_ref, g_ref, b_ref, o_ref, obf_ref,
                         buf_ref, sems):
    i, part = pl.program_id(0), pl.program_id(1)
    rows, d = o_ref.shape
    n = CHUNKS_PER_TILE
    buf_slot = i % 2
    cw = min(d, 512)

    @pl.when((i == 0) & (part == 0))
    def _():
        _fetch_chunks(ys_ref, back_ref, 0, n, buf_ref, 0, sems)

    @pl.when((part == 0) & (i + 1 < pl.num_programs(0)))
    def _():
        _fetch_chunks(ys_ref, back_ref, (i + 1) * n, n, buf_ref, 1 - buf_slot, sems)

    @pl.when(part == 0)
    def _():
        _wait_chunks(ys_ref, n, buf_ref, buf_slot, sems)
        gate = gl_ref[:, 0:1]
        for c in range(d // cw):
            sl = slice(c * cw, (c + 1) * cw)
            buf_ref[buf_slot, :, sl] = (buf_ref[buf_slot, :, sl].astype(jnp.float32) * gate).astype(buf_ref.dtype)

    slot = lax.broadcasted_iota(jnp.int32, (rows, LOCAL_SLOTS), 1)
    hit = (slot == p1_ref[:, 0:1]) | (slot == p2_ref[:, 0:1])
    perm = jnp.where(hit, 1.0, 0.0).astype(jnp.bfloat16)
    for c in range(d // cw):
        sl = slice(c * cw, (c + 1) * cw)
        o_ref[:, sl] = jnp.dot(perm, buf_ref[buf_slot, :, sl], preferred_element_type=jnp.float32)
    out = _layer_norm_rows(DEEPNORM_ALPHA * x_ref[...] + o_ref[...], g_ref[...], b_ref[...])
    o_ref[...] = out
    obf_ref[...] = out.astype(jnp.bfloat16)


def unpermute_resid_ln(ys, back, gate_local, pos, x, g, b):
    t, d = x.shape
    nt = t // MOE_TILE
    rows = MOE_TILE // UNPERMUTE_SPLIT
    p1 = jnp.broadcast_to(pos[0][:, None], (t, LANES))
    p2 = jnp.broadcast_to(pos[1][:, None], (t, LANES))
    row = lambda width: pl.BlockSpec((rows, width), lambda i, p, bk: (i * UNPERMUTE_SPLIT + p, 0))
    vec = pl.BlockSpec((1, d), lambda i, p, bk: (0, 0))
    return pl.pallas_call(
        _unpermute_ln_kernel,
        grid_spec=pltpu.PrefetchScalarGridSpec(
            num_scalar_prefetch=1,
            grid=(nt, UNPERMUTE_SPLIT),
            in_specs=[pl.BlockSpec(memory_space=pl.ANY),
                      pl.BlockSpec((LOCAL_SLOTS, LANES), lambda i, p, bk: (i, 0)),
                      row(LANES), row(LANES), row(d), vec, vec],
            out_specs=[row(d), row(d)],
            scratch_shapes=[pltpu.VMEM((2, LOCAL_SLOTS, d), ys.dtype), pltpu.SemaphoreType.DMA((2,))]),
        out_shape=[jax.ShapeDtypeStruct((t, d), jnp.float32), jax.ShapeDtypeStruct((t, d), jnp.bfloat16)],
        compiler_params=_cparams(("arbitrary", "arbitrary")),
        name="unpermute_resid_ln",
    )(back, ys.reshape(ys.shape[0] // SLOT_CHUNK, SLOT_CHUNK, d), gate_local, p1, p2, x,
      g.reshape(1, d), b.reshape(1, d))


def unpermute_tokens(ys, back, gate_local, pos):
    _, t = pos.shape
    d = ys.shape[1]
    nt = t // MOE_TILE
    p1 = jnp.broadcast_to(pos[0][:, None], (t, LANES))
    p2 = jnp.broadcast_to(pos[1][:, None], (t, LANES))
    return pl.pallas_call(
        _unpermute_kernel,
        grid_spec=pltpu.PrefetchScalarGridSpec(
            num_scalar_prefetch=1,
            grid=(nt,),
            in_specs=[pl.BlockSpec(memory_space=pl.ANY),
                      pl.BlockSpec((LOCAL_SLOTS, LANES), lambda i, bk: (i, 0)),
                      pl.BlockSpec((MOE_TILE, LANES), lambda i, bk: (i, 0)),
                      pl.BlockSpec((MOE_TILE, LANES), lambda i, bk: (i, 0))],
            out_specs=pl.BlockSpec((MOE_TILE, d), lambda i, bk: (i, 0)),
            scratch_shapes=[pltpu.VMEM((2, LOCAL_SLOTS, d), ys.dtype), pltpu.SemaphoreType.DMA((2,))]),
        out_shape=jax.ShapeDtypeStruct((t, d), jnp.float32),
        compiler_params=_cparams(("arbitrary",)),
        name="unpermute_tokens",
    )(back, ys.reshape(ys.shape[0] // SLOT_CHUNK, SLOT_CHUNK, d), gate_local, p1, p2)


def _chunk_copy(src_ref, chunk, buf_ref, slot, g, sems):
    return pltpu.make_async_copy(src_ref.at[chunk], buf_ref.at[slot, pl.ds(g * SLOT_CHUNK, SLOT_CHUNK)],
                                 sems.at[slot])


def _fetch_chunks(src_ref, idx_ref, first, n, buf_ref, slot, sems):
    for g in range(n):
        _chunk_copy(src_ref, idx_ref[first + g], buf_ref, slot, g, sems).start()


def _wait_chunks(src_ref, n, buf_ref, slot, sems):
    for g in range(n):
        _chunk_copy(src_ref, 0, buf_ref, slot, g, sems).wait()


def _expert_up_kernel(te_ref, na_ref, fwd_ref, xs1_ref, w1_ref, w3_ref, h_ref, buf_ref, sems):
    j, i = pl.program_id(0), pl.program_id(1)
    na = na_ref[0]
    active = i < na

    @pl.when(active)
    def _():
        a = j * na + i
        slot = a % 2
        n = CHUNKS_PER_EXPERT_TILE

        @pl.when(a == 0)
        def _():
            _fetch_chunks(xs1_ref, fwd_ref, 0, n, buf_ref, 0, sems)

        @pl.when(a + 1 < pl.num_programs(0) * na)
        def _():
            nxt = jnp.where(i + 1 < na, i + 1, 0)
            _fetch_chunks(xs1_ref, fwd_ref, nxt * n, n, buf_ref, 1 - slot, sems)

        _wait_chunks(xs1_ref, n, buf_ref, slot, sems)
        x = buf_ref[slot]
        h1 = jnp.dot(x, w1_ref[...].astype(jnp.bfloat16), preferred_element_type=jnp.float32)
        h3 = jnp.dot(x, w3_ref[...].astype(jnp.bfloat16), preferred_element_type=jnp.float32)
        h_ref[...] = (jax.nn.silu(h1) * h3).astype(h_ref.dtype)

    @pl.when(jnp.logical_not(active))
    def _():
        h_ref[...] = jnp.zeros(h_ref.shape, h_ref.dtype)


def _expert_down_kernel(te_ref, na_ref, h_ref, w2_ref, y_ref):
    active = pl.program_id(1) < na_ref[0]

    @pl.when(active)
    def _():
        y_ref[...] = jnp.dot(h_ref[...], w2_ref[...].astype(jnp.bfloat16),
                             preferred_element_type=jnp.float32).astype(y_ref.dtype)

    @pl.when(jnp.logical_not(active))
    def _():
        y_ref[...] = jnp.zeros(y_ref.shape, y_ref.dtype)


def expert_ffn(xs1, fwd, w1, w3, w2, layer, tile_expert, n_active, tf=512, tn=4096):
    d = xs1.shape[1]
    f = w1.shape[3]
    tm = EXPERT_TILE
    tf, tn = min(tf, f), min(tn, d)
    n_tiles = fwd.shape[0] // CHUNKS_PER_EXPERT_TILE
    s = n_tiles * tm

    def act(i, na):
        return jnp.minimum(i, na[0] - 1)

    up = pl.pallas_call(
        _expert_up_kernel,
        grid_spec=pltpu.PrefetchScalarGridSpec(
            num_scalar_prefetch=3,
            grid=(f // tf, n_tiles),
            in_specs=[pl.BlockSpec(memory_space=pl.ANY),
                      pl.BlockSpec((None, None, d, tf), lambda j, i, te, na, fw: (layer, te[act(i, na)], 0, j)),
                      pl.BlockSpec((None, None, d, tf), lambda j, i, te, na, fw: (layer, te[act(i, na)], 0, j))],
            out_specs=pl.BlockSpec((tm, tf), lambda j, i, te, na, fw: (i, j)),
            scratch_shapes=[pltpu.VMEM((2, tm, d), xs1.dtype), pltpu.SemaphoreType.DMA((2,))]),
        out_shape=jax.ShapeDtypeStruct((s, f), jnp.bfloat16),
        compiler_params=_cparams(("arbitrary", "arbitrary")),
        name="expert_up",
    )
    h = up(tile_expert, n_active, fwd, xs1.reshape(xs1.shape[0] // SLOT_CHUNK, SLOT_CHUNK, d), w1, w3)
    down = pl.pallas_call(
        _expert_down_kernel,
        grid_spec=pltpu.PrefetchScalarGridSpec(
            num_scalar_prefetch=2,
            grid=(d // tn, n_tiles),
            in_specs=[pl.BlockSpec((tm, f), lambda j, i, te, na: (act(i, na), 0)),
                      pl.BlockSpec((None, None, f, tn), lambda j, i, te, na: (layer, te[act(i, na)], 0, j))],
            out_specs=pl.BlockSpec((tm, tn), lambda j, i, te, na: (i, j))),
        out_shape=jax.ShapeDtypeStruct((s, d), jnp.bfloat16),
        compiler_params=_cparams(("arbitrary", "arbitrary")),
        name="expert_down",
    )
    return down(tile_expert, n_active, h, w2)


def moe_layer(x, x_bf, router_w, router_b, w1, w3, w2, layer, g, b, out_rows=None):
    idx, gate = router(x, router_w, router_b)
    pos, fwd, back, tile_expert, n_active = dispatch_plan(idx)
    xs1, gate_local = permute_tokens(x_bf, pos, gate)
    ys = expert_ffn(xs1, fwd, w1, w3, w2, layer, tile_expert, n_active)
    if out_rows is None:
        return unpermute_resid_ln(ys, back, gate_local, pos, x, g, b)
    y = unpermute_tokens(ys, back, gate_local, pos)
    return [resid_ln(x, y, g, b, rows=r)[0] for r in out_rows]


def alibi_slopes(n):
    return jnp.exp2(-8.0 * jnp.arange(1, n + 1, dtype=jnp.float32) / n)


def trunk(x, x_bf, segs, out_rows, w_in_ab, w_out_ab, a_sink, b_rpb, w_in_c, w_out_c, router_w, router_b,
          moe_w1, moe_w3, moe_w2, ln_g, ln_b):
    bf = jnp.bfloat16
    slopes_a = alibi_slopes(A_HEADS)
    slopes_c = alibi_slopes(C_HEADS)
    dils = tuple(d for _, d in C_BRANCHES)
    for layer in range(DEPTH):
        i = layer // 2
        if layer % 2 == 0:
            qkv = matmul([x_bf], w_in_ab[i].astype(bf), bf)
            oa = attention_a(qkv, slopes_a, a_sink[i].astype(jnp.float32), segs)
            ob = attention_b(qkv, natten_bias_table(b_rpb[i]), segs)
            mixed, w_out = [oa, ob], w_out_ab[i]
        else:
            zs = matmul_dilated(x_bf, w_in_c[i].astype(bf), dils[1:])
            zs = [zs[0].reshape((1,) + zs[0].shape)] + list(zs[1:])
            mixed, w_out = [attention_c(zs, slopes_c, segs)], w_out_c[i]
        mix = matmul(mixed, w_out.astype(bf), jnp.float32)
        x, x_bf = resid_ln(x, mix, ln_g[layer, 0], ln_b[layer, 0])
        last = layer == DEPTH - 1
        res = moe_layer(x, x_bf, router_w, router_b, moe_w1, moe_w3, moe_w2, layer,
                        ln_g[layer, 1], ln_b[layer, 1], out_rows=out_rows if last else None)
        if last:
            return res
        x, x_bf = res


def kernel(x_prompt, x_sample, w_in_ab, w_out_ab, a_sink, b_rpb, w_in_c, w_out_c, router_w, router_b,
           moe_w1, moe_w3, moe_w2, ln_g, ln_b):
    bp, sp, d = x_prompt.shape
    bs, ss, _ = x_sample.shape
    segs = ((bp, sp), (bs, ss))
    x, x_bf = concat_cast(x_prompt.reshape(bp * sp, d), x_sample.reshape(bs * ss, d))
    out_rows = ((0, bp * sp), (bp * sp, bs * ss))
    yp, ys = trunk(x, x_bf, segs, out_rows, w_in_ab, w_out_ab, a_sink, b_rpb, w_in_c, w_out_c, router_w, router_b,
                   moe_w1, moe_w3, moe_w2, ln_g, ln_b)
    return (yp.reshape(bp, sp, d), ys.reshape(bs, ss, d))
```

```python
import functools

import numpy as np
import jax
import jax.numpy as jnp
from jax import lax
from jax.experimental import pallas as pl
from jax.experimental.pallas import tpu as pltpu

HEAD_DIM = 128
GRID_W = 64
A_HEADS = 16
A_KV_HEADS = 4
A_GROUP = A_HEADS // A_KV_HEADS
A_HALF_WINDOW = 128
B_HEADS = 16
NA_KH = 8
NA_KW = 16
C_HEADS = 32
C_BRANCHES = ((128, 1), (512, 4), (2048, 16))
C_HALF = 64
N_EXPERTS = 16
N_GROUPS = 4
EXPERTS_PER_GROUP = N_EXPERTS // N_GROUPS
DEPTH = 4
DEEPNORM_ALPHA = (2 * DEPTH) ** 0.25
LN_EPS = 1e-5
NEG_INF = -1e30
LOG2E = 1.4426950408889634
LN2 = 0.6931471805599453

A_Q = A_HEADS * HEAD_DIM
A_KV = A_KV_HEADS * HEAD_DIM
B_W = B_HEADS * HEAD_DIM
C_W = C_HEADS * HEAD_DIM

LANES = 128
BF16_SUBLANES = 16
Q_BLOCK = 128
Q_SUB = 128
C_HEAD_GROUP = 16
C_GROUPS = C_HEADS // C_HEAD_GROUP
B_HEAD_GROUP = 8
B_COL_GROUPS = B_HEADS // B_HEAD_GROUP
MERGE_TILE = 256
MOE_TILE = 512
SLOT_CHUNK = BF16_SUBLANES
LOCAL_SLOTS = 2 * MOE_TILE + N_EXPERTS * SLOT_CHUNK
CHUNKS_PER_TILE = LOCAL_SLOTS // SLOT_CHUNK
EXPERT_TILE = 512
CHUNKS_PER_EXPERT_TILE = EXPERT_TILE // SLOT_CHUNK
VMEM_LIMIT = 56 * 1024 * 1024

_NT = (((1,), (1,)), ((), ()))


def _cparams(sem):
    return pltpu.CompilerParams(dimension_semantics=sem, vmem_limit_bytes=VMEM_LIMIT)


def _round_up(x, m):
    return (x + m - 1) // m * m


def _seg_bounds(pos, bounds):
    s = jnp.int32(bounds[0][0])
    e = jnp.int32(bounds[0][1])
    for a, b in bounds[1:]:
        inside = pos >= a
        s = jnp.where(inside, jnp.int32(a), s)
        e = jnp.where(inside, jnp.int32(b), e)
    return s, e


def _bounds(segs, unit):
    out, pos = [], 0
    for n, length in segs:
        for _ in range(n):
            out.append((pos // unit, (pos + length) // unit))
            pos += length
    return tuple(out)


def _matmul_kernel(*refs, widths):
    xs, w_ref, o_ref = refs[:len(widths)], refs[len(widths)], refs[len(widths) + 1]
    acc, off = None, 0
    for x_ref, k in zip(xs, widths):
        part = jnp.dot(x_ref[...], w_ref[off:off + k, :], preferred_element_type=jnp.float32)
        acc = part if acc is None else acc + part
        off += k
    o_ref[...] = acc.astype(o_ref.dtype)


def matmul(xs, w, out_dtype, tm=1024, tn=1024):
    t = xs[0].shape[0]
    k, n = w.shape
    widths = tuple(x.shape[1] for x in xs)
    assert sum(widths) == k
    tm, tn = min(tm, t), min(tn, n)
    assert t % tm == 0 and n % tn == 0
    in_specs = [pl.BlockSpec((tm, kw), lambda i, j: (i, 0)) for kw in widths]
    in_specs.append(pl.BlockSpec((k, tn), lambda i, j: (0, j)))
    return pl.pallas_call(
        functools.partial(_matmul_kernel, widths=widths),
        grid=(t // tm, n // tn),
        in_specs=in_specs,
        out_specs=pl.BlockSpec((tm, tn), lambda i, j: (i, j)),
        out_shape=jax.ShapeDtypeStruct((t, n), out_dtype),
        compiler_params=_cparams(("parallel", "arbitrary")),
        name="matmul",
    )(*xs, w)


def _matmul_dilated_kernel(x_ref, w_ref, o_ref, *rest, dils):
    z_refs, acc_ref = rest[:-1], rest[-1]
    tm, tn = o_ref.shape
    acc = jnp.dot(x_ref[...], w_ref[...], preferred_element_type=jnp.float32)
    o_ref[...] = acc.astype(o_ref.dtype)
    for s in range(tn // LANES):
        acc_ref[s] = acc[:, s * LANES:(s + 1) * LANES]
    for z_ref, d in zip(z_refs, dils):
        for rho in range(d):
            for s in range(tn // LANES):
                z_ref[rho, :, s * LANES:(s + 1) * LANES] = (
                    acc_ref[s, pl.ds(rho, tm // d, stride=d), :].astype(z_ref.dtype))


def matmul_dilated(x, w, dils, tm=1024, tn=512):
    t, k = x.shape
    n = w.shape[1]
    tm, tn = min(tm, t), min(tn, n)
    assert t % tm == 0 and n % tn == 0 and all(tm % (d * BF16_SUBLANES) == 0 for d in dils)
    out_specs = [pl.BlockSpec((tm, tn), lambda i, j: (i, j))]
    out_shape = [jax.ShapeDtypeStruct((t, n), jnp.bfloat16)]
    for d in dils:
        out_specs.append(pl.BlockSpec((d, tm // d, tn), lambda i, j: (0, i, j)))
        out_shape.append(jax.ShapeDtypeStruct((d, t // d, n), jnp.bfloat16))
    return pl.pallas_call(
        functools.partial(_matmul_dilated_kernel, dils=tuple(dils)),
        grid=(t // tm, n // tn),
        in_specs=[pl.BlockSpec((tm, k), lambda i, j: (i, 0)), pl.BlockSpec((k, tn), lambda i, j: (0, j))],
        out_specs=out_specs,
        out_shape=out_shape,
        scratch_shapes=[pltpu.VMEM((tn // LANES, tm, LANES), jnp.float32)],
        compiler_params=_cparams(("parallel", "arbitrary")),
        name="matmul_dilated",
    )(x, w)


def _resid_ln_kernel(x_ref, y_ref, g_ref, b_ref, o_ref, obf_ref):
    out = _layer_norm_rows(DEEPNORM_ALPHA * x_ref[...] + y_ref[...], g_ref[...], b_ref[...])
    o_ref[...] = out
    obf_ref[...] = out.astype(jnp.bfloat16)


def resid_ln(x, y, g, b, rows=None, tm=256):
    t, d = x.shape
    start, count = (0, t) if rows is None else rows
    tm = min(tm, count)
    assert count % tm == 0 and start % tm == 0
    first = start // tm
    row_in = pl.BlockSpec((tm, d), lambda i: (first + i, 0))
    row_out = pl.BlockSpec((tm, d), lambda i: (i, 0))
    vec = pl.BlockSpec((1, d), lambda i: (0, 0))
    return pl.pallas_call(
        _resid_ln_kernel,
        grid=(count // tm,),
        in_specs=[row_in, row_in, vec, vec],
        out_specs=[row_out, row_out],
        out_shape=[jax.ShapeDtypeStruct((count, d), jnp.float32), jax.ShapeDtypeStruct((count, d), jnp.bfloat16)],
        compiler_params=_cparams(("parallel",)),
        name="resid_ln",
    )(x, y, g.reshape(1, d), b.reshape(1, d))


def _layer_norm_rows(z, g, b):
    mu = jnp.mean(z, axis=-1, keepdims=True)
    zc = z - mu
    var = jnp.mean(zc * zc, axis=-1, keepdims=True)
    return zc * lax.rsqrt(var + LN_EPS) * g + b


def _concat_cast_kernel(a_ref, b_ref, o_ref, obf_ref, *, a_blocks):
    i = pl.program_id(0)

    @pl.when(i < a_blocks)
    def _():
        o_ref[...] = a_ref[...]
        obf_ref[...] = a_ref[...].astype(jnp.bfloat16)

    @pl.when(i >= a_blocks)
    def _():
        o_ref[...] = b_ref[...]
        obf_ref[...] = b_ref[...].astype(jnp.bfloat16)


def concat_cast(a, b, tm=256):
    (ta, d), tb = a.shape, b.shape[0]
    tm = min(tm, ta, tb)
    assert ta % tm == 0 and tb % tm == 0
    a_blocks = ta // tm
    row = pl.BlockSpec((tm, d), lambda i: (i, 0))
    return pl.pallas_call(
        functools.partial(_concat_cast_kernel, a_blocks=a_blocks),
        grid=((ta + tb) // tm,),
        in_specs=[pl.BlockSpec((tm, d), lambda i: (jnp.minimum(i, a_blocks - 1), 0)),
                  pl.BlockSpec((tm, d), lambda i: (jnp.maximum(i - a_blocks, 0), 0))],
        out_specs=[row, row],
        out_shape=[jax.ShapeDtypeStruct((ta + tb, d), jnp.float32), jax.ShapeDtypeStruct((ta + tb, d), jnp.bfloat16)],
        compiler_params=_cparams(("arbitrary",)),
        name="concat_cast",
    )(a, b)


def _banded_kernel(*refs, half, nk, group, step, bounds, use_sink, emit_lse, row_axis, head_axis):
    refs = list(refs)
    slopes_ref = refs.pop(0)
    sink_ref = refs.pop(0) if use_sink else None
    q_ref, kp_ref, kc_ref, kn_ref, vp_ref, vc_ref, vn_ref = refs[:7]
    o_ref = refs[7]
    lse_ref = refs[8] if emit_lse else None

    ub = pl.program_id(row_axis)
    hgi = pl.program_id(head_axis)
    s0, e0 = _seg_bounds(ub * Q_BLOCK, bounds)
    has_prev = ub * Q_BLOCK > s0
    has_next = (ub + 1) * Q_BLOCK < e0
    nsub = Q_BLOCK // Q_SUB
    win = Q_BLOCK + 2 * half
    row = lax.broadcasted_iota(jnp.int32, (Q_SUB, win), 0)
    col = lax.broadcasted_iota(jnp.int32, (Q_SUB, win), 1)
    edge_ok = ((col >= half) | has_prev) & ((col < half + Q_BLOCK) | has_next)
    valids, distfs = [], []
    for sb in range(nsub):
        dist = jnp.abs(col - half - row - sb * Q_SUB)
        valids.append((dist <= half) & edge_ok)
        distfs.append((step * dist).astype(jnp.float32))
    scale2 = HEAD_DIM ** -0.5 * LOG2E
    lane = lax.broadcasted_iota(jnp.int32, (Q_SUB, LANES), 1)
    m_tiles = [jnp.zeros((Q_SUB, LANES), jnp.float32) for _ in range(nsub)]
    l_tiles = [jnp.ones((Q_SUB, LANES), jnp.float32) for _ in range(nsub)]

    for k in range(nk):
        ksl = slice(k * HEAD_DIM, (k + 1) * HEAD_DIM)
        kwin = jnp.concatenate([kp_ref[:, ksl], kc_ref[:, ksl], kn_ref[:, ksl]], axis=0)
        vwin = jnp.concatenate([vp_ref[:, ksl], vc_ref[:, ksl], vn_ref[:, ksl]], axis=0)
        for g in range(group):
            hidx = k * group + g
            hsl = slice(hidx * HEAD_DIM, (hidx + 1) * HEAD_DIM)
            head = hgi * (nk * group) + hidx
            slope2 = slopes_ref[head] * LOG2E
            for sb in range(nsub):
                rows = slice(sb * Q_SUB, (sb + 1) * Q_SUB)
                s = lax.dot_general(q_ref[rows, hsl], kwin, _NT, preferred_element_type=jnp.float32) * scale2
                s = jnp.where(valids[sb], s - slope2 * distfs[sb], NEG_INF)
                m = jnp.max(s, axis=-1, keepdims=True)
                if use_sink:
                    sk = sink_ref[head] * LOG2E
                    m = jnp.maximum(m, sk)
                p = jnp.exp2(s - m)
                l = jnp.sum(p, axis=-1, keepdims=True)
                if use_sink:
                    l = l + jnp.exp2(sk - m)
                o = jnp.dot(p.astype(jnp.bfloat16), vwin, preferred_element_type=jnp.float32) / l
                if emit_lse:
                    m_tiles[sb] = jnp.where(lane == hidx, m, m_tiles[sb])
                    l_tiles[sb] = jnp.where(lane == hidx, l, l_tiles[sb])
                o_ref[rows, hsl] = o.astype(o_ref.dtype)
    if emit_lse:
        for sb in range(nsub):
            lse_ref[sb * Q_SUB:(sb + 1) * Q_SUB, :] = (m_tiles[sb] + jnp.log2(l_tiles[sb])) * LN2


def attention_a(qkv, slopes, sink, segs):
    t = qkv.shape[0]
    nblk = t // Q_BLOCK
    bounds = _bounds(segs, 1)
    qw, kw = A_Q, A_KV
    k_col, v_col = A_Q // kw, (A_Q + A_KV) // kw
    prev = lambda b: jnp.maximum(b - 1, 0)
    nxt = lambda b: jnp.minimum(b + 1, nblk - 1)
    smem = pl.BlockSpec(memory_space=pltpu.SMEM)
    in_specs = [smem, smem,
                pl.BlockSpec((Q_BLOCK, qw), lambda b, h: (b, 0)),
                pl.BlockSpec((Q_BLOCK, kw), lambda b, h: (prev(b), k_col)),
                pl.BlockSpec((Q_BLOCK, kw), lambda b, h: (b, k_col)),
                pl.BlockSpec((Q_BLOCK, kw), lambda b, h: (nxt(b), k_col)),
                pl.BlockSpec((Q_BLOCK, kw), lambda b, h: (prev(b), v_col)),
                pl.BlockSpec((Q_BLOCK, kw), lambda b, h: (b, v_col)),
                pl.BlockSpec((Q_BLOCK, kw), lambda b, h: (nxt(b), v_col))]
    kern = functools.partial(_banded_kernel, half=A_HALF_WINDOW, nk=A_KV_HEADS, group=A_GROUP, step=1,
                             bounds=bounds, use_sink=True, emit_lse=False, row_axis=0, head_axis=1)
    return pl.pallas_call(
        kern,
        grid=(nblk, 1),
        in_specs=in_specs,
        out_specs=pl.BlockSpec((Q_BLOCK, qw), lambda b, h: (b, 0)),
        out_shape=jax.ShapeDtypeStruct((t, qw), jnp.bfloat16),
        compiler_params=_cparams(("parallel", "arbitrary")),
        name="attention_a",
    )(slopes, sink, *([qkv] * 7))


def attention_c_branch(z, slopes, dil, segs):
    rows = z.shape[1]
    nub = rows // Q_BLOCK
    half = C_HALF
    hw = C_HEAD_GROUP * HEAD_DIM
    bounds = _bounds(segs, dil)
    k_col0, v_col0 = C_W // hw, 2 * C_W // hw
    nhalf = rows // half
    prev = lambda u: jnp.maximum(2 * u - 1, 0)
    nxt = lambda u: jnp.minimum(2 * u + 2, nhalf - 1)
    smem = pl.BlockSpec(memory_space=pltpu.SMEM)
    in_specs = [smem,
                pl.BlockSpec((None, Q_BLOCK, hw), lambda r, u, h: (r, u, h)),
                pl.BlockSpec((None, half, hw), lambda r, u, h: (r, prev(u), k_col0 + h)),
                pl.BlockSpec((None, Q_BLOCK, hw), lambda r, u, h: (r, u, k_col0 + h)),
                pl.BlockSpec((None, half, hw), lambda r, u, h: (r, nxt(u), k_col0 + h)),
                pl.BlockSpec((None, half, hw), lambda r, u, h: (r, prev(u), v_col0 + h)),
                pl.BlockSpec((None, Q_BLOCK, hw), lambda r, u, h: (r, u, v_col0 + h)),
                pl.BlockSpec((None, half, hw), lambda r, u, h: (r, nxt(u), v_col0 + h))]
    kern = functools.partial(_banded_kernel, half=half, nk=C_HEAD_GROUP, group=1, step=dil, bounds=bounds,
                             use_sink=False, emit_lse=True, row_axis=1, head_axis=2)
    return pl.pallas_call(
        kern,
        grid=(dil, nub, C_GROUPS),
        in_specs=in_specs,
        out_specs=[pl.BlockSpec((None, Q_BLOCK, hw), lambda r, u, h: (r, u, h)),
                   pl.BlockSpec((None, Q_BLOCK, LANES), lambda r, u, h: (r, u, h))],
        out_shape=[jax.ShapeDtypeStruct((dil, rows, C_W), jnp.bfloat16),
                   jax.ShapeDtypeStruct((dil, rows, C_GROUPS * LANES), jnp.float32)],
        compiler_params=_cparams(("parallel", "parallel", "arbitrary")),
        name="attention_c_d%d" % dil,
    )(slopes, *([z] * 7))


def _plane_perm(dil, tm):
    p = np.zeros((tm, tm), np.float32)
    tok = np.arange(tm)
    p[tok, (tok % dil) * (tm // dil) + tok // dil] = 1.0
    return jnp.asarray(p, jnp.bfloat16)


def _split3(x):
    hi = x.astype(jnp.bfloat16)
    r = x - hi.astype(jnp.float32)
    mid = r.astype(jnp.bfloat16)
    lo = (r - mid.astype(jnp.float32)).astype(jnp.bfloat16)
    return hi, mid, lo


def _merge_kernel(*refs, dils):
    nb = len(dils)
    o_refs, l_refs = refs[:nb], refs[nb:2 * nb]
    p_refs = refs[2 * nb:2 * nb + nb - 1]
    out_ref = refs[-1]
    outs, lses = [o_refs[0][0].astype(jnp.float32)], [l_refs[0][0]]
    for b in range(1, nb):
        d = dils[b]
        perm = p_refs[b - 1][...]
        stacked = jnp.concatenate([o_refs[b][r] for r in range(d)], axis=0)
        outs.append(jnp.dot(perm, stacked, preferred_element_type=jnp.float32))
        lstack = jnp.concatenate([l_refs[b][r] for r in range(d)], axis=0)
        lses.append(sum(jnp.dot(perm, part, preferred_element_type=jnp.float32) for part in _split3(lstack)))
    for h in range(C_HEADS):
        hsl = slice(h * HEAD_DIM, (h + 1) * HEAD_DIM)
        col = (h // C_HEAD_GROUP) * LANES + h % C_HEAD_GROUP
        ls = [l[:, col:col + 1] for l in lses]
        m = functools.reduce(jnp.maximum, ls)
        es = [jnp.exp(l - m) for l in ls]
        tot = sum(es)
        acc = sum((e / tot) * o[:, hsl] for e, o in zip(es, outs))
        out_ref[:, hsl] = acc.astype(out_ref.dtype)


def merge_branches(os_, ls_, dils):
    t = os_[0].shape[0] * os_[0].shape[1]
    tm = min(MERGE_TILE, t)
    in_specs, args = [], []
    for arr, width in ((os_, C_W), (ls_, C_GROUPS * LANES)):
        for a, d in zip(arr, dils):
            in_specs.append(pl.BlockSpec((d, tm // d, width), lambda i: (0, i, 0)))
            args.append(a)
    for d in dils[1:]:
        in_specs.append(pl.BlockSpec((tm, tm), lambda i: (0, 0)))
        args.append(_plane_perm(d, tm))
    return pl.pallas_call(
        functools.partial(_merge_kernel, dils=tuple(dils)),
        grid=(t // tm,),
        in_specs=in_specs,
        out_specs=pl.BlockSpec((tm, C_W), lambda i: (i, 0)),
        out_shape=jax.ShapeDtypeStruct((t, C_W), jnp.bfloat16),
        compiler_params=_cparams(("parallel",)),
        name="merge_branches",
    )(*args)


def attention_c(zs, slopes, segs):
    dils = tuple(d for _, d in C_BRANCHES)
    res = [attention_c_branch(z, slopes, d, segs) for z, d in zip(zs, dils)]
    return merge_branches([r[0] for r in res], [r[1] for r in res], dils)


NA_PAIR = 2
NA_WIN_ROWS = NA_KH + NA_PAIR
NA_WIN_BLOCKS = NA_WIN_ROWS // NA_PAIR
NA_PAIR_CLASSES = ((0, 3, 1, 3), (0, 7, 0, 6), (0, 5, 0, 4), (2, 3, 2, 2), (2, 1, 2, 0))


def _natten_kernel(*refs):
    ng, nb = B_COL_GROUPS, NA_WIN_BLOCKS
    q_refs = refs[:ng]
    k_refs = [refs[ng + g * nb:ng + (g + 1) * nb] for g in range(ng)]
    v_refs = [refs[ng + (ng + g) * nb:ng + (ng + g + 1) * nb] for g in range(ng)]
    bias_ref, o_ref = refs[ng + 2 * ng * nb:]
    scale2 = HEAD_DIM ** -0.5 * LOG2E
    for h in range(B_HEADS):
        g, hl = divmod(h, B_HEAD_GROUP)
        hsl = slice(hl * HEAD_DIM, (hl + 1) * HEAD_DIM)
        kwin = jnp.concatenate([r[:, hsl] for r in k_refs[g]], axis=0)
        vwin = jnp.concatenate([r[:, hsl] for r in v_refs[g]], axis=0)
        s = lax.dot_general(q_refs[g][:, hsl], kwin, _NT, preferred_element_type=jnp.float32) * scale2 + bias_ref[h]
        m = jnp.max(s, axis=-1, keepdims=True)
        p = jnp.exp2(s - m)
        l = jnp.sum(p, axis=-1, keepdims=True)
        o = jnp.dot(p.astype(jnp.bfloat16), vwin, preferred_element_type=jnp.float32) / l
        o_ref[:, h * HEAD_DIM:(h + 1) * HEAD_DIM] = o.astype(o_ref.dtype)


def natten_bias_table(rpb):
    ncls = len(NA_PAIR_CLASSES)
    sel = np.zeros((ncls, NA_PAIR, NA_WIN_ROWS, 2 * NA_KH - 1), np.float32)
    row_ok = np.zeros((ncls, NA_PAIR, NA_WIN_ROWS), bool)
    for ci, cls in enumerate(NA_PAIR_CLASSES):
        for a in range(NA_PAIR):
            off, o = cls[2 * a], cls[2 * a + 1]
            for i in range(NA_KH):
                sel[ci, a, off + i, o + i] = 1.0
                row_ok[ci, a, off + i] = True
    c = jnp.arange(GRID_W)
    cs = jnp.clip(c - NA_KW // 2, 0, GRID_W - NA_KW)
    col_ok = (c[None, :] >= cs[:, None]) & (c[None, :] < cs[:, None] + NA_KW)
    dc = jnp.clip(c[None, :] - c[:, None] + (NA_KW - 1), 0, 2 * NA_KW - 2)
    pick = (dc[:, :, None] == jnp.arange(2 * NA_KW - 1)[None, None, :]).astype(jnp.float32)
    hi = lax.Precision.HIGHEST
    rows = jnp.einsum("capd,hdx->caphx", jnp.asarray(sel), rpb.astype(jnp.float32), precision=hi)
    tab = jnp.einsum("caphx,qkx->chaqpk", rows, pick, precision=hi)
    ok = jnp.asarray(row_ok)[:, None, :, None, :, None] & col_ok[None, None, None, :, None, :]
    tab = jnp.where(ok, tab * LOG2E, NEG_INF)
    return tab.reshape(ncls, B_HEADS, NA_PAIR * GRID_W, NA_WIN_ROWS * GRID_W)


def attention_b(qkv, bias_tab, segs):
    t = qkv.shape[0]
    nrows = t // GRID_W
    bounds = _bounds(segs, GRID_W)
    hw = B_HEAD_GROUP * HEAD_DIM
    ngroups = B_HEADS // B_HEAD_GROUP
    q_col0 = (A_Q + 2 * A_KV) // hw
    k_col0 = q_col0 + B_W // hw
    v_col0 = k_col0 + B_W // hw

    assert all(re - rs >= 2 * NA_KH and (re - rs) % NA_PAIR == 0 for rs, re in bounds)
    blk = NA_PAIR * GRID_W

    def win_block(pr):
        rs, re = _seg_bounds(pr * NA_PAIR, bounds)
        return jnp.clip(pr * NA_PAIR - NA_KH // 2, rs, re - NA_WIN_ROWS) // NA_PAIR

    def pair_class(pr):
        rs, re = _seg_bounds(pr * NA_PAIR, bounds)
        from_start, to_end = pr * NA_PAIR - rs, re - pr * NA_PAIR
        return jnp.where(from_start == 0, 1, jnp.where(from_start == 2, 2,
                         jnp.where(to_end == 4, 3, jnp.where(to_end == 2, 4, 0))))

    def kv_spec(col, i):
        return pl.BlockSpec((blk, hw), lambda pr: (win_block(pr) + i, col))

    assert ngroups == B_COL_GROUPS
    in_specs = [pl.BlockSpec((blk, hw), functools.partial(lambda pr, g: (pr, q_col0 + g), g=g))
                for g in range(ngroups)]
    in_specs += [kv_spec(k_col0 + g, i) for g in range(ngroups) for i in range(NA_WIN_BLOCKS)]
    in_specs += [kv_spec(v_col0 + g, i) for g in range(ngroups) for i in range(NA_WIN_BLOCKS)]
    in_specs.append(pl.BlockSpec((None, B_HEADS, blk, NA_WIN_ROWS * GRID_W), lambda pr: (pair_class(pr), 0, 0, 0)))
    return pl.pallas_call(
        _natten_kernel,
        grid=(nrows // NA_PAIR,),
        in_specs=in_specs,
        out_specs=pl.BlockSpec((blk, B_W), lambda pr: (pr, 0)),
        out_shape=jax.ShapeDtypeStruct((t, B_W), jnp.bfloat16),
        compiler_params=_cparams(("parallel",)),
        name="attention_b",
    )(*([qkv] * (ngroups * (1 + 2 * NA_WIN_BLOCKS))), bias_tab)


def _top2(vals):
    v1, i1 = vals[0], jnp.zeros(vals[0].shape, jnp.int32)
    for j in range(1, len(vals)):
        better = vals[j] > v1
        v1 = jnp.where(better, vals[j], v1)
        i1 = jnp.where(better, j, i1)
    v2, i2 = None, None
    for j in range(len(vals)):
        cand = jnp.where(i1 == j, -1.0, vals[j])
        if v2 is None:
            v2, i2 = cand, jnp.zeros(vals[0].shape, jnp.int32)
        else:
            better = cand > v2
            v2 = jnp.where(better, cand, v2)
            i2 = jnp.where(better, j, i2)
    return v1, i1, v2, i2


def _router_kernel(x_ref, w2_ref, whi_ref, b_ref, idx_ref, gate_ref):
    x = x_ref[...]
    x_hi = x.astype(jnp.bfloat16)
    x_lo = (x - x_hi.astype(jnp.float32)).astype(jnp.bfloat16)
    r1 = lax.dot_general(w2_ref[...], x_hi, _NT, preferred_element_type=jnp.float32)
    r2 = lax.dot_general(whi_ref[...], x_lo, _NT, preferred_element_type=jnp.float32)
    logits = r1[:N_EXPERTS] + r1[N_EXPERTS:] + r2 + b_ref[...]
    m = jnp.max(logits, axis=0, keepdims=True)
    ex = jnp.exp(logits - m)
    probs = ex / jnp.sum(ex, axis=0, keepdims=True)
    rows = [probs[e:e + 1, :] for e in range(N_EXPERTS)]
    tops = [_top2(rows[g * EXPERTS_PER_GROUP:(g + 1) * EXPERTS_PER_GROUP]) for g in range(N_GROUPS)]
    score = [tp[0] + tp[2] for tp in tops]
    best, gsel = score[0], jnp.zeros(score[0].shape, jnp.int32)
    for g in range(1, N_GROUPS):
        better = score[g] > best
        best = jnp.where(better, score[g], best)
        gsel = jnp.where(better, g, gsel)
    v1, i1, v2, i2 = tops[0]
    for g in range(1, N_GROUPS):
        pick = gsel == g
        v1 = jnp.where(pick, tops[g][0], v1)
        i1 = jnp.where(pick, tops[g][1], i1)
        v2 = jnp.where(pick, tops[g][2], v2)
        i2 = jnp.where(pick, tops[g][3], i2)
    tot = v1 + v2
    idx_ref[0:1, :] = gsel * EXPERTS_PER_GROUP + i1
    idx_ref[1:2, :] = gsel * EXPERTS_PER_GROUP + i2
    gate_ref[0:1, :] = v1 / tot
    gate_ref[1:2, :] = v2 / tot


def router(x, router_w, router_b, tm=512):
    t, d = x.shape
    tm = min(tm, t)
    wt = router_w.astype(jnp.float32).T
    w_hi = wt.astype(jnp.bfloat16)
    w_lo = (wt - w_hi.astype(jnp.float32)).astype(jnp.bfloat16)
    w2 = jnp.concatenate([w_hi, w_lo], axis=0)
    return pl.pallas_call(
        _router_kernel,
        grid=(t // tm,),
        in_specs=[pl.BlockSpec((tm, d), lambda i: (i, 0)),
                  pl.BlockSpec((2 * N_EXPERTS, d), lambda i: (0, 0)),
                  pl.BlockSpec((N_EXPERTS, d), lambda i: (0, 0)),
                  pl.BlockSpec((N_EXPERTS, 1), lambda i: (0, 0))],
        out_specs=[pl.BlockSpec((2, tm), lambda i: (0, i))] * 2,
        out_shape=[jax.ShapeDtypeStruct((2, t), jnp.int32), jax.ShapeDtypeStruct((2, t), jnp.float32)],
        compiler_params=_cparams(("parallel",)),
        name="router",
    )(x, w2, w_hi, router_b.astype(jnp.float32).reshape(N_EXPERTS, 1))


def dispatch_plan(idx):
    _, t = idx.shape
    nt = t // MOE_TILE
    n_chunks = _round_up(nt * CHUNKS_PER_TILE + N_EXPERTS * (CHUNKS_PER_EXPERT_TILE - 1), CHUNKS_PER_EXPERT_TILE)
    i32 = jnp.int32
    e_tile = idx.reshape(2, nt, MOE_TILE).transpose(1, 0, 2).reshape(nt, 2 * MOE_TILE)
    hit = e_tile[:, :, None] == jnp.arange(N_EXPERTS, dtype=i32)[None, None, :]
    onehot = hit.astype(i32)
    tri = (jnp.arange(2 * MOE_TILE)[:, None] >= jnp.arange(2 * MOE_TILE)[None, :]).astype(jnp.bfloat16)
    csum = jnp.einsum("ab,nbe->nae", tri, hit.astype(jnp.bfloat16), preferred_element_type=jnp.float32).astype(i32)
    rank = jnp.sum(csum * onehot, axis=-1) - 1
    cnt = csum[:, -1, :]
    nc = (cnt + SLOT_CHUNK - 1) // SLOT_CHUNK
    lo = jnp.cumsum(nc, axis=1) - nc
    local_pos = jnp.sum(onehot * (lo * SLOT_CHUNK)[:, None, :], axis=-1) + rank
    pos = local_pos.reshape(nt, 2, MOE_TILE).transpose(1, 0, 2).reshape(2, t).astype(i32)

    nc_t = nc.T
    e_chunks = nc_t.sum(axis=1)
    e_pad = _round_up(e_chunks, CHUNKS_PER_EXPERT_TILE)
    e_end = jnp.cumsum(e_pad)
    run_start = (e_end - e_pad)[:, None] + jnp.cumsum(nc_t, axis=1) - nc_t
    src0 = jnp.arange(nt, dtype=i32)[None, :] * CHUNKS_PER_TILE + lo.T
    rs_f, nc_f, src0_f = run_start.reshape(1, -1), nc_t.reshape(1, -1), src0.reshape(1, -1)
    d = jnp.arange(n_chunks, dtype=i32)[:, None]
    in_run = (d >= rs_f) & (d < rs_f + nc_f)
    fwd = jnp.sum(jnp.where(in_run, src0_f + d - rs_f, 0), axis=1).astype(i32)

    lc = jnp.arange(CHUNKS_PER_TILE, dtype=i32)[None, :, None]
    lo_b, nc_b = lo[:, None, :], nc[:, None, :]
    in_run_b = (lc >= lo_b) & (lc < lo_b + nc_b)
    back = jnp.sum(jnp.where(in_run_b, run_start.T[:, None, :] + lc - lo_b, 0), axis=-1)
    back = back.reshape(nt * CHUNKS_PER_TILE).astype(i32)

    n_tiles = n_chunks // CHUNKS_PER_EXPERT_TILE
    tile_first = jnp.arange(n_tiles, dtype=i32) * CHUNKS_PER_EXPERT_TILE
    tile_expert = jnp.minimum(jnp.sum((e_end[None, :] <= tile_first[:, None]).astype(i32), axis=1),
                              N_EXPERTS - 1).astype(i32)
    n_active = (e_end[-1] // CHUNKS_PER_EXPERT_TILE).astype(i32).reshape(1)
    return pos, fwd, back, tile_expert, n_active


def _permute_kernel(x_ref, pos_ref, gate_ref, xs_ref, gl_ref):
    tt, d = x_ref.shape
    slot = lax.broadcasted_iota(jnp.int32, (LOCAL_SLOTS, tt), 0)
    hit1 = slot == pos_ref[0:1, :]
    hit2 = slot == pos_ref[1:2, :]
    perm = jnp.where(hit1 | hit2, 1.0, 0.0).astype(jnp.bfloat16)
    cw = min(d, 512)
    for n in range(d // cw):
        sl = slice(n * cw, (n + 1) * cw)
        xs_ref[:, sl] = jnp.dot(perm, x_ref[:, sl], preferred_element_type=jnp.float32).astype(xs_ref.dtype)
    gates = jnp.where(hit1, gate_ref[0:1, :], 0.0) + jnp.where(hit2, gate_ref[1:2, :], 0.0)
    gl_ref[...] = jnp.broadcast_to(jnp.sum(gates, axis=1, keepdims=True), gl_ref.shape)


def permute_tokens(x_bf, pos, gate):
    t, d = x_bf.shape
    nt = t // MOE_TILE
    return pl.pallas_call(
        _permute_kernel,
        grid=(nt,),
        in_specs=[pl.BlockSpec((MOE_TILE, d), lambda i: (i, 0)),
                  pl.BlockSpec((2, MOE_TILE), lambda i: (0, i)),
                  pl.BlockSpec((2, MOE_TILE), lambda i: (0, i))],
        out_specs=[pl.BlockSpec((LOCAL_SLOTS, d), lambda i: (i, 0)),
                   pl.BlockSpec((LOCAL_SLOTS, LANES), lambda i: (i, 0))],
        out_shape=[jax.ShapeDtypeStruct((nt * LOCAL_SLOTS, d), jnp.bfloat16),
                   jax.ShapeDtypeStruct((nt * LOCAL_SLOTS, LANES), jnp.float32)],
        compiler_params=_cparams(("parallel",)),
        name="permute_tokens",
    )(x_bf, pos, gate)


def _unpermute_kernel(back_ref, ys_ref, gl_ref, p1_ref, p2_ref, y_ref, buf_ref, sems):
    i = pl.program_id(0)
    tt, d = y_ref.shape
    n = CHUNKS_PER_TILE
    buf_slot = i % 2

    @pl.when(i == 0)
    def _():
        _fetch_chunks(ys_ref, back_ref, 0, n, buf_ref, 0, sems)

    @pl.when(i + 1 < pl.num_programs(0))
    def _():
        _fetch_chunks(ys_ref, back_ref, (i + 1) * n, n, buf_ref, 1 - buf_slot, sems)

    slot = lax.broadcasted_iota(jnp.int32, (tt, LOCAL_SLOTS), 1)
    hit = (slot == p1_ref[:, 0:1]) | (slot == p2_ref[:, 0:1])
    perm = jnp.where(hit, 1.0, 0.0).astype(jnp.bfloat16)
    g = gl_ref[:, 0:1]
    _wait_chunks(ys_ref, n, buf_ref, buf_slot, sems)
    cw = min(d, 512)
    for c in range(d // cw):
        sl = slice(c * cw, (c + 1) * cw)
        scaled = (buf_ref[buf_slot, :, sl].astype(jnp.float32) * g).astype(jnp.bfloat16)
        y_ref[:, sl] = jnp.dot(perm, scaled, preferred_element_type=jnp.float32)


UNPERMUTE_SPLIT = 2


def _unpermute_ln_kernel(back_ref, ys_ref, gl_ref, p1_ref, p2_ref, x_ref, g_ref, b_ref, o_ref, obf_ref,
                         buf_ref, sems):
    i, part = pl.program_id(0), pl.program_id(1)
    rows, d = o_ref.shape
    n = CHUNKS_PER_TILE
    buf_slot = i % 2
    cw = min(d, 512)

    @pl.when((i == 0) & (part == 0))
    def _():
        _fetch_chunks(ys_ref, back_ref, 0, n, buf_ref, 0, sems)

    @pl.when((part == 0) & (i + 1 < pl.num_programs(0)))
    def _():
        _fetch_chunks(ys_ref, back_ref, (i + 1) * n, n, buf_ref, 1 - buf_slot, sems)

    @pl.when(part == 0)
    def _():
        _wait_chunks(ys_ref, n, buf_ref, buf_slot, sems)
        gate = gl_ref[:, 0:1]
        for c in range(d // cw):
            sl = slice(c * cw, (c + 1) * cw)
            buf_ref[buf_slot, :, sl] = (buf_ref[buf_slot, :, sl].astype(jnp.float32) * gate).astype(buf_ref.dtype)

    slot = lax.broadcasted_iota(jnp.int32, (rows, LOCAL_SLOTS), 1)
    hit = (slot == p1_ref[:, 0:1]) | (slot == p2_ref[:, 0:1])
    perm = jnp.where(hit, 1.0, 0.0).astype(jnp.bfloat16)
    for c in range(d // cw):
        sl = slice(c * cw, (c + 1) * cw)
        o_ref[:, sl] = jnp.dot(perm, buf_ref[buf_slot, :, sl], preferred_element_type=jnp.float32)
    out = _layer_norm_rows(DEEPNORM_ALPHA * x_ref[...] + o_ref[...], g_ref[...], b_ref[...])
    o_ref[...] = out
    obf_ref[...] = out.astype(jnp.bfloat16)


def unpermute_resid_ln(ys, back, gate_local, pos, x, g, b):
    t, d = x.shape
    nt = t // MOE_TILE
    rows = MOE_TILE // UNPERMUTE_SPLIT
    p1 = jnp.broadcast_to(pos[0][:, None], (t, LANES))
    p2 = jnp.broadcast_to(pos[1][:, None], (t, LANES))
    row = lambda width: pl.BlockSpec((rows, width), lambda i, p, bk: (i * UNPERMUTE_SPLIT + p, 0))
    vec = pl.BlockSpec((1, d), lambda i, p, bk: (0, 0))
    return pl.pallas_call(
        _unpermute_ln_kernel,
        grid_spec=pltpu.PrefetchScalarGridSpec(
            num_scalar_prefetch=1,
            grid=(nt, UNPERMUTE_SPLIT),
            in_specs=[pl.BlockSpec(memory_space=pl.ANY),
                      pl.BlockSpec((LOCAL_SLOTS, LANES), lambda i, p, bk: (i, 0)),
                      row(LANES), row(LANES), row(d), vec, vec],
            out_specs=[row(d), row(d)],
            scratch_shapes=[pltpu.VMEM((2, LOCAL_SLOTS, d), ys.dtype), pltpu.SemaphoreType.DMA((2,))]),
        out_shape=[jax.ShapeDtypeStruct((t, d), jnp.float32), jax.ShapeDtypeStruct((t, d), jnp.bfloat16)],
        compiler_params=_cparams(("arbitrary", "arbitrary")),
        name="unpermute_resid_ln",
    )(back, ys.reshape(ys.shape[0] // SLOT_CHUNK, SLOT_CHUNK, d), gate_local, p1, p2, x,
      g.reshape(1, d), b.reshape(1, d))


def unpermute_tokens(ys, back, gate_local, pos):
    _, t = pos.shape
    d = ys.shape[1]
    nt = t // MOE_TILE
    p1 = jnp.broadcast_to(pos[0][:, None], (t, LANES))
    p2 = jnp.broadcast_to(pos[1][:, None], (t, LANES))
    return pl.pallas_call(
        _unpermute_kernel,
        grid_spec=pltpu.PrefetchScalarGridSpec(
            num_scalar_prefetch=1,
            grid=(nt,),
            in_specs=[pl.BlockSpec(memory_space=pl.ANY),
                      pl.BlockSpec((LOCAL_SLOTS, LANES), lambda i, bk: (i, 0)),
                      pl.BlockSpec((MOE_TILE, LANES), lambda i, bk: (i, 0)),
                      pl.BlockSpec((MOE_TILE, LANES), lambda i, bk: (i, 0))],
            out_specs=pl.BlockSpec((MOE_TILE, d), lambda i, bk: (i, 0)),
            scratch_shapes=[pltpu.VMEM((2, LOCAL_SLOTS, d), ys.dtype), pltpu.SemaphoreType.DMA((2,))]),
        out_shape=jax.ShapeDtypeStruct((t, d), jnp.float32),
        compiler_params=_cparams(("arbitrary",)),
        name="unpermute_tokens",
    )(back, ys.reshape(ys.shape[0] // SLOT_CHUNK, SLOT_CHUNK, d), gate_local, p1, p2)


def _chunk_copy(src_ref, chunk, buf_ref, slot, g, sems):
    return pltpu.make_async_copy(src_ref.at[chunk], buf_ref.at[slot, pl.ds(g * SLOT_CHUNK, SLOT_CHUNK)],
                                 sems.at[slot])


def _fetch_chunks(src_ref, idx_ref, first, n, buf_ref, slot, sems):
    for g in range(n):
        _chunk_copy(src_ref, idx_ref[first + g], buf_ref, slot, g, sems).start()


def _wait_chunks(src_ref, n, buf_ref, slot, sems):
    for g in range(n):
        _chunk_copy(src_ref, 0, buf_ref, slot, g, sems).wait()


def _expert_up_kernel(te_ref, na_ref, fwd_ref, xs1_ref, w1_ref, w3_ref, h_ref, buf_ref, sems):
    j, i = pl.program_id(0), pl.program_id(1)
    na = na_ref[0]
    active = i < na

    @pl.when(active)
    def _():
        a = j * na + i
        slot = a % 2
        n = CHUNKS_PER_EXPERT_TILE

        @pl.when(a == 0)
        def _():
            _fetch_chunks(xs1_ref, fwd_ref, 0, n, buf_ref, 0, sems)

        @pl.when(a + 1 < pl.num_programs(0) * na)
        def _():
            nxt = jnp.where(i + 1 < na, i + 1, 0)
            _fetch_chunks(xs1_ref, fwd_ref, nxt * n, n, buf_ref, 1 - slot, sems)

        _wait_chunks(xs1_ref, n, buf_ref, slot, sems)
        x = buf_ref[slot]
        h1 = jnp.dot(x, w1_ref[...].astype(jnp.bfloat16), preferred_element_type=jnp.float32)
        h3 = jnp.dot(x, w3_ref[...].astype(jnp.bfloat16), preferred_element_type=jnp.float32)
        h_ref[...] = (jax.nn.silu(h1) * h3).astype(h_ref.dtype)

    @pl.when(jnp.logical_not(active))
    def _():
        h_ref[...] = jnp.zeros(h_ref.shape, h_ref.dtype)


def _expert_down_kernel(te_ref, na_ref, h_ref, w2_ref, y_ref):
    active = pl.program_id(1) < na_ref[0]

    @pl.when(active)
    def _():
        y_ref[...] = jnp.dot(h_ref[...], w2_ref[...].astype(jnp.bfloat16),
                             preferred_element_type=jnp.float32).astype(y_ref.dtype)

    @pl.when(jnp.logical_not(active))
    def _():
        y_ref[...] = jnp.zeros(y_ref.shape, y_ref.dtype)


def expert_ffn(xs1, fwd, w1, w3, w2, layer, tile_expert, n_active, tf=512, tn=4096):
    d = xs1.shape[1]
    f = w1.shape[3]
    tm = EXPERT_TILE
    tf, tn = min(tf, f), min(tn, d)
    n_tiles = fwd.shape[0] // CHUNKS_PER_EXPERT_TILE
    s = n_tiles * tm

    def act(i, na):
        return jnp.minimum(i, na[0] - 1)

    up = pl.pallas_call(
        _expert_up_kernel,
        grid_spec=pltpu.PrefetchScalarGridSpec(
            num_scalar_prefetch=3,
            grid=(f // tf, n_tiles),
            in_specs=[pl.BlockSpec(memory_space=pl.ANY),
                      pl.BlockSpec((None, None, d, tf), lambda j, i, te, na, fw: (layer, te[act(i, na)], 0, j)),
                      pl.BlockSpec((None, None, d, tf), lambda j, i, te, na, fw: (layer, te[act(i, na)], 0, j))],
            out_specs=pl.BlockSpec((tm, tf), lambda j, i, te, na, fw: (i, j)),
            scratch_shapes=[pltpu.VMEM((2, tm, d), xs1.dtype), pltpu.SemaphoreType.DMA((2,))]),
        out_shape=jax.ShapeDtypeStruct((s, f), jnp.bfloat16),
        compiler_params=_cparams(("arbitrary", "arbitrary")),
        name="expert_up",
    )
    h = up(tile_expert, n_active, fwd, xs1.reshape(xs1.shape[0] // SLOT_CHUNK, SLOT_CHUNK, d), w1, w3)
    down = pl.pallas_call(
        _expert_down_kernel,
        grid_spec=pltpu.PrefetchScalarGridSpec(
            num_scalar_prefetch=2,
            grid=(d // tn, n_tiles),
            in_specs=[pl.BlockSpec((tm, f), lambda j, i, te, na: (act(i, na), 0)),
                      pl.BlockSpec((None, None, f, tn), lambda j, i, te, na: (layer, te[act(i, na)], 0, j))],
            out_specs=pl.BlockSpec((tm, tn), lambda j, i, te, na: (i, j))),
        out_shape=jax.ShapeDtypeStruct((s, d), jnp.bfloat16),
        compiler_params=_cparams(("arbitrary", "arbitrary")),
        name="expert_down",
    )
    return down(tile_expert, n_active, h, w2)


def moe_layer(x, x_bf, router_w, router_b, w1, w3, w2, layer, g, b, out_rows=None):
    idx, gate = router(x, router_w, router_b)
    pos, fwd, back, tile_expert, n_active = dispatch_plan(idx)
    xs1, gate_local = permute_tokens(x_bf, pos, gate)
    ys = expert_ffn(xs1, fwd, w1, w3, w2, layer, tile_expert, n_active)
    if out_rows is None:
        return unpermute_resid_ln(ys, back, gate_local, pos, x, g, b)
    y = unpermute_tokens(ys, back, gate_local, pos)
    return [resid_ln(x, y, g, b, rows=r)[0] for r in out_rows]


def alibi_slopes(n):
    return jnp.exp2(-8.0 * jnp.arange(1, n + 1, dtype=jnp.float32) / n)


def trunk(x, x_bf, segs, out_rows, w_in_ab, w_out_ab, a_sink, b_rpb, w_in_c, w_out_c, router_w, router_b,
          moe_w1, moe_w3, moe_w2, ln_g, ln_b):
    bf = jnp.bfloat16
    slopes_a = alibi_slopes(A_HEADS)
    slopes_c = alibi_slopes(C_HEADS)
    dils = tuple(d for _, d in C_BRANCHES)
    for layer in range(DEPTH):
        i = layer // 2
        if layer % 2 == 0:
            qkv = matmul([x_bf], w_in_ab[i].astype(bf), bf)
            oa = attention_a(qkv, slopes_a, a_sink[i].astype(jnp.float32), segs)
            ob = attention_b(qkv, natten_bias_table(b_rpb[i]), segs)
            mixed, w_out = [oa, ob], w_out_ab[i]
        else:
            zs = matmul_dilated(x_bf, w_in_c[i].astype(bf), dils[1:])
            zs = [zs[0].reshape((1,) + zs[0].shape)] + list(zs[1:])
            mixed, w_out = [attention_c(zs, slopes_c, segs)], w_out_c[i]
        mix = matmul(mixed, w_out.astype(bf), jnp.float32)
        x, x_bf = resid_ln(x, mix, ln_g[layer, 0], ln_b[layer, 0])
        last = layer == DEPTH - 1
        res = moe_layer(x, x_bf, router_w, router_b, moe_w1, moe_w3, moe_w2, layer,
                        ln_g[layer, 1], ln_b[layer, 1], out_rows=out_rows if last else None)
        if last:
            return res
        x, x_bf = res


def kernel(x_prompt, x_sample, w_in_ab, w_out_ab, a_sink, b_rpb, w_in_c, w_out_c, router_w, router_b,
           moe_w1, moe_w3, moe_w2, ln_g, ln_b):
    bp, sp, d = x_prompt.shape
    bs, ss, _ = x_sample.shape
    segs = ((bp, sp), (bs, ss))
    x, x_bf = concat_cast(x_prompt.reshape(bp * sp, d), x_sample.reshape(bs * ss, d))
    out_rows = ((0, bp * sp), (bp * sp, bs * ss))
    yp, ys = trunk(x, x_bf, segs, out_rows, w_in_ab, w_out_ab, a_sink, b_rpb, w_in_c, w_out_c, router_w, router_b,
                   moe_w1, moe_w3, moe_w2, ln_g, ln_b)
    return (yp.reshape(bp, sp, d), ys.reshape(bs, ss, d))
```

```python
import functools

import numpy as np
import jax
import jax.numpy as jnp
from jax import lax
from jax.experimental import pallas as pl
from jax.experimental.pallas import tpu as pltpu

HEAD_DIM = 128
GRID_W = 64
A_HEADS = 16
A_KV_HEADS = 4
A_GROUP = A_HEADS // A_KV_HEADS
A_HALF_WINDOW = 128
B_HEADS = 16
NA_KH = 8
NA_KW = 16
C_HEADS = 32
C_BRANCHES = ((128, 1), (512, 4), (2048, 16))
C_HALF = 64
N_EXPERTS = 16
N_GROUPS = 4
EXPERTS_PER_GROUP = N_EXPERTS // N_GROUPS
DEPTH = 4
DEEPNORM_ALPHA = (2 * DEPTH) ** 0.25
LN_EPS = 1e-5
NEG_INF = -1e30
LOG2E = 1.4426950408889634
LN2 = 0.6931471805599453

A_Q = A_HEADS * HEAD_DIM
A_KV = A_KV_HEADS * HEAD_DIM
B_W = B_HEADS * HEAD_DIM
C_W = C_HEADS * HEAD_DIM

LANES = 128
BF16_SUBLANES = 16
Q_BLOCK = 128
Q_SUB = 128
C_HEAD_GROUP = 32
C_GROUPS = C_HEADS // C_HEAD_GROUP
B_HEAD_GROUP = 8
B_COL_GROUPS = B_HEADS // B_HEAD_GROUP
MERGE_TILE = 256
MOE_TILE = 512
SLOT_CHUNK = BF16_SUBLANES
LOCAL_SLOTS = 2 * MOE_TILE + N_EXPERTS * SLOT_CHUNK
CHUNKS_PER_TILE = LOCAL_SLOTS // SLOT_CHUNK
EXPERT_TILE = 512
CHUNKS_PER_EXPERT_TILE = EXPERT_TILE // SLOT_CHUNK
VMEM_LIMIT = 56 * 1024 * 1024

_NT = (((1,), (1,)), ((), ()))


def _cparams(sem):
    return pltpu.CompilerParams(dimension_semantics=sem, vmem_limit_bytes=VMEM_LIMIT)


def _round_up(x, m):
    return (x + m - 1) // m * m


def _seg_bounds(pos, bounds):
    s = jnp.int32(bounds[0][0])
    e = jnp.int32(bounds[0][1])
    for a, b in bounds[1:]:
        inside = pos >= a
        s = jnp.where(inside, jnp.int32(a), s)
        e = jnp.where(inside, jnp.int32(b), e)
    return s, e


def _bounds(segs, unit):
    out, pos = [], 0
    for n, length in segs:
        for _ in range(n):
            out.append((pos // unit, (pos + length) // unit))
            pos += length
    return tuple(out)


def _matmul_kernel(*refs, widths):
    xs, w_ref, o_ref = refs[:len(widths)], refs[len(widths)], refs[len(widths) + 1]
    acc, off = None, 0
    for x_ref, k in zip(xs, widths):
        part = jnp.dot(x_ref[...], w_ref[off:off + k, :], preferred_element_type=jnp.float32)
        acc = part if acc is None else acc + part
        off += k
    o_ref[...] = acc.astype(o_ref.dtype)


def matmul(xs, w, out_dtype, tm=1024, tn=1024):
    t = xs[0].shape[0]
    k, n = w.shape
    widths = tuple(x.shape[1] for x in xs)
    assert sum(widths) == k
    tm, tn = min(tm, t), min(tn, n)
    assert t % tm == 0 and n % tn == 0
    in_specs = [pl.BlockSpec((tm, kw), lambda i, j: (i, 0)) for kw in widths]
    in_specs.append(pl.BlockSpec((k, tn), lambda i, j: (0, j)))
    return pl.pallas_call(
        functools.partial(_matmul_kernel, widths=widths),
        grid=(t // tm, n // tn),
        in_specs=in_specs,
        out_specs=pl.BlockSpec((tm, tn), lambda i, j: (i, j)),
        out_shape=jax.ShapeDtypeStruct((t, n), out_dtype),
        compiler_params=_cparams(("parallel", "arbitrary")),
        name="matmul",
    )(*xs, w)


def _matmul_dilated_kernel(x_ref, w_ref, o_ref, *rest, dils):
    z_refs, acc_ref = rest[:-1], rest[-1]
    tm, tn = o_ref.shape
    acc = jnp.dot(x_ref[...], w_ref[...], preferred_element_type=jnp.float32)
    o_ref[...] = acc.astype(o_ref.dtype)
    for s in range(tn // LANES):
        acc_ref[s] = acc[:, s * LANES:(s + 1) * LANES]
    for z_ref, d in zip(z_refs, dils):
        for rho in range(d):
            for s in range(tn // LANES):
                z_ref[rho, :, s * LANES:(s + 1) * LANES] = (
                    acc_ref[s, pl.ds(rho, tm // d, stride=d), :].astype(z_ref.dtype))


def matmul_dilated(x, w, dils, tm=1024, tn=512):
    t, k = x.shape
    n = w.shape[1]
    tm, tn = min(tm, t), min(tn, n)
    assert t % tm == 0 and n % tn == 0 and all(tm % (d * BF16_SUBLANES) == 0 for d in dils)
    out_specs = [pl.BlockSpec((tm, tn), lambda i, j: (i, j))]
    out_shape = [jax.ShapeDtypeStruct((t, n), jnp.bfloat16)]
    for d in dils:
        out_specs.append(pl.BlockSpec((d, tm // d, tn), lambda i, j: (0, i, j)))
        out_shape.append(jax.ShapeDtypeStruct((d, t // d, n), jnp.bfloat16))
    return pl.pallas_call(
        functools.partial(_matmul_dilated_kernel, dils=tuple(dils)),
        grid=(t // tm, n // tn),
        in_specs=[pl.BlockSpec((tm, k), lambda i, j: (i, 0)), pl.BlockSpec((k, tn), lambda i, j: (0, j))],
        out_specs=out_specs,
        out_shape=out_shape,
        scratch_shapes=[pltpu.VMEM((tn // LANES, tm, LANES), jnp.float32)],
        compiler_params=_cparams(("parallel", "arbitrary")),
        name="matmul_dilated",
    )(x, w)


def _resid_ln_kernel(x_ref, y_ref, g_ref, b_ref, o_ref, obf_ref):
    out = _layer_norm_rows(DEEPNORM_ALPHA * x_ref[...] + y_ref[...], g_ref[...], b_ref[...])
    o_ref[...] = out
    obf_ref[...] = out.astype(jnp.bfloat16)


def resid_ln(x, y, g, b, rows=None, tm=256):
    t, d = x.shape
    start, count = (0, t) if rows is None else rows
    tm = min(tm, count)
    assert count % tm == 0 and start % tm == 0
    first = start // tm
    row_in = pl.BlockSpec((tm, d), lambda i: (first + i, 0))
    row_out = pl.BlockSpec((tm, d), lambda i: (i, 0))
    vec = pl.BlockSpec((1, d), lambda i: (0, 0))
    return pl.pallas_call(
        _resid_ln_kernel,
        grid=(count // tm,),
        in_specs=[row_in, row_in, vec, vec],
        out_specs=[row_out, row_out],
        out_shape=[jax.ShapeDtypeStruct((count, d), jnp.float32), jax.ShapeDtypeStruct((count, d), jnp.bfloat16)],
        compiler_params=_cparams(("parallel",)),
        name="resid_ln",
    )(x, y, g.reshape(1, d), b.reshape(1, d))


def _layer_norm_rows(z, g, b):
    mu = jnp.mean(z, axis=-1, keepdims=True)
    zc = z - mu
    var = jnp.mean(zc * zc, axis=-1, keepdims=True)
    return zc * lax.rsqrt(var + LN_EPS) * g + b


def _concat_cast_kernel(a_ref, b_ref, o_ref, obf_ref, *, a_blocks):
    i = pl.program_id(0)

    @pl.when(i < a_blocks)
    def _():
        o_ref[...] = a_ref[...]
        obf_ref[...] = a_ref[...].astype(jnp.bfloat16)

    @pl.when(i >= a_blocks)
    def _():
        o_ref[...] = b_ref[...]
        obf_ref[...] = b_ref[...].astype(jnp.bfloat16)


def concat_cast(a, b, tm=256):
    (ta, d), tb = a.shape, b.shape[0]
    tm = min(tm, ta, tb)
    assert ta % tm == 0 and tb % tm == 0
    a_blocks = ta // tm
    row = pl.BlockSpec((tm, d), lambda i: (i, 0))
    return pl.pallas_call(
        functools.partial(_concat_cast_kernel, a_blocks=a_blocks),
        grid=((ta + tb) // tm,),
        in_specs=[pl.BlockSpec((tm, d), lambda i: (jnp.minimum(i, a_blocks - 1), 0)),
                  pl.BlockSpec((tm, d), lambda i: (jnp.maximum(i - a_blocks, 0), 0))],
        out_specs=[row, row],
        out_shape=[jax.ShapeDtypeStruct((ta + tb, d), jnp.float32), jax.ShapeDtypeStruct((ta + tb, d), jnp.bfloat16)],
        compiler_params=_cparams(("arbitrary",)),
        name="concat_cast",
    )(a, b)


def _banded_kernel(*refs, half, nk, group, step, bounds, use_sink, emit_lse, row_axis, head_axis):
    refs = list(refs)
    slopes_ref = refs.pop(0)
    sink_ref = refs.pop(0) if use_sink else None
    q_ref, kp_ref, kc_ref, kn_ref, vp_ref, vc_ref, vn_ref = refs[:7]
    o_ref = refs[7]
    lse_ref = refs[8] if emit_lse else None

    ub = pl.program_id(row_axis)
    hgi = pl.program_id(head_axis)
    s0, e0 = _seg_bounds(ub * Q_BLOCK, bounds)
    has_prev = ub * Q_BLOCK > s0
    has_next = (ub + 1) * Q_BLOCK < e0
    nsub = Q_BLOCK // Q_SUB
    win = Q_BLOCK + 2 * half
    row = lax.broadcasted_iota(jnp.int32, (Q_SUB, win), 0)
    col = lax.broadcasted_iota(jnp.int32, (Q_SUB, win), 1)
    edge_ok = ((col >= half) | has_prev) & ((col < half + Q_BLOCK) | has_next)
    valids, distfs = [], []
    for sb in range(nsub):
        dist = jnp.abs(col - half - row - sb * Q_SUB)
        valids.append((dist <= half) & edge_ok)
        distfs.append((step * dist).astype(jnp.float32))
    scale2 = HEAD_DIM ** -0.5 * LOG2E
    lane = lax.broadcasted_iota(jnp.int32, (Q_SUB, LANES), 1)
    m_tiles = [jnp.zeros((Q_SUB, LANES), jnp.float32) for _ in range(nsub)]
    l_tiles = [jnp.ones((Q_SUB, LANES), jnp.float32) for _ in range(nsub)]

    for k in range(nk):
        ksl = slice(k * HEAD_DIM, (k + 1) * HEAD_DIM)
        kwin = jnp.concatenate([kp_ref[:, ksl], kc_ref[:, ksl], kn_ref[:, ksl]], axis=0)
        vwin = jnp.concatenate([vp_ref[:, ksl], vc_ref[:, ksl], vn_ref[:, ksl]], axis=0)
        for g in range(group):
            hidx = k * group + g
            hsl = slice(hidx * HEAD_DIM, (hidx + 1) * HEAD_DIM)
            head = hgi * (nk * group) + hidx
            slope2 = slopes_ref[head] * LOG2E
            for sb in range(nsub):
                rows = slice(sb * Q_SUB, (sb + 1) * Q_SUB)
                s = lax.dot_general(q_ref[rows, hsl], kwin, _NT, preferred_element_type=jnp.float32) * scale2
                s = jnp.where(valids[sb], s - slope2 * distfs[sb], NEG_INF)
                m = jnp.max(s, axis=-1, keepdims=True)
                if use_sink:
                    sk = sink_ref[head] * LOG2E
                    m = jnp.maximum(m, sk)
                p = jnp.exp2(s - m)
                l = jnp.sum(p, axis=-1, keepdims=True)
                if use_sink:
                    l = l + jnp.exp2(sk - m)
                o = jnp.dot(p.astype(jnp.bfloat16), vwin, preferred_element_type=jnp.float32) / l
                if emit_lse:
                    m_tiles[sb] = jnp.where(lane == hidx, m, m_tiles[sb])
                    l_tiles[sb] = jnp.where(lane == hidx, l, l_tiles[sb])
                o_ref[rows, hsl] = o.astype(o_ref.dtype)
    if emit_lse:
        for sb in range(nsub):
            lse_ref[sb * Q_SUB:(sb + 1) * Q_SUB, :] = (m_tiles[sb] + jnp.log2(l_tiles[sb])) * LN2


def attention_a(qkv, slopes, sink, segs):
    t = qkv.shape[0]
    nblk = t // Q_BLOCK
    bounds = _bounds(segs, 1)
    qw, kw = A_Q, A_KV
    k_col, v_col = A_Q // kw, (A_Q + A_KV) // kw
    prev = lambda b: jnp.maximum(b - 1, 0)
    nxt = lambda b: jnp.minimum(b + 1, nblk - 1)
    smem = pl.BlockSpec(memory_space=pltpu.SMEM)
    in_specs = [smem, smem,
                pl.BlockSpec((Q_BLOCK, qw), lambda b, h: (b, 0)),
                pl.BlockSpec((Q_BLOCK, kw), lambda b, h: (prev(b), k_col)),
                pl.BlockSpec((Q_BLOCK, kw), lambda b, h: (b, k_col)),
                pl.BlockSpec((Q_BLOCK, kw), lambda b, h: (nxt(b), k_col)),
                pl.BlockSpec((Q_BLOCK, kw), lambda b, h: (prev(b), v_col)),
                pl.BlockSpec((Q_BLOCK, kw), lambda b, h: (b, v_col)),
                pl.BlockSpec((Q_BLOCK, kw), lambda b, h: (nxt(b), v_col))]
    kern = functools.partial(_banded_kernel, half=A_HALF_WINDOW, nk=A_KV_HEADS, group=A_GROUP, step=1,
                             bounds=bounds, use_sink=True, emit_lse=False, row_axis=0, head_axis=1)
    return pl.pallas_call(
        kern,
        grid=(nblk, 1),
        in_specs=in_specs,
        out_specs=pl.BlockSpec((Q_BLOCK, qw), lambda b, h: (b, 0)),
        out_shape=jax.ShapeDtypeStruct((t, qw), jnp.bfloat16),
        compiler_params=_cparams(("parallel", "arbitrary")),
        name="attention_a",
    )(slopes, sink, *([qkv] * 7))


def attention_c_branch(z, slopes, dil, segs):
    rows = z.shape[1]
    nub = rows // Q_BLOCK
    half = C_HALF
    hw = C_HEAD_GROUP * HEAD_DIM
    bounds = _bounds(segs, dil)
    k_col0, v_col0 = C_W // hw, 2 * C_W // hw
    nhalf = rows // half
    prev = lambda u: jnp.maximum(2 * u - 1, 0)
    nxt = lambda u: jnp.minimum(2 * u + 2, nhalf - 1)
    smem = pl.BlockSpec(memory_space=pltpu.SMEM)
    in_specs = [smem,
                pl.BlockSpec((None, Q_BLOCK, hw), lambda r, u, h: (r, u, h)),
                pl.BlockSpec((None, half, hw), lambda r, u, h: (r, prev(u), k_col0 + h)),
                pl.BlockSpec((None, Q_BLOCK, hw), lambda r, u, h: (r, u, k_col0 + h)),
                pl.BlockSpec((None, half, hw), lambda r, u, h: (r, nxt(u), k_col0 + h)),
                pl.BlockSpec((None, half, hw), lambda r, u, h: (r, prev(u), v_col0 + h)),
                pl.BlockSpec((None, Q_BLOCK, hw), lambda r, u, h: (r, u, v_col0 + h)),
                pl.BlockSpec((None, half, hw), lambda r, u, h: (r, nxt(u), v_col0 + h))]
    kern = functools.partial(_banded_kernel, half=half, nk=C_HEAD_GROUP, group=1, step=dil, bounds=bounds,
                             use_sink=False, emit_lse=True, row_axis=1, head_axis=2)
    return pl.pallas_call(
        kern,
        grid=(dil, nub, C_GROUPS),
        in_specs=in_specs,
        out_specs=[pl.BlockSpec((None, Q_BLOCK, hw), lambda r, u, h: (r, u, h)),
                   pl.BlockSpec((None, Q_BLOCK, LANES), lambda r, u, h: (r, u, h))],
        out_shape=[jax.ShapeDtypeStruct((dil, rows, C_W), jnp.bfloat16),
                   jax.ShapeDtypeStruct((dil, rows, C_GROUPS * LANES), jnp.float32)],
        compiler_params=_cparams(("parallel", "parallel", "arbitrary")),
        name="attention_c_d%d" % dil,
    )(slopes, *([z] * 7))


def _plane_perm(dil, tm):
    p = np.zeros((tm, tm), np.float32)
    tok = np.arange(tm)
    p[tok, (tok % dil) * (tm // dil) + tok // dil] = 1.0
    return jnp.asarray(p, jnp.bfloat16)


def _split3(x):
    hi = x.astype(jnp.bfloat16)
    r = x - hi.astype(jnp.float32)
    mid = r.astype(jnp.bfloat16)
    lo = (r - mid.astype(jnp.float32)).astype(jnp.bfloat16)
    return hi, mid, lo


def _merge_kernel(*refs, dils):
    nb = len(dils)
    o_refs, l_refs = refs[:nb], refs[nb:2 * nb]
    p_refs = refs[2 * nb:2 * nb + nb - 1]
    out_ref = refs[-1]
    outs, lses = [o_refs[0][0].astype(jnp.float32)], [l_refs[0][0]]
    for b in range(1, nb):
        d = dils[b]
        perm = p_refs[b - 1][...]
        stacked = jnp.concatenate([o_refs[b][r] for r in range(d)], axis=0)
        outs.append(jnp.dot(perm, stacked, preferred_element_type=jnp.float32))
        lstack = jnp.concatenate([l_refs[b][r] for r in range(d)], axis=0)
        lses.append(sum(jnp.dot(perm, part, preferred_element_type=jnp.float32) for part in _split3(lstack)))
    for h in range(C_HEADS):
        hsl = slice(h * HEAD_DIM, (h + 1) * HEAD_DIM)
        col = (h // C_HEAD_GROUP) * LANES + h % C_HEAD_GROUP
        ls = [l[:, col:col + 1] for l in lses]
        m = functools.reduce(jnp.maximum, ls)
        es = [jnp.exp(l - m) for l in ls]
        tot = sum(es)
        acc = sum((e / tot) * o[:, hsl] for e, o in zip(es, outs))
        out_ref[:, hsl] = acc.astype(out_ref.dtype)


def merge_branches(os_, ls_, dils):
    t = os_[0].shape[0] * os_[0].shape[1]
    tm = min(MERGE_TILE, t)
    in_specs, args = [], []
    for arr, width in ((os_, C_W), (ls_, C_GROUPS * LANES)):
        for a, d in zip(arr, dils):
            in_specs.append(pl.BlockSpec((d, tm // d, width), lambda i: (0, i, 0)))
            args.append(a)
    for d in dils[1:]:
        in_specs.append(pl.BlockSpec((tm, tm), lambda i: (0, 0)))
        args.append(_plane_perm(d, tm))
    return pl.pallas_call(
        functools.partial(_merge_kernel, dils=tuple(dils)),
        grid=(t // tm,),
        in_specs=in_specs,
        out_specs=pl.BlockSpec((tm, C_W), lambda i: (i, 0)),
        out_shape=jax.ShapeDtypeStruct((t, C_W), jnp.bfloat16),
        compiler_params=_cparams(("parallel",)),
        name="merge_branches",
    )(*args)


def attention_c(zs, slopes, segs):
    dils = tuple(d for _, d in C_BRANCHES)
    res = [attention_c_branch(z, slopes, d, segs) for z, d in zip(zs, dils)]
    return merge_branches([r[0] for r in res], [r[1] for r in res], dils)


NA_PAIR = 2
NA_WIN_ROWS = NA_KH + NA_PAIR
NA_WIN_BLOCKS = NA_WIN_ROWS // NA_PAIR
NA_PAIR_CLASSES = ((0, 3, 1, 3), (0, 7, 0, 6), (0, 5, 0, 4), (2, 3, 2, 2), (2, 1, 2, 0))


def _natten_kernel(*refs):
    ng, nb = B_COL_GROUPS, NA_WIN_BLOCKS
    q_refs = refs[:ng]
    k_refs = [refs[ng + g * nb:ng + (g + 1) * nb] for g in range(ng)]
    v_refs = [refs[ng + (ng + g) * nb:ng + (ng + g + 1) * nb] for g in range(ng)]
    bias_ref, o_ref = refs[ng + 2 * ng * nb:]
    scale2 = HEAD_DIM ** -0.5 * LOG2E
    for h in range(B_HEADS):
        g, hl = divmod(h, B_HEAD_GROUP)
        hsl = slice(hl * HEAD_DIM, (hl + 1) * HEAD_DIM)
        kwin = jnp.concatenate([r[:, hsl] for r in k_refs[g]], axis=0)
        vwin = jnp.concatenate([r[:, hsl] for r in v_refs[g]], axis=0)
        s = lax.dot_general(q_refs[g][:, hsl], kwin, _NT, preferred_element_type=jnp.float32) * scale2 + bias_ref[h]
        m = jnp.max(s, axis=-1, keepdims=True)
        p = jnp.exp2(s - m)
        l = jnp.sum(p, axis=-1, keepdims=True)
        o = jnp.dot(p.astype(jnp.bfloat16), vwin, preferred_element_type=jnp.float32) / l
        o_ref[:, h * HEAD_DIM:(h + 1) * HEAD_DIM] = o.astype(o_ref.dtype)


def natten_bias_table(rpb):
    ncls = len(NA_PAIR_CLASSES)
    sel = np.zeros((ncls, NA_PAIR, NA_WIN_ROWS, 2 * NA_KH - 1), np.float32)
    row_ok = np.zeros((ncls, NA_PAIR, NA_WIN_ROWS), bool)
    for ci, cls in enumerate(NA_PAIR_CLASSES):
        for a in range(NA_PAIR):
            off, o = cls[2 * a], cls[2 * a + 1]
            for i in range(NA_KH):
                sel[ci, a, off + i, o + i] = 1.0
                row_ok[ci, a, off + i] = True
    c = jnp.arange(GRID_W)
    cs = jnp.clip(c - NA_KW // 2, 0, GRID_W - NA_KW)
    col_ok = (c[None, :] >= cs[:, None]) & (c[None, :] < cs[:, None] + NA_KW)
    dc = jnp.clip(c[None, :] - c[:, None] + (NA_KW - 1), 0, 2 * NA_KW - 2)
    pick = (dc[:, :, None] == jnp.arange(2 * NA_KW - 1)[None, None, :]).astype(jnp.float32)
    hi = lax.Precision.HIGHEST
    rows = jnp.einsum("capd,hdx->caphx", jnp.asarray(sel), rpb.astype(jnp.float32), precision=hi)
    tab = jnp.einsum("caphx,qkx->chaqpk", rows, pick, precision=hi)
    ok = jnp.asarray(row_ok)[:, None, :, None, :, None] & col_ok[None, None, None, :, None, :]
    tab = jnp.where(ok, tab * LOG2E, NEG_INF)
    return tab.reshape(ncls, B_HEADS, NA_PAIR * GRID_W, NA_WIN_ROWS * GRID_W)


def attention_b(qkv, bias_tab, segs):
    t = qkv.shape[0]
    nrows = t // GRID_W
    bounds = _bounds(segs, GRID_W)
    hw = B_HEAD_GROUP * HEAD_DIM
    ngroups = B_HEADS // B_HEAD_GROUP
    q_col0 = (A_Q + 2 * A_KV) // hw
    k_col0 = q_col0 + B_W // hw
    v_col0 = k_col0 + B_W // hw

    assert all(re - rs >= 2 * NA_KH and (re - rs) % NA_PAIR == 0 for rs, re in bounds)
    blk = NA_PAIR * GRID_W

    def win_block(pr):
        rs, re = _seg_bounds(pr * NA_PAIR, bounds)
        return jnp.clip(pr * NA_PAIR - NA_KH // 2, rs, re - NA_WIN_ROWS) // NA_PAIR

    def pair_class(pr):
        rs, re = _seg_bounds(pr * NA_PAIR, bounds)
        from_start, to_end = pr * NA_PAIR - rs, re - pr * NA_PAIR
        return jnp.where(from_start == 0, 1, jnp.where(from_start == 2, 2,
                         jnp.where(to_end == 4, 3, jnp.where(to_end == 2, 4, 0))))

    def kv_spec(col, i):
        return pl.BlockSpec((blk, hw), lambda pr: (win_block(pr) + i, col))

    assert ngroups == B_COL_GROUPS
    in_specs = [pl.BlockSpec((blk, hw), functools.partial(lambda pr, g: (pr, q_col0 + g), g=g))
                for g in range(ngroups)]
    in_specs += [kv_spec(k_col0 + g, i) for g in range(ngroups) for i in range(NA_WIN_BLOCKS)]
    in_specs += [kv_spec(v_col0 + g, i) for g in range(ngroups) for i in range(NA_WIN_BLOCKS)]
    in_specs.append(pl.BlockSpec((None, B_HEADS, blk, NA_WIN_ROWS * GRID_W), lambda pr: (pair_class(pr), 0, 0, 0)))
    return pl.pallas_call(
        _natten_kernel,
        grid=(nrows // NA_PAIR,),
        in_specs=in_specs,
        out_specs=pl.BlockSpec((blk, B_W), lambda pr: (pr, 0)),
        out_shape=jax.ShapeDtypeStruct((t, B_W), jnp.bfloat16),
        compiler_params=_cparams(("parallel",)),
        name="attention_b",
    )(*([qkv] * (ngroups * (1 + 2 * NA_WIN_BLOCKS))), bias_tab)


def _top2(vals):
    v1, i1 = vals[0], jnp.zeros(vals[0].shape, jnp.int32)
    for j in range(1, len(vals)):
        better = vals[j] > v1
        v1 = jnp.where(better, vals[j], v1)
        i1 = jnp.where(better, j, i1)
    v2, i2 = None, None
    for j in range(len(vals)):
        cand = jnp.where(i1 == j, -1.0, vals[j])
        if v2 is None:
            v2, i2 = cand, jnp.zeros(vals[0].shape, jnp.int32)
        else:
            better = cand > v2
            v2 = jnp.where(better, cand, v2)
            i2 = jnp.where(better, j, i2)
    return v1, i1, v2, i2


def _router_kernel(x_ref, w2_ref, whi_ref, b_ref, idx_ref, gate_ref):
    x = x_ref[...]
    x_hi = x.astype(jnp.bfloat16)
    x_lo = (x - x_hi.astype(jnp.float32)).astype(jnp.bfloat16)
    r1 = lax.dot_general(w2_ref[...], x_hi, _NT, preferred_element_type=jnp.float32)
    r2 = lax.dot_general(whi_ref[...], x_lo, _NT, preferred_element_type=jnp.float32)
    logits = r1[:N_EXPERTS] + r1[N_EXPERTS:] + r2 + b_ref[...]
    m = jnp.max(logits, axis=0, keepdims=True)
    ex = jnp.exp(logits - m)
    probs = ex / jnp.sum(ex, axis=0, keepdims=True)
    rows = [probs[e:e + 1, :] for e in range(N_EXPERTS)]
    tops = [_top2(rows[g * EXPERTS_PER_GROUP:(g + 1) * EXPERTS_PER_GROUP]) for g in range(N_GROUPS)]
    score = [tp[0] + tp[2] for tp in tops]
    best, gsel = score[0], jnp.zeros(score[0].shape, jnp.int32)
    for g in range(1, N_GROUPS):
        better = score[g] > best
        best = jnp.where(better, score[g], best)
        gsel = jnp.where(better, g, gsel)
    v1, i1, v2, i2 = tops[0]
    for g in range(1, N_GROUPS):
        pick = gsel == g
        v1 = jnp.where(pick, tops[g][0], v1)
        i1 = jnp.where(pick, tops[g][1], i1)
        v2 = jnp.where(pick, tops[g][2], v2)
        i2 = jnp.where(pick, tops[g][3], i2)
    tot = v1 + v2
    idx_ref[0:1, :] = gsel * EXPERTS_PER_GROUP + i1
    idx_ref[1:2, :] = gsel * EXPERTS_PER_GROUP + i2
    gate_ref[0:1, :] = v1 / tot
    gate_ref[1:2, :] = v2 / tot


def router(x, router_w, router_b, tm=512):
    t, d = x.shape
    tm = min(tm, t)
    wt = router_w.astype(jnp.float32).T
    w_hi = wt.astype(jnp.bfloat16)
    w_lo = (wt - w_hi.astype(jnp.float32)).astype(jnp.bfloat16)
    w2 = jnp.concatenate([w_hi, w_lo], axis=0)
    return pl.pallas_call(
        _router_kernel,
        grid=(t // tm,),
        in_specs=[pl.BlockSpec((tm, d), lambda i: (i, 0)),
                  pl.BlockSpec((2 * N_EXPERTS, d), lambda i: (0, 0)),
                  pl.BlockSpec((N_EXPERTS, d), lambda i: (0, 0)),
                  pl.BlockSpec((N_EXPERTS, 1), lambda i: (0, 0))],
        out_specs=[pl.BlockSpec((2, tm), lambda i: (0, i))] * 2,
        out_shape=[jax.ShapeDtypeStruct((2, t), jnp.int32), jax.ShapeDtypeStruct((2, t), jnp.float32)],
        compiler_params=_cparams(("parallel",)),
        name="router",
    )(x, w2, w_hi, router_b.astype(jnp.float32).reshape(N_EXPERTS, 1))


def dispatch_plan(idx):
    _, t = idx.shape
    nt = t // MOE_TILE
    n_chunks = _round_up(nt * CHUNKS_PER_TILE + N_EXPERTS * (CHUNKS_PER_EXPERT_TILE - 1), CHUNKS_PER_EXPERT_TILE)
    i32 = jnp.int32
    e_tile = idx.reshape(2, nt, MOE_TILE).transpose(1, 0, 2).reshape(nt, 2 * MOE_TILE)
    hit = e_tile[:, :, None] == jnp.arange(N_EXPERTS, dtype=i32)[None, None, :]
    onehot = hit.astype(i32)
    tri = (jnp.arange(2 * MOE_TILE)[:, None] >= jnp.arange(2 * MOE_TILE)[None, :]).astype(jnp.bfloat16)
    csum = jnp.einsum("ab,nbe->nae", tri, hit.astype(jnp.bfloat16), preferred_element_type=jnp.float32).astype(i32)
    rank = jnp.sum(csum * onehot, axis=-1) - 1
    cnt = csum[:, -1, :]
    nc = (cnt + SLOT_CHUNK - 1) // SLOT_CHUNK
    lo = jnp.cumsum(nc, axis=1) - nc
    local_pos = jnp.sum(onehot * (lo * SLOT_CHUNK)[:, None, :], axis=-1) + rank
    pos = local_pos.reshape(nt, 2, MOE_TILE).transpose(1, 0, 2).reshape(2, t).astype(i32)

    nc_t = nc.T
    e_chunks = nc_t.sum(axis=1)
    e_pad = _round_up(e_chunks, CHUNKS_PER_EXPERT_TILE)
    e_end = jnp.cumsum(e_pad)
    run_start = (e_end - e_pad)[:, None] + jnp.cumsum(nc_t, axis=1) - nc_t
    src0 = jnp.arange(nt, dtype=i32)[None, :] * CHUNKS_PER_TILE + lo.T
    rs_f, nc_f, src0_f = run_start.reshape(1, -1), nc_t.reshape(1, -1), src0.reshape(1, -1)
    d = jnp.arange(n_chunks, dtype=i32)[:, None]
    in_run = (d >= rs_f) & (d < rs_f + nc_f)
    fwd = jnp.sum(jnp.where(in_run, src0_f + d - rs_f, 0), axis=1).astype(i32)

    lc = jnp.arange(CHUNKS_PER_TILE, dtype=i32)[None, :, None]
    lo_b, nc_b = lo[:, None, :], nc[:, None, :]
    in_run_b = (lc >= lo_b) & (lc < lo_b + nc_b)
    back = jnp.sum(jnp.where(in_run_b, run_start.T[:, None, :] + lc - lo_b, 0), axis=-1)
    back = back.reshape(nt * CHUNKS_PER_TILE).astype(i32)

    n_tiles = n_chunks // CHUNKS_PER_EXPERT_TILE
    tile_first = jnp.arange(n_tiles, dtype=i32) * CHUNKS_PER_EXPERT_TILE
    tile_expert = jnp.minimum(jnp.sum((e_end[None, :] <= tile_first[:, None]).astype(i32), axis=1),
                              N_EXPERTS - 1).astype(i32)
    n_active = (e_end[-1] // CHUNKS_PER_EXPERT_TILE).astype(i32).reshape(1)
    return pos, fwd, back, tile_expert, n_active


def _permute_kernel(x_ref, pos_ref, gate_ref, xs_ref, gl_ref):
    tt, d = x_ref.shape
    slot = lax.broadcasted_iota(jnp.int32, (LOCAL_SLOTS, tt), 0)
    hit1 = slot == pos_ref[0:1, :]
    hit2 = slot == pos_ref[1:2, :]
    perm = jnp.where(hit1 | hit2, 1.0, 0.0).astype(jnp.bfloat16)
    cw = min(d, 512)
    for n in range(d // cw):
        sl = slice(n * cw, (n + 1) * cw)
        xs_ref[:, sl] = jnp.dot(perm, x_ref[:, sl], preferred_element_type=jnp.float32).astype(xs_ref.dtype)
    gates = jnp.where(hit1, gate_ref[0:1, :], 0.0) + jnp.where(hit2, gate_ref[1:2, :], 0.0)
    gl_ref[...] = jnp.broadcast_to(jnp.sum(gates, axis=1, keepdims=True), gl_ref.shape)


def permute_tokens(x_bf, pos, gate):
    t, d = x_bf.shape
    nt = t // MOE_TILE
    return pl.pallas_call(
        _permute_kernel,
        grid=(nt,),
        in_specs=[pl.BlockSpec((MOE_TILE, d), lambda i: (i, 0)),
                  pl.BlockSpec((2, MOE_TILE), lambda i: (0, i)),
                  pl.BlockSpec((2, MOE_TILE), lambda i: (0, i))],
        out_specs=[pl.BlockSpec((LOCAL_SLOTS, d), lambda i: (i, 0)),
                   pl.BlockSpec((LOCAL_SLOTS, LANES), lambda i: (i, 0))],
        out_shape=[jax.ShapeDtypeStruct((nt * LOCAL_SLOTS, d), jnp.bfloat16),
                   jax.ShapeDtypeStruct((nt * LOCAL_SLOTS, LANES), jnp.float32)],
        compiler_params=_cparams(("parallel",)),
        name="permute_tokens",
    )(x_bf, pos, gate)


def _unpermute_kernel(back_ref, ys_ref, gl_ref, p1_ref, p2_ref, y_ref, buf_ref, sems):
    i = pl.program_id(0)
    tt, d = y_ref.shape
    n = CHUNKS_PER_TILE
    buf_slot = i % 2

    @pl.when(i == 0)
    def _():
        _fetch_chunks(ys_ref, back_ref, 0, n, buf_ref, 0, sems)

    @pl.when(i + 1 < pl.num_programs(0))
    def _():
        _fetch_chunks(ys_ref, back_ref, (i + 1) * n, n, buf_ref, 1 - buf_slot, sems)

    slot = lax.broadcasted_iota(jnp.int32, (tt, LOCAL_SLOTS), 1)
    hit = (slot == p1_ref[:, 0:1]) | (slot == p2_ref[:, 0:1])
    perm = jnp.where(hit, 1.0, 0.0).astype(jnp.bfloat16)
    g = gl_ref[:, 0:1]
    _wait_chunks(ys_ref, n, buf_ref, buf_slot, sems)
    cw = min(d, 512)
    for c in range(d // cw):
        sl = slice(c * cw, (c + 1) * cw)
        scaled = (buf_ref[buf_slot, :, sl].astype(jnp.float32) * g).astype(jnp.bfloat16)
        y_ref[:, sl] = jnp.dot(perm, scaled, preferred_element_type=jnp.float32)


UNPERMUTE_SPLIT = 2


def _unpermute_ln_kernel(back_ref, ys_ref, gl_ref, p1_ref, p2_ref, x_ref, g_ref, b_ref, o_ref, obf_ref,
                         buf_ref, sems):
    i, part = pl.program_id(0), pl.program_id(1)
    rows, d = o_ref.shape
    n = CHUNKS_PER_TILE
    buf_slot = i % 2
    cw = min(d, 512)

    @pl.when((i == 0) & (part == 0))
    def _():
        _fetch_chunks(ys_ref, back_ref, 0, n, buf_ref, 0, sems)

    @pl.when((part == 0) & (i + 1 < pl.num_programs(0)))
    def _():
        _fetch_chunks(ys_ref, back_ref, (i + 1) * n, n, buf_ref, 1 - buf_slot, sems)

    @pl.when(part == 0)
    def _():
        _wait_chunks(ys_ref, n, buf_ref, buf_slot, sems)
        gate = gl_ref[:, 0:1]
        for c in range(d // cw):
            sl = slice(c * cw, (c + 1) * cw)
            buf_ref[buf_slot, :, sl] = (buf_ref[buf_slot, :, sl].astype(jnp.float32) * gate).astype(buf_ref.dtype)

    slot = lax.broadcasted_iota(jnp.int32, (rows, LOCAL_SLOTS), 1)
    hit = (slot == p1_ref[:, 0:1]) | (slot == p2_ref[:, 0:1])
    perm = jnp.where(hit, 1.0, 0.0).astype(jnp.bfloat16)
    for c in range(d // cw):
        sl = slice(c * cw, (c + 1) * cw)
        o_ref[:, sl] = jnp.dot(perm, buf_ref[buf_slot, :, sl], preferred_element_type=jnp.float32)
    out = _layer_norm_rows(DEEPNORM_ALPHA * x_ref[...] + o_ref[...], g_ref[...], b_ref[...])
    o_ref[...] = out
    obf_ref[...] = out.astype(jnp.bfloat16)


def unpermute_resid_ln(ys, back, gate_local, pos, x, g, b):
    t, d = x.shape
    nt = t // MOE_TILE
    rows = MOE_TILE // UNPERMUTE_SPLIT
    p1 = jnp.broadcast_to(pos[0][:, None], (t, LANES))
    p2 = jnp.broadcast_to(pos[1][:, None], (t, LANES))
    row = lambda width: pl.BlockSpec((rows, width), lambda i, p, bk: (i * UNPERMUTE_SPLIT + p, 0))
    vec = pl.BlockSpec((1, d), lambda i, p, bk: (0, 0))
    return pl.pallas_call(
        _unpermute_ln_kernel,
        grid_spec=pltpu.PrefetchScalarGridSpec(
            num_scalar_prefetch=1,
            grid=(nt, UNPERMUTE_SPLIT),
            in_specs=[pl.BlockSpec(memory_space=pl.ANY),
                      pl.BlockSpec((LOCAL_SLOTS, LANES), lambda i, p, bk: (i, 0)),
                      row(LANES), row(LANES), row(d), vec, vec],
            out_specs=[row(d), row(d)],
            scratch_shapes=[pltpu.VMEM((2, LOCAL_SLOTS, d), ys.dtype), pltpu.SemaphoreType.DMA((2,))]),
        out_shape=[jax.ShapeDtypeStruct((t, d), jnp.float32), jax.ShapeDtypeStruct((t, d), jnp.bfloat16)],
        compiler_params=_cparams(("arbitrary", "arbitrary")),
        name="unpermute_resid_ln",
    )(back, ys.reshape(ys.shape[0] // SLOT_CHUNK, SLOT_CHUNK, d), gate_local, p1, p2, x,
      g.reshape(1, d), b.reshape(1, d))


def unpermute_tokens(ys, back, gate_local, pos):
    _, t = pos.shape
    d = ys.shape[1]
    nt = t // MOE_TILE
    p1 = jnp.broadcast_to(pos[0][:, None], (t, LANES))
    p2 = jnp.broadcast_to(pos[1][:, None], (t, LANES))
    return pl.pallas_call(
        _unpermute_kernel,
        grid_spec=pltpu.PrefetchScalarGridSpec(
            num_scalar_prefetch=1,
            grid=(nt,),
            in_specs=[pl.BlockSpec(memory_space=pl.ANY),
                      pl.BlockSpec((LOCAL_SLOTS, LANES), lambda i, bk: (i, 0)),
                      pl.BlockSpec((MOE_TILE, LANES), lambda i, bk: (i, 0)),
                      pl.BlockSpec((MOE_TILE, LANES), lambda i, bk: (i, 0))],
            out_specs=pl.BlockSpec((MOE_TILE, d), lambda i, bk: (i, 0)),
            scratch_shapes=[pltpu.VMEM((2, LOCAL_SLOTS, d), ys.dtype), pltpu.SemaphoreType.DMA((2,))]),
        out_shape=jax.ShapeDtypeStruct((t, d), jnp.float32),
        compiler_params=_cparams(("arbitrary",)),
        name="unpermute_tokens",
    )(back, ys.reshape(ys.shape[0] // SLOT_CHUNK, SLOT_CHUNK, d), gate_local, p1, p2)


def _chunk_copy(src_ref, chunk, buf_ref, slot, g, sems):
    return pltpu.make_async_copy(src_ref.at[chunk], buf_ref.at[slot, pl.ds(g * SLOT_CHUNK, SLOT_CHUNK)],
                                 sems.at[slot])


def _fetch_chunks(src_ref, idx_ref, first, n, buf_ref, slot, sems):
    for g in range(n):
        _chunk_copy(src_ref, idx_ref[first + g], buf_ref, slot, g, sems).start()


def _wait_chunks(src_ref, n, buf_ref, slot, sems):
    for g in range(n):
        _chunk_copy(src_ref, 0, buf_ref, slot, g, sems).wait()


def _expert_up_kernel(te_ref, na_ref, fwd_ref, xs1_ref, w1_ref, w3_ref, h_ref, buf_ref, sems):
    j, i = pl.program_id(0), pl.program_id(1)
    na = na_ref[0]
    active = i < na

    @pl.when(active)
    def _():
        a = j * na + i
        slot = a % 2
        n = CHUNKS_PER_EXPERT_TILE

        @pl.when(a == 0)
        def _():
            _fetch_chunks(xs1_ref, fwd_ref, 0, n, buf_ref, 0, sems)

        @pl.when(a + 1 < pl.num_programs(0) * na)
        def _():
            nxt = jnp.where(i + 1 < na, i + 1, 0)
            _fetch_chunks(xs1_ref, fwd_ref, nxt * n, n, buf_ref, 1 - slot, sems)

        _wait_chunks(xs1_ref, n, buf_ref, slot, sems)
        x = buf_ref[slot]
        h1 = jnp.dot(x, w1_ref[...].astype(jnp.bfloat16), preferred_element_type=jnp.float32)
        h3 = jnp.dot(x, w3_ref[...].astype(jnp.bfloat16), preferred_element_type=jnp.float32)
        h_ref[...] = (jax.nn.silu(h1) * h3).astype(h_ref.dtype)

    @pl.when(jnp.logical_not(active))
    def _():
        h_ref[...] = jnp.zeros(h_ref.shape, h_ref.dtype)


def _expert_down_kernel(te_ref, na_ref, h_ref, w2_ref, y_ref):
    active = pl.program_id(1) < na_ref[0]

    @pl.when(active)
    def _():
        y_ref[...] = jnp.dot(h_ref[...], w2_ref[...].astype(jnp.bfloat16),
                             preferred_element_type=jnp.float32).astype(y_ref.dtype)

    @pl.when(jnp.logical_not(active))
    def _():
        y_ref[...] = jnp.zeros(y_ref.shape, y_ref.dtype)


def expert_ffn(xs1, fwd, w1, w3, w2, layer, tile_expert, n_active, tf=512, tn=4096):
    d = xs1.shape[1]
    f = w1.shape[3]
    tm = EXPERT_TILE
    tf, tn = min(tf, f), min(tn, d)
    n_tiles = fwd.shape[0] // CHUNKS_PER_EXPERT_TILE
    s = n_tiles * tm

    def act(i, na):
        return jnp.minimum(i, na[0] - 1)

    up = pl.pallas_call(
        _expert_up_kernel,
        grid_spec=pltpu.PrefetchScalarGridSpec(
            num_scalar_prefetch=3,
            grid=(f // tf, n_tiles),
            in_specs=[pl.BlockSpec(memory_space=pl.ANY),
                      pl.BlockSpec((None, None, d, tf), lambda j, i, te, na, fw: (layer, te[act(i, na)], 0, j)),
                      pl.BlockSpec((None, None, d, tf), lambda j, i, te, na, fw: (layer, te[act(i, na)], 0, j))],
            out_specs=pl.BlockSpec((tm, tf), lambda j, i, te, na, fw: (i, j)),
            scratch_shapes=[pltpu.VMEM((2, tm, d), xs1.dtype), pltpu.SemaphoreType.DMA((2,))]),
        out_shape=jax.ShapeDtypeStruct((s, f), jnp.bfloat16),
        compiler_params=_cparams(("arbitrary", "arbitrary")),
        name="expert_up",
    )
    h = up(tile_expert, n_active, fwd, xs1.reshape(xs1.shape[0] // SLOT_CHUNK, SLOT_CHUNK, d), w1, w3)
    down = pl.pallas_call(
        _expert_down_kernel,
        grid_spec=pltpu.PrefetchScalarGridSpec(
            num_scalar_prefetch=2,
            grid=(d // tn, n_tiles),
            in_specs=[pl.BlockSpec((tm, f), lambda j, i, te, na: (act(i, na), 0)),
                      pl.BlockSpec((None, None, f, tn), lambda j, i, te, na: (layer, te[act(i, na)], 0, j))],
            out_specs=pl.BlockSpec((tm, tn), lambda j, i, te, na: (i, j))),
        out_shape=jax.ShapeDtypeStruct((s, d), jnp.bfloat16),
        compiler_params=_cparams(("arbitrary", "arbitrary")),
        name="expert_down",
    )
    return down(tile_expert, n_active, h, w2)


def moe_layer(x, x_bf, router_w, router_b, w1, w3, w2, layer, g, b, out_rows=None):
    idx, gate = router(x, router_w, router_b)
    pos, fwd, back, tile_expert, n_active = dispatch_plan(idx)
    xs1, gate_local = permute_tokens(x_bf, pos, gate)
    ys = expert_ffn(xs1, fwd, w1, w3, w2, layer, tile_expert, n_active)
    if out_rows is None:
        return unpermute_resid_ln(ys, back, gate_local, pos, x, g, b)
    y = unpermute_tokens(ys, back, gate_local, pos)
    return [resid_ln(x, y, g, b, rows=r)[0] for r in out_rows]


def alibi_slopes(n):
    return jnp.exp2(-8.0 * jnp.arange(1, n + 1, dtype=jnp.float32) / n)


def trunk(x, x_bf, segs, out_rows, w_in_ab, w_out_ab, a_sink, b_rpb, w_in_c, w_out_c, router_w, router_b,
          moe_w1, moe_w3, moe_w2, ln_g, ln_b):
    bf = jnp.bfloat16
    slopes_a = alibi_slopes(A_HEADS)
    slopes_c = alibi_slopes(C_HEADS)
    dils = tuple(d for _, d in C_BRANCHES)
    for layer in range(DEPTH):
        i = layer // 2
        if layer % 2 == 0:
            qkv = matmul([x_bf], w_in_ab[i].astype(bf), bf)
            oa = attention_a(qkv, slopes_a, a_sink[i].astype(jnp.float32), segs)
            ob = attention_b(qkv, natten_bias_table(b_rpb[i]), segs)
            mixed, w_out = [oa, ob], w_out_ab[i]
        else:
            zs = matmul_dilated(x_bf, w_in_c[i].astype(bf), dils[1:])
            zs = [zs[0].reshape((1,) + zs[0].shape)] + list(zs[1:])
            mixed, w_out = [attention_c(zs, slopes_c, segs)], w_out_c[i]
        mix = matmul(mixed, w_out.astype(bf), jnp.float32)
        x, x_bf = resid_ln(x, mix, ln_g[layer, 0], ln_b[layer, 0])
        last = layer == DEPTH - 1
        res = moe_layer(x, x_bf, router_w, router_b, moe_w1, moe_w3, moe_w2, layer,
                        ln_g[layer, 1], ln_b[layer, 1], out_rows=out_rows if last else None)
        if last:
            return res
        x, x_bf = res


def kernel(x_prompt, x_sample, w_in_ab, w_out_ab, a_sink, b_rpb, w_in_c, w_out_c, router_w, router_b,
           moe_w1, moe_w3, moe_w2, ln_g, ln_b):
    bp, sp, d = x_prompt.shape
    bs, ss, _ = x_sample.shape
    segs = ((bp, sp), (bs, ss))
    x, x_bf = concat_cast(x_prompt.reshape(bp * sp, d), x_sample.reshape(bs * ss, d))
    out_rows = ((0, bp * sp), (bp * sp, bs * ss))
    yp, ys = trunk(x, x_bf, segs, out_rows, w_in_ab, w_out_ab, a_sink, b_rpb, w_in_c, w_out_c, router_w, router_b,
                   moe_w1, moe_w3, moe_w2, ln_g, ln_b)
    return (yp.reshape(bp, sp, d), ys.reshape(bs, ss, d))
```
